```python
import math
import jax, jax.numpy as jnp
from jax import lax
import numpy as np

D_MODEL = 2048
BATCH = 32
SEQ = 256
DEPTH = 4
DEC_BATCH = 4
DEC_SEQ = 1024
PAST_LEN = 512

GRID_W = 64
EPS = 1e-6
NEG_INF = -1e30
CTX_QBLOCK = 128
H_A = D_MODEL // 256
HD_A = 64
W_A = H_A * HD_A
NA_ROWS = 8
NA_COLS = 16
NA_QCOLS = 16
H_B = D_MODEL // 512
DK_B = 64
DV_B = 128
W_B = H_B * DV_B
GLA_RANK = 16
GLA_TAU = 16.0
GLA_CHUNK = 64
HQ_C = D_MODEL // 128
HKV_C = HQ_C // 4
G_C = HQ_C // HKV_C
HD_C = 64
W_C = HQ_C * HD_C
SWA_WIN = 128
SWA_BLOCK = 128
ROPE_BASE = 10000.0
D_FF = 11 * D_MODEL // 4
N_EXPERTS = 8
TOP_K = 2
MOE_FF = D_FF // 2
N_DENSE = (DEPTH + 1) // 2
N_MOE = DEPTH // 2
SPLIT_SIZES = (W_A, W_A, W_A, H_B * DK_B, H_B * DK_B, W_B, W_B, GLA_RANK, GLA_RANK, W_C, HKV_C * HD_C, HKV_C * HD_C)
SPLIT_POINTS = tuple(int(s) for s in np.cumsum(SPLIT_SIZES)[:-1])
D_IN = int(sum(SPLIT_SIZES))

kernel_name = "hybrid_na_gla_swa_prefix_diffusion_step"

F32 = jnp.float32


def rmsnorm(x, g):
    xf = x.astype(F32)
    y = xf * lax.rsqrt(jnp.mean(xf * xf, axis=-1, keepdims=True) + EPS)
    return (y * g.astype(F32)).astype(x.dtype)


def adaln_params(cond, w_ada, b_ada):
    m = jax.nn.silu(cond) @ w_ada + b_ada
    return [t[:, None, :] for t in jnp.split(m, 6, axis=-1)]


def joint_softmax(scores, sink=None):
    m = scores[0].max(-1, keepdims=True)
    for s in scores[1:]:
        m = jnp.maximum(m, s.max(-1, keepdims=True))
    if sink is not None:
        m = jnp.maximum(m, sink)
    exps = [jnp.exp(s - m) for s in scores]
    den = exps[0].sum(-1, keepdims=True)
    for e in exps[1:]:
        den = den + e.sum(-1, keepdims=True)
    if sink is not None:
        den = den + jnp.exp(sink - m)
    return [e / den for e in exps]


def context_self_attention(q, k, v, sink=None):
    bn, L, hk, g, d = q.shape
    nb = L // CTX_QBLOCK
    scale = d ** -0.5
    qb = jnp.moveaxis(q.reshape(bn, nb, CTX_QBLOCK, hk, g, d), 1, 0)
    sk = None if sink is None else sink.astype(F32)[None, :, :, None, None]

    def block(qi):
        s = jnp.einsum('bqkgd,bskd->bkgqs', qi, k, preferred_element_type=F32) * scale
        (p,) = joint_softmax((s,), sk)
        return jnp.einsum('bkgqs,bskd->bqkgd', p.astype(v.dtype), v)

    o = lax.map(block, qb)
    return jnp.moveaxis(o, 0, 1).reshape(bn, L, hk * g * d)


def neighbourhood_attention(q, k, v, k_ctx, v_ctx, rpb):
    bn, t, h, d = q.shape
    rows = t // GRID_W
    kh = min(NA_ROWS, rows)
    ncb = GRID_W // NA_QCOLS
    span = NA_QCOLS + NA_COLS
    r = np.arange(rows)
    row_idx = np.clip(r - kh // 2, 0, rows - kh)[:, None] + np.arange(kh)[None, :]
    col = np.arange(GRID_W)
    col_start = np.clip(col - NA_COLS // 2, 0, GRID_W - NA_COLS).reshape(ncb, NA_QCOLS)
    span_idx = np.clip(col_start[:, 0], 0, GRID_W - span)[:, None] + np.arange(span)[None, :]
    qcol = col.reshape(ncb, NA_QCOLS)
    in_win = (span_idx[:, None, :] >= col_start[:, :, None]) & (span_idx[:, None, :] < col_start[:, :, None] + NA_COLS)
    dr = row_idx - r[:, None] + NA_ROWS - 1
    dc = np.clip(span_idx[:, None, :] - qcol[:, :, None] + NA_COLS - 1, 0, 2 * NA_COLS - 2)
    bias = rpb.astype(F32)[:, dr[:, None, None, :, None], dc[None, :, :, None, :]]
    bias = jnp.where(in_win[None, None, :, :, None, :], bias, NEG_INF)
    kg = k.reshape(bn, rows, GRID_W, h, d)[:, row_idx][:, :, :, span_idx]
    vg = v.reshape(bn, rows, GRID_W, h, d)[:, row_idx][:, :, :, span_idx]
    qg = q.reshape(bn, rows, ncb, NA_QCOLS, h, d)
    scale = d ** -0.5
    s_win = jnp.einsum('brjqhd,brijshd->bhrjqis', qg, kg, preferred_element_type=F32) * scale + bias[None]
    s_win = s_win.reshape(bn, h, rows, ncb, NA_QCOLS, kh * span)
    s_ctx = jnp.einsum('brjqhd,bshd->bhrjqs', qg, k_ctx, preferred_element_type=F32) * scale
    p_win, p_ctx = joint_softmax((s_win, s_ctx))
    p_win = p_win.reshape(bn, h, rows, ncb, NA_QCOLS, kh, span).astype(v.dtype)
    o = (jnp.einsum('bhrjqis,brijshd->brjqhd', p_win, vg)
         + jnp.einsum('bhrjqs,bshd->brjqhd', p_ctx.astype(v.dtype), v_ctx))
    return o.reshape(bn, t, h * d)


def axial_rope(x):
    t, d = x.shape[1], x.shape[-1]
    half, quarter = d // 2, d // 4
    pos = jnp.arange(t)
    freqs = ROPE_BASE ** (-jnp.arange(quarter, dtype=F32) / quarter)
    shape = (t,) + (1,) * (x.ndim - 3) + (quarter,)

    def rotate(xh, p):
        ang = (p.astype(F32)[:, None] * freqs[None, :]).reshape(shape)
        cos, sin = jnp.cos(ang), jnp.sin(ang)
        x1, x2 = xh[..., :quarter], xh[..., quarter:]
        return jnp.concatenate([x1 * cos - x2 * sin, x1 * sin + x2 * cos], axis=-1)

    xf = x.astype(F32)
    out = jnp.concatenate([rotate(xf[..., :half], pos // GRID_W), rotate(xf[..., half:], pos % GRID_W)], axis=-1)
    return out.astype(x.dtype)


def window_attention(q, k, v, k_ctx, v_ctx, sink):
    bn, t, hk, g, d = q.shape
    nb = t // SWA_BLOCK
    idx = np.arange(nb)[:, None] * SWA_BLOCK + np.arange(3 * SWA_BLOCK)[None, :]
    kpos = idx - SWA_BLOCK
    qpos = np.arange(t).reshape(nb, SWA_BLOCK)
    valid = ((np.abs(qpos[:, :, None] - kpos[:, None, :]) <= SWA_WIN)
             & (kpos[:, None, :] >= 0) & (kpos[:, None, :] < t))
    pad = ((0, 0), (SWA_BLOCK, SWA_BLOCK), (0, 0), (0, 0))
    kb = jnp.pad(k, pad)[:, idx]
    vb = jnp.pad(v, pad)[:, idx]
    qb = q.reshape(bn, nb, SWA_BLOCK, hk, g, d)
    scale = d ** -0.5
    s_loc = jnp.einsum('bnqkgd,bnskd->bkgnqs', qb, kb, preferred_element_type=F32) * scale
    s_loc = jnp.where(valid, s_loc, NEG_INF)
    s_ctx = jnp.einsum('bnqkgd,bskd->bkgnqs', qb, k_ctx, preferred_element_type=F32) * scale
    sk = sink.astype(F32)[None, :, :, None, None, None]
    p_loc, p_ctx = joint_softmax((s_loc, s_ctx), sk)
    o = (jnp.einsum('bkgnqs,bnskd->bnqkgd', p_loc.astype(v.dtype), vb)
         + jnp.einsum('bkgnqs,bskd->bnqkgd', p_ctx.astype(v.dtype), v_ctx))
    return o.reshape(bn, t, hk * g * d)


def gla_scan(q, k, v, log_a, s0):
    bn, L, h, dk = q.shape
    dv = v.shape[-1]
    nc = L // GLA_CHUNK

    def chunks(x):
        return jnp.moveaxis(x.astype(F32).reshape(bn, nc, GLA_CHUNK, h, x.shape[-1]), 1, 0).transpose(0, 1, 3, 2, 4)

    tril = jnp.tril(jnp.ones((GLA_CHUNK, GLA_CHUNK), dtype=bool))

    def step(s, inp):
        qc, kc, vc, ac = inp
        b = jnp.cumsum(ac, axis=-2)
        b_end = b[:, :, -1:, :]
        q_t = qc * jnp.exp(b)
        k_t = kc * jnp.exp(-b)
        a = jnp.where(tril, jnp.einsum('bhtk,bhsk->bhts', q_t, k_t), 0.0)
        o = jnp.einsum('bhtk,bhkv->bhtv', q_t, s) + jnp.einsum('bhts,bhsv->bhtv', a, vc)
        s_new = (jnp.exp(b_end[:, :, 0, :, None]) * s
                 + jnp.einsum('bhsk,bhsv->bhkv', kc * jnp.exp(b_end - b), vc))
        return s_new, o

    s_fin, o = lax.scan(step, s0.astype(F32), (chunks(q), chunks(k), chunks(v), chunks(log_a)))
    o = o.transpose(1, 0, 3, 2, 4).reshape(bn, L, h, dv)
    return o.astype(v.dtype), s_fin.astype(v.dtype)


def bidirectional_gla(q, k, v, a_fwd, a_bwd, s_fwd, s_bwd):
    o_f, sf = gla_scan(q, k, v, a_fwd, s_fwd)
    flip = lambda x: jnp.flip(x, axis=1)
    o_b, sb = gla_scan(flip(q), flip(k), flip(v), flip(a_bwd), s_bwd)
    return o_f + flip(o_b), sf, sb


def gla_output(o, r, g):
    bn, L = o.shape[:2]
    of = o.astype(F32)
    of = of * lax.rsqrt(jnp.mean(of * of, axis=-1, keepdims=True) + EPS) * g.astype(F32)
    return (of.reshape(bn, L, W_B) * jax.nn.silu(r.astype(F32))).astype(r.dtype)


def mixer_inputs(u, gate_w, gate_b):
    bn, L = u.shape[:2]
    qa, ka, va, qb, kb, vb, rb, zf, zb, qc, kc, vc = jnp.split(u, SPLIT_POINTS, axis=-1)
    grp_a = (qa.reshape(bn, L, H_A, 1, HD_A), ka.reshape(bn, L, H_A, HD_A), va.reshape(bn, L, H_A, HD_A))
    log_a = [(jax.nn.log_sigmoid((z @ gate_w[i] + gate_b[i]).astype(F32)) / GLA_TAU).reshape(bn, L, H_B, DK_B)
             for i, z in enumerate((zf, zb))]
    grp_b = (qb.reshape(bn, L, H_B, DK_B) * (DK_B ** -0.5), kb.reshape(bn, L, H_B, DK_B),
             vb.reshape(bn, L, H_B, DV_B), rb, log_a[0], log_a[1])
    grp_c = (qc.reshape(bn, L, HKV_C, G_C, HD_C), kc.reshape(bn, L, HKV_C, HD_C), vc.reshape(bn, L, HKV_C, HD_C))
    return grp_a, grp_b, grp_c


def swiglu(h, w_gate, w_up, w_down):
    return (jax.nn.silu(h @ w_gate) * (h @ w_up)) @ w_down


def moe_swiglu(h, w_router, b_router, w_gate, w_up, w_down):
    logits = (h @ w_router).astype(F32) + b_router.astype(F32)
    top_v, top_i = lax.top_k(logits, TOP_K)
    wts = jax.nn.softmax(top_v, axis=-1)
    gates = jnp.sum(jax.nn.one_hot(top_i, N_EXPERTS, dtype=F32) * wts[..., None], axis=-2)
    out = jnp.zeros(h.shape, h.dtype)
    for e in range(N_EXPERTS):
        out = out + gates[..., e:e + 1].astype(h.dtype) * swiglu(h, w_gate[e], w_up[e], w_down[e])
    return out


def run_layer(x, cond, l, W, mixer):
    sh_a, sc_a, g_a, sh_f, sc_f, g_f = adaln_params(cond, W['w_ada'][l], W['b_ada'][l])
    h = rmsnorm(x, W['norm_mix_g'][l]) * (1 + sc_a) + sh_a
    mixed, extra = mixer(h @ W['w_in'][l])
    x = x + g_a * (mixed @ W['w_out'][l])
    h = rmsnorm(x, W['norm_ffn_g'][l]) * (1 + sc_f) + sh_f
    if l % 2 == 0:
        i = l // 2
        f = swiglu(h, W['ffn_w_gate'][i], W['ffn_w_up'][i], W['ffn_w_down'][i])
    else:
        i = l // 2
        f = moe_swiglu(h, W['moe_w_router'][i], W['moe_b_router'][i], W['moe_w_gate'][i], W['moe_w_up'][i], W['moe_w_down'][i])
    return x + g_f * f, extra


def setup_inputs(seed: int = 0) -> dict:
    key = jax.random.key(seed)
    ks = iter(jax.random.split(key, 40))
    D = D_MODEL

    def nrm(shape, scale):
        return jax.random.normal(next(ks), shape, F32) * scale

    return {
        "x_prompt": nrm((BATCH, SEQ, D), 1.0),
        "x_sample": nrm((DEC_BATCH, DEC_SEQ, D), 1.0),
        "cache_nat_k": nrm((DEC_BATCH, DEPTH, PAST_LEN, H_A, HD_A), 1.0),
        "cache_nat_v": nrm((DEC_BATCH, DEPTH, PAST_LEN, H_A, HD_A), 1.0),
        "state_gla": nrm((DEC_BATCH, DEPTH, 2, H_B, DK_B, DV_B), 1.0),
        "cache_swa_k": nrm((DEC_BATCH, DEPTH, PAST_LEN, HKV_C, HD_C), 1.0),
        "cache_swa_v": nrm((DEC_BATCH, DEPTH, PAST_LEN, HKV_C, HD_C), 1.0),
        "c": nrm((DEC_BATCH, D), 1.0),
        "c_ctx": nrm((D,), 1.0),
        "w_ada": nrm((DEPTH, D, 6 * D), 0.5 * D ** -0.5),
        "b_ada": nrm((DEPTH, 6 * D), 0.02),
        "norm_mix_g": 1.0 + nrm((DEPTH, D), 0.05),
        "norm_ffn_g": 1.0 + nrm((DEPTH, D), 0.05),
        "w_in": nrm((DEPTH, D, D_IN), D ** -0.5),
        "na_rpb": nrm((DEPTH, H_A, 2 * NA_ROWS - 1, 2 * NA_COLS - 1), 0.5),
        "gla_w_gate": nrm((DEPTH, 2, GLA_RANK, H_B * DK_B), GLA_RANK ** -0.5),
        "gla_b_gate": 1.0 + nrm((DEPTH, 2, H_B * DK_B), 0.5),
        "gla_norm_g": 1.0 + nrm((DEPTH, H_B, DV_B), 0.05),
        "swa_sink": nrm((DEPTH, HKV_C, G_C), 0.5),
        "w_out": nrm((DEPTH, D, D), D ** -0.5),
        "ffn_w_gate": nrm((N_DENSE, D, D_FF), D ** -0.5),
        "ffn_w_up": nrm((N_DENSE, D, D_FF), D ** -0.5),
        "ffn_w_down": nrm((N_DENSE, D_FF, D), D_FF ** -0.5),
        "moe_w_router": nrm((N_MOE, D, N_EXPERTS), D ** -0.5),
        "moe_b_router": nrm((N_MOE, N_EXPERTS), 0.01),
        "moe_w_gate": nrm((N_MOE, N_EXPERTS, D, MOE_FF), D ** -0.5),
        "moe_w_up": nrm((N_MOE, N_EXPERTS, D, MOE_FF), D ** -0.5),
        "moe_w_down": nrm((N_MOE, N_EXPERTS, MOE_FF, D), MOE_FF ** -0.5),
        "final_norm_g": 1.0 + nrm((D,), 0.05),
    }


def reference(x_prompt, x_sample, cache_nat_k, cache_nat_v, state_gla, cache_swa_k, cache_swa_v, c, c_ctx,
              w_ada, b_ada, norm_mix_g, norm_ffn_g, w_in, na_rpb, gla_w_gate, gla_b_gate, gla_norm_g, swa_sink,
              w_out, ffn_w_gate, ffn_w_up, ffn_w_down, moe_w_router, moe_b_router, moe_w_gate, moe_w_up,
              moe_w_down, final_norm_g):
    W = {"w_ada": w_ada, "b_ada": b_ada, "norm_mix_g": norm_mix_g, "norm_ffn_g": norm_ffn_g, "w_in": w_in,
         "w_out": w_out, "ffn_w_gate": ffn_w_gate, "ffn_w_up": ffn_w_up, "ffn_w_down": ffn_w_down,
         "moe_w_router": moe_w_router, "moe_b_router": moe_b_router, "moe_w_gate": moe_w_gate,
         "moe_w_up": moe_w_up, "moe_w_down": moe_w_down}

    x = x_prompt
    nat_k, nat_v, gla_s, swa_k, swa_v = [], [], [], [], []
    for l in range(DEPTH):
        def context_mixer(u, l=l):
            (qa, ka, va), (qb, kb, vb, rb, af, ab), (qc, kc, vc) = mixer_inputs(u, gla_w_gate[l], gla_b_gate[l])
            zeros = jnp.zeros((u.shape[0], H_B, DK_B, DV_B), F32)
            o_a = context_self_attention(qa, ka, va)
            o_g, sf, sb = bidirectional_gla(qb, kb, vb, af, ab, zeros, zeros)
            o_b = gla_output(o_g, rb, gla_norm_g[l])
            o_c = context_self_attention(qc, kc, vc, swa_sink[l])
            mixed = jnp.concatenate([o_a, o_b.astype(o_a.dtype), o_c], axis=-1)
            return mixed, (ka, va, jnp.stack([sf, sb], axis=1), kc, vc)

        x, (ka, va, st, kc, vc) = run_layer(x, c_ctx[None, :], l, W, context_mixer)
        nat_k.append(ka); nat_v.append(va); gla_s.append(st); swa_k.append(kc); swa_v.append(vc)
    y_prompt = rmsnorm(x, final_norm_g)
    new_nat_k = jnp.stack(nat_k, axis=1)
    new_nat_v = jnp.stack(nat_v, axis=1)
    new_state_gla = jnp.stack(gla_s, axis=1)
    new_swa_k = jnp.stack(swa_k, axis=1)
    new_swa_v = jnp.stack(swa_v, axis=1)

    x = x_sample
    for l in range(DEPTH):
        def latent_mixer(u, l=l):
            (qa, ka, va), (qb, kb, vb, rb, af, ab), (qc, kc, vc) = mixer_inputs(u, gla_w_gate[l], gla_b_gate[l])
            o_a = neighbourhood_attention(qa[:, :, :, 0], ka, va, cache_nat_k[:, l], cache_nat_v[:, l], na_rpb[l])
            o_g, _, _ = bidirectional_gla(qb, kb, vb, af, ab, state_gla[:, l, 0], state_gla[:, l, 1])
            o_b = gla_output(o_g, rb, gla_norm_g[l])
            o_c = window_attention(axial_rope(qc), axial_rope(kc), vc, cache_swa_k[:, l], cache_swa_v[:, l], swa_sink[l])
            return jnp.concatenate([o_a, o_b.astype(o_a.dtype), o_c], axis=-1), None

        x, _ = run_layer(x, c, l, W, latent_mixer)
    y_sample = rmsnorm(x, final_norm_g)

    return (y_prompt, y_sample, new_nat_k, new_nat_v, new_state_gla, new_swa_k, new_swa_v)
```

```python
import functools

import numpy as np
import jax
import jax.numpy as jnp
from jax import lax
from jax.experimental import pallas as pl
from jax.experimental.pallas import tpu as pltpu

F32 = jnp.float32
BF16 = jnp.bfloat16
HIGHEST = lax.Precision.HIGHEST

EPS = 1e-6
NEG_INF = -1e30
GRID_W = 64
HEAD_DIM = 64
NA_ROWS = 8
NA_COLS = 16
SWA_WIN = 128
SWA_QBLOCK = 128
GLA_CHUNK = 64
GLA_TAU = 16.0
ROPE_BASE = 10000.0
LANES = 128
VMEM_LIMIT = 56 * 1024 * 1024


def _cparams(n_axes):
    return pltpu.CompilerParams(dimension_semantics=("arbitrary",) * n_axes, vmem_limit_bytes=VMEM_LIMIT)


def _dot(a, b, **kw):
    return jnp.dot(a, b, preferred_element_type=F32, **kw)


def _dot_nt(a, b, **kw):
    return lax.dot_general(a, b, (((1,), (1,)), ((), ())), preferred_element_type=F32, **kw)


def _dot_tn(a, b, **kw):
    return lax.dot_general(a, b, (((0,), (0,)), ((), ())), preferred_element_type=F32, **kw)


def _silu(x):
    return x / (1.0 + jnp.exp(-x))


def _rmsnorm(x, g):
    return x * lax.rsqrt(jnp.mean(x * x, axis=-1, keepdims=True) + EPS) * g


def _adaln_kernel(cond_ref, w_ref, b_ref, o_ref):
    s = _silu(cond_ref[...]).astype(BF16)
    o_ref[0] = _dot(s, w_ref[0].astype(BF16)) + b_ref[0]


def _adaln(cond8, w_ada, b_ada, tn=1024):
    depth, d, n6 = w_ada.shape
    return pl.pallas_call(
        _adaln_kernel,
        grid=(depth, n6 // tn),
        in_specs=[
            pl.BlockSpec((8, d), lambda l, j: (0, 0)),
            pl.BlockSpec((1, d, tn), lambda l, j: (l, 0, j)),
            pl.BlockSpec((1, 1, tn), lambda l, j: (l, 0, j)),
        ],
        out_specs=pl.BlockSpec((1, 8, tn), lambda l, j: (l, 0, j)),
        out_shape=jax.ShapeDtypeStruct((depth, 8, n6), F32),
        compiler_params=_cparams(2),
        name="adaln",
    )(cond8, w_ada, b_ada.reshape(depth, 1, n6))


def _inproj_kernel(x_ref, mod_ref, g_ref, w_ref, wz_ref, u_ref, z_ref, nk_ref, nv_ref, sk_ref, sv_ref, h_scr,
                   *, nct, seq, j_ka, j_va, j_kvc, w_kvc):
    i = pl.program_id(0)
    j = pl.program_id(1)

    @pl.when(j == 0)
    def _():
        h = _rmsnorm(x_ref[...], g_ref[...]) * (1.0 + mod_ref[0, 1:2, :]) + mod_ref[0, 0:1, :]
        hb = h.astype(BF16)
        h_scr[...] = hb
        z_ref[...] = _dot(hb, wz_ref[...])

    acc = _dot(h_scr[...], w_ref[...])
    u_ref[...] = acc.astype(BF16)
    is_ctx = i < nct
    nseq = nk_ref.shape[0]

    @pl.when(is_ctx & (j == j_ka))
    def _():
        for bb in range(nseq):
            nk_ref[bb] = acc[bb * seq:(bb + 1) * seq, :]

    @pl.when(is_ctx & (j == j_va))
    def _():
        for bb in range(nseq):
            nv_ref[bb] = acc[bb * seq:(bb + 1) * seq, :]

    @pl.when(is_ctx & (j == j_kvc))
    def _():
        for bb in range(nseq):
            sk_ref[bb] = acc[bb * seq:(bb + 1) * seq, 0:w_kvc]
            sv_ref[bb] = acc[bb * seq:(bb + 1) * seq, w_kvc:2 * w_kvc]


def _inproj(x, mod_l, g, w_main, wz, *, n_ctx, seq, t_lat, w_a, w_kvc, tm=512, tn=512):
    n, d = x.shape
    n_main = w_main.shape[1]
    nct = n_ctx // tm
    nseq = tm // seq
    nb = n_ctx // seq
    assert n_ctx % tm == 0 and tm % seq == 0 and t_lat % tm == 0 and w_a == tn and 2 * w_kvc == tn

    def cond_idx(i):
        return jnp.where(i < nct, 0, 1 + ((i - nct) * tm) // t_lat)

    kv_idx = lambda i, j: (jnp.minimum(i, nct - 1), 0, 0)
    kern = functools.partial(_inproj_kernel, nct=nct, seq=seq, j_ka=1, j_va=2, j_kvc=n_main // tn - 1, w_kvc=w_kvc)
    return pl.pallas_call(
        kern,
        grid=(n // tm, n_main // tn),
        in_specs=[
            pl.BlockSpec((tm, d), lambda i, j: (i, 0)),
            pl.BlockSpec((1, 6, d), lambda i, j: (cond_idx(i), 0, 0)),
            pl.BlockSpec((1, d), lambda i, j: (0, 0)),
            pl.BlockSpec((d, tn), lambda i, j: (0, j)),
            pl.BlockSpec((d, LANES), lambda i, j: (0, 0)),
        ],
        out_specs=[
            pl.BlockSpec((tm, tn), lambda i, j: (i, j)),
            pl.BlockSpec((tm, LANES), lambda i, j: (i, 0)),
            pl.BlockSpec((nseq, seq, w_a), kv_idx),
            pl.BlockSpec((nseq, seq, w_a), kv_idx),
            pl.BlockSpec((nseq, seq, w_kvc), kv_idx),
            pl.BlockSpec((nseq, seq, w_kvc), kv_idx),
        ],
        out_shape=[
            jax.ShapeDtypeStruct((n, n_main), BF16),
            jax.ShapeDtypeStruct((n, LANES), F32),
            jax.ShapeDtypeStruct((nb, seq, w_a), F32),
            jax.ShapeDtypeStruct((nb, seq, w_a), F32),
            jax.ShapeDtypeStruct((nb, seq, w_kvc), F32),
            jax.ShapeDtypeStruct((nb, seq, w_kvc), F32),
        ],
        scratch_shapes=[pltpu.VMEM((tm, d), BF16)],
        compiler_params=_cparams(2),
        name="inproj",
    )(x, mod_l, g, w_main, wz)


def _ctx_attn_kernel(*refs, hkv, g, has_sink):
    if has_sink:
        sink_ref, q_ref, k_ref, v_ref, o_ref = refs
    else:
        q_ref, k_ref, v_ref, o_ref = refs
    scale = HEAD_DIM ** -0.5
    q = q_ref[...]
    k = k_ref[...]
    v = v_ref[...]
    outs = []
    for kh in range(hkv):
        k_h = k[:, kh * HEAD_DIM:(kh + 1) * HEAD_DIM]
        v_h = v[:, kh * HEAD_DIM:(kh + 1) * HEAD_DIM]
        for gi in range(g):
            hq = kh * g + gi
            q_h = q[:, hq * HEAD_DIM:(hq + 1) * HEAD_DIM]
            s = _dot_nt(q_h, k_h) * scale
            m = jnp.max(s, axis=-1, keepdims=True)
            if has_sink:
                sk = sink_ref[kh, gi]
                m = jnp.maximum(m, sk)
            e = jnp.exp(s - m)
            den = jnp.sum(e, axis=-1, keepdims=True)
            if has_sink:
                den = den + jnp.exp(sk - m)
            outs.append(_dot(e.astype(BF16), v_h) / den)
    o_ref[...] = jnp.concatenate(outs, axis=-1).astype(BF16)


def _ctx_attn(u, sink, *, n_rows_out, nb, seq, qcol, kcol, vcol, wq, wk, hkv, g):
    has_sink = sink is not None
    kern = functools.partial(_ctx_attn_kernel, hkv=hkv, g=g, has_sink=has_sink)
    in_specs = [
        pl.BlockSpec((seq, wq), lambda b: (b, qcol)),
        pl.BlockSpec((seq, wk), lambda b: (b, kcol)),
        pl.BlockSpec((seq, wk), lambda b: (b, vcol)),
    ]
    args = [u, u, u]
    if has_sink:
        in_specs = [pl.BlockSpec(memory_space=pltpu.SMEM)] + in_specs
        args = [sink] + args
    return pl.pallas_call(
        kern,
        grid=(nb,),
        in_specs=in_specs,
        out_specs=pl.BlockSpec((seq, wq), lambda b: (b, 0)),
        out_shape=jax.ShapeDtypeStruct((n_rows_out, wq), BF16),
        compiler_params=_cparams(1),
        name="ctx_attn_sink" if has_sink else "ctx_attn",
    )(*args)


def _na_kernel(prev_ref, q_ref, k_ref, v_ref, kc_ref, vc_ref, bias_ref, o_ref, *, heads):
    del prev_ref
    scale = HEAD_DIM ** -0.5
    q = q_ref[...]
    k = k_ref[...]
    v = v_ref[...]
    kc = kc_ref[0, 0].astype(BF16)
    vc = vc_ref[0, 0].astype(BF16)
    outs = []
    for h in range(heads):
        sl = slice(h * HEAD_DIM, (h + 1) * HEAD_DIM)
        q_h = q[:, sl]
        s_w = _dot_nt(q_h, k[:, sl]) * scale + bias_ref[h]
        s_c = _dot_nt(q_h, kc[:, sl]) * scale
        m = jnp.maximum(jnp.max(s_w, axis=-1, keepdims=True), jnp.max(s_c, axis=-1, keepdims=True))
        e_w = jnp.exp(s_w - m)
        e_c = jnp.exp(s_c - m)
        den = jnp.sum(e_w, axis=-1, keepdims=True) + jnp.sum(e_c, axis=-1, keepdims=True)
        o = _dot(e_w.astype(BF16), v[:, sl]) + _dot(e_c.astype(BF16), vc[:, sl])
        outs.append(o / den)
    o_ref[...] = jnp.concatenate(outs, axis=-1).astype(BF16)


def _na_attn(o_prev, u, kc, vc, bias, l, *, n_ctx, db, t_lat, w_a, heads, tq=256):
    nq = t_lat // tq
    p = kc.shape[2]
    kern = functools.partial(_na_kernel, heads=heads)
    return pl.pallas_call(
        kern,
        grid=(db, nq),
        in_specs=[
            pl.BlockSpec(memory_space=pl.ANY),
            pl.BlockSpec((tq, w_a), lambda b, i: (n_ctx // tq + b * nq + i, 0)),
            pl.BlockSpec((t_lat, w_a), lambda b, i: (n_ctx // t_lat + b, 1)),
            pl.BlockSpec((t_lat, w_a), lambda b, i: (n_ctx // t_lat + b, 2)),
            pl.BlockSpec((1, 1, p, w_a), lambda b, i: (b, l, 0, 0)),
            pl.BlockSpec((1, 1, p, w_a), lambda b, i: (b, l, 0, 0)),
            pl.BlockSpec((heads, tq, t_lat), lambda b, i: (0, i, 0)),
        ],
        out_specs=pl.BlockSpec((tq, w_a), lambda b, i: (n_ctx // tq + b * nq + i, 0)),
        out_shape=jax.ShapeDtypeStruct(o_prev.shape, BF16),
        input_output_aliases={0: 0},
        compiler_params=_cparams(2),
        name="na_attn",
    )(o_prev, u, u, u, kc, vc, bias)


def _na_bias(rpb, t_lat):
    heads = rpb.shape[0]
    rows = t_lat // GRID_W
    kh = min(NA_ROWS, rows)
    r = np.arange(rows)
    row_start = np.clip(r - kh // 2, 0, rows - kh)
    col = np.arange(GRID_W)
    col_start = np.clip(col - NA_COLS // 2, 0, GRID_W - NA_COLS)
    row_ok = (r[None, :] >= row_start[:, None]) & (r[None, :] < row_start[:, None] + kh)
    col_ok = (col[None, :] >= col_start[:, None]) & (col[None, :] < col_start[:, None] + NA_COLS)
    padc = GRID_W
    rp = jnp.pad(rpb.astype(F32), ((0, 0), (0, 0), (padc, padc)))
    cols = jnp.stack([rp[:, :, padc + NA_COLS - 1 - c: padc + NA_COLS - 1 - c + GRID_W] for c in range(GRID_W)], axis=2)
    padr = rows
    cp = jnp.pad(cols, ((0, 0), (padr, padr), (0, 0), (0, 0)))
    full = jnp.stack([cp[:, padr + NA_ROWS - 1 - rr: padr + NA_ROWS - 1 - rr + rows] for rr in range(rows)], axis=1)
    full = full.transpose(0, 1, 3, 2, 4)
    ok = row_ok[:, None, :, None] & col_ok[None, :, None, :]
    full = jnp.where(jnp.asarray(ok)[None], full, NEG_INF)
    return full.reshape(heads, t_lat, t_lat)


def _rope(x, cos, sin_signed):
    w = x.shape[-1]
    q4 = HEAD_DIM // 4
    lane = lax.broadcasted_iota(jnp.int32, x.shape, 1)
    first = (lane % (2 * q4)) < q4
    swapped = jnp.where(first, pltpu.roll(x, w - q4, 1), pltpu.roll(x, q4, 1))
    return x * cos + swapped * sin_signed


def _swa_kernel(sink_ref, prev_ref, q_ref, k_ref, v_ref, kc_ref, vc_ref, cos_ref, sin_ref, o_ref, k_scr,
                *, hkv, g, t_lat):
    del prev_ref
    n = pl.program_id(1)
    scale = HEAD_DIM ** -0.5
    qb = SWA_QBLOCK
    wk = hkv * HEAD_DIM
    span = 3 * qb

    @pl.when(n == 0)
    def _():
        k_scr[...] = _rope(k_ref[...].astype(F32), cos_ref[...], sin_ref[...]).astype(BF16)

    q0 = pl.multiple_of(n * qb, qb)
    start = pl.multiple_of(jnp.clip(n * qb - qb, 0, t_lat - span), qb)
    cos_q = cos_ref[pl.ds(q0, qb), :]
    sin_q = sin_ref[pl.ds(q0, qb), :]
    k_loc = k_scr[pl.ds(start, span), :]
    v_loc = v_ref[pl.ds(start, span), :]
    kc = kc_ref[0, 0].astype(BF16)
    vc = vc_ref[0, 0].astype(BF16)
    qpos = q0 + lax.broadcasted_iota(jnp.int32, (qb, span), 0)
    kpos = start + lax.broadcasted_iota(jnp.int32, (qb, span), 1)
    valid1 = jnp.abs(qpos - kpos) <= SWA_WIN
    valid = jnp.concatenate([valid1] * g, axis=0)
    outs = []
    for kh in range(hkv):
        q_r = _rope(q_ref[:, kh * wk:(kh + 1) * wk].astype(F32), cos_q, sin_q).astype(BF16)
        q_st = jnp.concatenate([q_r[:, gi * HEAD_DIM:(gi + 1) * HEAD_DIM] for gi in range(g)], axis=0)
        sk = jnp.concatenate([jnp.full((qb, 1), sink_ref[kh, gi], F32) for gi in range(g)], axis=0)
        ksl = slice(kh * HEAD_DIM, (kh + 1) * HEAD_DIM)
        s_l = jnp.where(valid, _dot_nt(q_st, k_loc[:, ksl]) * scale, NEG_INF)
        s_c = _dot_nt(q_st, kc[:, ksl]) * scale
        m = jnp.maximum(jnp.maximum(jnp.max(s_l, axis=-1, keepdims=True), jnp.max(s_c, axis=-1, keepdims=True)), sk)
        e_l = jnp.exp(s_l - m)
        e_c = jnp.exp(s_c - m)
        den = jnp.sum(e_l, axis=-1, keepdims=True) + jnp.sum(e_c, axis=-1, keepdims=True) + jnp.exp(sk - m)
        o = (_dot(e_l.astype(BF16), v_loc[:, ksl]) + _dot(e_c.astype(BF16), vc[:, ksl])) / den
        outs.extend(o[gi * qb:(gi + 1) * qb] for gi in range(g))
    o_ref[...] = jnp.concatenate(outs, axis=-1).astype(BF16)


def _swa_attn(o_prev, u, sink, kc, vc, cos, sin, l, *, n_ctx, db, t_lat, qcol, kcol, vcol, hkv, g):
    qb = SWA_QBLOCK
    nq = t_lat // qb
    wq = hkv * g * HEAD_DIM
    wk = hkv * HEAD_DIM
    assert g * HEAD_DIM == wk, "rotary tables are shared between the q groups and k"
    p = kc.shape[2]
    kern = functools.partial(_swa_kernel, hkv=hkv, g=g, t_lat=t_lat)
    return pl.pallas_call(
        kern,
        grid=(db, nq),
        in_specs=[
            pl.BlockSpec(memory_space=pltpu.SMEM),
            pl.BlockSpec(memory_space=pl.ANY),
            pl.BlockSpec((qb, wq), lambda b, i: (n_ctx // qb + b * nq + i, qcol)),
            pl.BlockSpec((t_lat, wk), lambda b, i: (n_ctx // t_lat + b, kcol)),
            pl.BlockSpec((t_lat, wk), lambda b, i: (n_ctx // t_lat + b, vcol)),
            pl.BlockSpec((1, 1, p, wk), lambda b, i: (b, l, 0, 0)),
            pl.BlockSpec((1, 1, p, wk), lambda b, i: (b, l, 0, 0)),
            pl.BlockSpec((t_lat, wk), lambda b, i: (0, 0)),
            pl.BlockSpec((t_lat, wk), lambda b, i: (0, 0)),
        ],
        out_specs=pl.BlockSpec((qb, wq), lambda b, i: (n_ctx // qb + b * nq + i, 0)),
        out_shape=jax.ShapeDtypeStruct(o_prev.shape, BF16),
        scratch_shapes=[pltpu.VMEM((t_lat, wk), BF16)],
        input_output_aliases={1: 0},
        compiler_params=_cparams(2),
        name="swa_attn",
    )(sink, o_prev, u, u, u, kc, vc, cos, sin)


def _rope_tables(t_lat, width):
    quarter = HEAD_DIM // 4
    pos = np.arange(t_lat)
    freqs = ROPE_BASE ** (-np.arange(quarter, dtype=np.float32) / quarter)
    ang_r = (pos // GRID_W).astype(np.float32)[:, None] * freqs[None, :]
    ang_c = (pos % GRID_W).astype(np.float32)[:, None] * freqs[None, :]
    ang_r, ang_c = jnp.asarray(ang_r, F32), jnp.asarray(ang_c, F32)
    cos = jnp.concatenate([jnp.cos(ang_r), jnp.cos(ang_r), jnp.cos(ang_c), jnp.cos(ang_c)], axis=-1)
    sin = jnp.concatenate([-jnp.sin(ang_r), jnp.sin(ang_r), -jnp.sin(ang_c), jnp.sin(ang_c)], axis=-1)
    reps = width // HEAD_DIM
    return jnp.tile(cos, (1, reps)), jnp.tile(sin, (1, reps))


def _log_sigmoid(x):
    return jnp.minimum(x, 0.0) - jnp.log(1.0 + jnp.exp(-jnp.abs(x)))


def _gla_kernel(*refs, heads, dk, dv, seq_len, has_s0, has_prev, out_state):
    refs = list(refs)
    if has_prev:
        refs.pop(0)
    ub_ref, z_ref, gw_ref, gb_ref, gn_ref = refs[:5]
    refs = refs[5:]
    s0_ref = refs.pop(0) if has_s0 else None
    o_ref = refs.pop(0)
    st_ref = refs.pop(0) if out_state else None
    la_scr, o_scr, st_scr = refs
    c_len = GLA_CHUNK
    nc = seq_len // c_len
    wk = heads * dk
    wv = heads * dv

    z = z_ref[...]
    for d in range(2):
        pre = _dot(z, gw_ref[d], precision=HIGHEST) + gb_ref[d]
        la_scr[d] = _log_sigmoid(pre) / GLA_TAU

    eye_v = (lax.broadcasted_iota(jnp.int32, (dv, dv), 0) == lax.broadcasted_iota(jnp.int32, (dv, dv), 1)).astype(F32)
    eye_k = (lax.broadcasted_iota(jnp.int32, (dk, dk), 0) == lax.broadcasted_iota(jnp.int32, (dk, dk), 1)).astype(F32)
    for d in range(2):
        for h in range(heads):
            if has_s0:
                st_scr[d, h] = _dot_nt(eye_v, s0_ref[0, 0, d, h], precision=HIGHEST)
            else:
                st_scr[d, h] = jnp.zeros((dv, dk), F32)

    ti = lax.broadcasted_iota(jnp.int32, (c_len, c_len), 0)
    si = lax.broadcasted_iota(jnp.int32, (c_len, c_len), 1)

    def chunk(c, d):
        reverse = d == 1
        keep = (si >= ti) if reverse else (si <= ti)
        tri = keep.astype(F32)
        r0 = pl.multiple_of(c * c_len, c_len)
        rows = pl.ds(r0, c_len)
        b = _dot(tri, la_scr[d, rows, :], precision=HIGHEST)
        b_end = b[0:1] if reverse else b[c_len - 1:c_len]
        q = ub_ref[rows, 0:wk].astype(F32) * (dk ** -0.5)
        k = ub_ref[rows, wk:2 * wk].astype(F32)
        v = ub_ref[rows, 2 * wk:2 * wk + wv]
        qt = (q * jnp.exp(b)).astype(BF16)
        kt = (k * jnp.exp(-b)).astype(BF16)
        ke = (k * jnp.exp(b_end - b)).astype(BF16)
        dec = jnp.exp(b_end)
        for h in range(heads):
            ks = slice(h * dk, (h + 1) * dk)
            vs = slice(h * dv, (h + 1) * dv)
            a = jnp.where(keep, _dot_nt(qt[:, ks], kt[:, ks]), 0.0).astype(BF16)
            st = st_scr[d, h]
            o = _dot_nt(qt[:, ks], st.astype(BF16)) + _dot(a, v[:, vs])
            st_scr[d, h] = st * dec[:, ks] + _dot_tn(v[:, vs], ke[:, ks])
            if reverse:
                o_scr[rows, vs] += o
            else:
                o_scr[rows, vs] = o

    lax.fori_loop(0, nc, lambda c, carry: (chunk(c, 0), carry)[1], 0)
    lax.fori_loop(0, nc, lambda c, carry: (chunk(nc - 1 - c, 1), carry)[1], 0)

    r = ub_ref[:, 2 * wk + wv:2 * wk + 2 * wv].astype(F32)
    gn = gn_ref[...]
    for h in range(heads):
        vs = slice(h * dv, (h + 1) * dv)
        o_ref[:, vs] = (_rmsnorm(o_scr[:, vs], gn[:, vs]) * _silu(r[:, vs])).astype(BF16)
    if out_state:
        for d in range(2):
            for h in range(heads):
                st_ref[0, d, h] = _dot_nt(eye_k, st_scr[d, h], precision=HIGHEST)


def _gla(o_prev, u, z, gw, gb, gn, s0, l, *, n_rows_out, row0, nb, seq_len, bcol, heads, dk, dv, out_state):
    wk, wv = heads * dk, heads * dv
    wb = 2 * wk + 2 * wv
    has_s0 = s0 is not None
    has_prev = o_prev is not None
    rb = row0 // seq_len
    kern = functools.partial(_gla_kernel, heads=heads, dk=dk, dv=dv, seq_len=seq_len, has_s0=has_s0,
                             has_prev=has_prev, out_state=out_state)
    in_specs = [
        pl.BlockSpec((seq_len, wb), lambda b: (rb + b, bcol)),
        pl.BlockSpec((seq_len, LANES), lambda b: (rb + b, 0)),
        pl.BlockSpec((2, LANES, wk), lambda b: (0, 0, 0)),
        pl.BlockSpec((2, 1, wk), lambda b: (0, 0, 0)),
        pl.BlockSpec((1, wv), lambda b: (0, 0)),
    ]
    args = [u, z, gw, gb, gn]
    if has_s0:
        in_specs.append(pl.BlockSpec((1, 1, 2, heads, dk, dv), lambda b: (b, l, 0, 0, 0, 0)))
        args.append(s0)
    aliases = {}
    if has_prev:
        in_specs = [pl.BlockSpec(memory_space=pl.ANY)] + in_specs
        args = [o_prev] + args
        aliases = {0: 0}
    out_specs = [pl.BlockSpec((seq_len, wv), lambda b: (rb + b, 0))]
    out_shape = [jax.ShapeDtypeStruct((n_rows_out, wv), BF16)]
    if out_state:
        out_specs.append(pl.BlockSpec((1, 2, heads, dk, dv), lambda b: (b, 0, 0, 0, 0)))
        out_shape.append(jax.ShapeDtypeStruct((nb, 2, heads, dk, dv), F32))
    res = pl.pallas_call(
        kern,
        grid=(nb,),
        in_specs=in_specs,
        out_specs=out_specs,
        out_shape=out_shape,
        scratch_shapes=[
            pltpu.VMEM((2, seq_len, wk), F32),
            pltpu.VMEM((seq_len, wv), F32),
            pltpu.VMEM((2, heads, dv, dk), F32),
        ],
        input_output_aliases=aliases,
        compiler_params=_cparams(1),
        name="gla_lat" if has_s0 else "gla_ctx",
    )(*args)
    return res


def _outproj_kernel(*refs, n_experts, top_k):
    if n_experts:
        x_ref, oa_ref, ob_ref, oc_ref, w_ref, mod_ref, g_ref, wr_ref, br_ref, xn_ref, h_ref, gates_ref = refs
    else:
        x_ref, oa_ref, ob_ref, oc_ref, w_ref, mod_ref, g_ref, xn_ref, h_ref = refs
    mixed = jnp.concatenate([oa_ref[...], ob_ref[...], oc_ref[...]], axis=-1)
    xn = x_ref[...] + mod_ref[0, 2:3, :] * _dot(mixed, w_ref[...])
    xn_ref[...] = xn
    h = _rmsnorm(xn, g_ref[...]) * (1.0 + mod_ref[0, 4:5, :]) + mod_ref[0, 3:4, :]
    h_ref[...] = h.astype(BF16)
    if n_experts:
        assert top_k == 2
        logits = _dot(h, wr_ref[...], precision=HIGHEST) + br_ref[...]
        lane = lax.broadcasted_iota(jnp.int32, logits.shape, 1)
        l1 = jnp.max(logits, axis=-1, keepdims=True)
        i1 = jnp.min(jnp.where(logits == l1, lane, LANES), axis=-1, keepdims=True)
        rest = jnp.where(lane == i1, -jnp.inf, logits)
        l2 = jnp.max(rest, axis=-1, keepdims=True)
        i2 = jnp.min(jnp.where(rest == l2, lane, LANES), axis=-1, keepdims=True)
        e2 = jnp.exp(l2 - l1)
        w1 = 1.0 / (1.0 + e2)
        gates_ref[...] = jnp.where(lane == i1, w1, 0.0) + jnp.where(lane == i2, e2 * w1, 0.0)


def _outproj(x, oa, ob, oc, w_out, mod_l, g, router, *, n_ctx, t_lat, tm=512):
    n, d = x.shape
    nct = n_ctx // tm

    def cond_idx(i):
        return jnp.where(i < nct, 0, 1 + ((i - nct) * tm) // t_lat)

    row = lambda i: (i, 0)
    const = lambda i: (0, 0)
    in_specs = [
        pl.BlockSpec((tm, d), row),
        pl.BlockSpec((tm, oa.shape[1]), row),
        pl.BlockSpec((tm, ob.shape[1]), row),
        pl.BlockSpec((tm, oc.shape[1]), row),
        pl.BlockSpec((d, d), const),
        pl.BlockSpec((1, 6, d), lambda i: (cond_idx(i), 0, 0)),
        pl.BlockSpec((1, d), const),
    ]
    args = [x, oa, ob, oc, w_out, mod_l, g]
    out_specs = [pl.BlockSpec((tm, d), row), pl.BlockSpec((tm, d), row)]
    out_shape = [jax.ShapeDtypeStruct((n, d), F32), jax.ShapeDtypeStruct((n, d), BF16)]
    n_experts = 0
    if router is not None:
        wr, br, n_experts = router
        in_specs += [pl.BlockSpec((d, LANES), const), pl.BlockSpec((1, LANES), const)]
        args += [wr, br]
        out_specs.append(pl.BlockSpec((tm, LANES), row))
        out_shape.append(jax.ShapeDtypeStruct((n, LANES), F32))
    kern = functools.partial(_outproj_kernel, n_experts=n_experts, top_k=2)
    return pl.pallas_call(
        kern,
        grid=(n // tm,),
        in_specs=in_specs,
        out_specs=out_specs,
        out_shape=out_shape,
        compiler_params=_cparams(1),
        name="outproj_router" if n_experts else "outproj",
    )(*args)


def _ffn_kernel(*refs, has_gates, final_norm):
    refs = list(refs)
    h_ref, x_ref = refs[:2]
    refs = refs[2:]
    gates_ref = refs.pop(0) if has_gates else None
    wg_ref, wu_ref, wd_ref, mod_ref = refs[:4]
    refs = refs[4:]
    fg_ref = refs.pop(0) if final_norm else None
    (o_ref,) = refs
    e = pl.program_id(1)
    f = pl.program_id(2)

    @pl.when((e == 0) & (f == 0))
    def _():
        o_ref[...] = jnp.zeros(o_ref.shape, F32)

    h = h_ref[...]
    t = _silu(_dot(h, wg_ref[0])) * _dot(h, wu_ref[0])
    if has_gates:
        gates = gates_ref[...]
        lane = lax.broadcasted_iota(jnp.int32, gates.shape, 1)
        t = t * jnp.sum(jnp.where(lane == e, gates, 0.0), axis=-1, keepdims=True)
    o_ref[...] += _dot(t.astype(BF16), wd_ref[0])

    @pl.when((e == pl.num_programs(1) - 1) & (f == pl.num_programs(2) - 1))
    def _():
        y = x_ref[...] + mod_ref[0, 5:6, :] * o_ref[...]
        if final_norm:
            y = _rmsnorm(y, fg_ref[...])
        o_ref[...] = y


def _ffn(h, x, gates, wg, wu, wd, mod_l, final_g, *, n_ctx, t_lat, tm=512, tf=512):
    n, d = x.shape
    n_e, _, ff = wg.shape
    nct = n_ctx // tm

    def cond_idx(i):
        return jnp.where(i < nct, 0, 1 + ((i - nct) * tm) // t_lat)

    row = lambda i, e, f: (i, 0)
    in_specs = [pl.BlockSpec((tm, d), row), pl.BlockSpec((tm, d), row)]
    args = [h, x]
    if gates is not None:
        in_specs.append(pl.BlockSpec((tm, LANES), row))
        args.append(gates)
    in_specs += [
        pl.BlockSpec((1, d, tf), lambda i, e, f: (e, 0, f)),
        pl.BlockSpec((1, d, tf), lambda i, e, f: (e, 0, f)),
        pl.BlockSpec((1, tf, d), lambda i, e, f: (e, f, 0)),
        pl.BlockSpec((1, 6, d), lambda i, e, f: (cond_idx(i), 0, 0)),
    ]
    args += [wg, wu, wd, mod_l]
    if final_g is not None:
        in_specs.append(pl.BlockSpec((1, d), lambda i, e, f: (0, 0)))
        args.append(final_g)
    kern = functools.partial(_ffn_kernel, has_gates=gates is not None, final_norm=final_g is not None)
    return pl.pallas_call(
        kern,
        grid=(n // tm, n_e, ff // tf),
        in_specs=in_specs,
        out_specs=pl.BlockSpec((tm, d), row),
        out_shape=jax.ShapeDtypeStruct((n, d), F32),
        compiler_params=_cparams(3),
        name="ffn_moe" if gates is not None else "ffn_dense",
    )(*args)


def kernel(x_prompt, x_sample, cache_nat_k, cache_nat_v, state_gla, cache_swa_k, cache_swa_v, c, c_ctx, w_ada, b_ada,
           norm_mix_g, norm_ffn_g, w_in, na_rpb, gla_w_gate, gla_b_gate, gla_norm_g, swa_sink, w_out, ffn_w_gate,
           ffn_w_up, ffn_w_down, moe_w_router, moe_b_router, moe_w_gate, moe_w_up, moe_w_down, final_norm_g):
    nb, seq, d = x_prompt.shape
    db, t_lat, _ = x_sample.shape
    depth = w_in.shape[0]
    p_len, h_a = cache_nat_k.shape[2], cache_nat_k.shape[3]
    h_b, dk_b, dv_b = state_gla.shape[3:6]
    hkv_c, g_c = swa_sink.shape[1], swa_sink.shape[2]
    rank = gla_w_gate.shape[2]
    n_exp = moe_w_router.shape[-1]
    w_a = h_a * HEAD_DIM
    w_bk = h_b * dk_b
    w_bv = h_b * dv_b
    w_cq = hkv_c * g_c * HEAD_DIM
    w_ck = hkv_c * HEAD_DIM
    n_ctx = nb * seq
    n_lat = db * t_lat
    n = n_ctx + n_lat
    z0 = 3 * w_a + 2 * w_bk + 2 * w_bv
    z1 = z0 + 2 * rank
    a_w = 3 * w_a
    b_w = 2 * w_bk + 2 * w_bv
    assert a_w == b_w and (a_w + b_w) % w_cq == 0 and (a_w + b_w + w_cq) % w_ck == 0 and 2 * rank <= LANES
    assert db + 1 <= 8

    x = jnp.concatenate([x_prompt.reshape(n_ctx, d), x_sample.reshape(n_lat, d)], axis=0)
    cond8 = jnp.concatenate([c_ctx[None, :], c, jnp.zeros((7 - db, d), F32)], axis=0)
    mod = _adaln(cond8, w_ada, b_ada).reshape(depth, 8, 6, d)

    cos, sin = _rope_tables(t_lat, w_ck)
    kc_a = cache_nat_k.reshape(db, depth, p_len, w_a)
    vc_a = cache_nat_v.reshape(db, depth, p_len, w_a)
    kc_c = cache_swa_k.reshape(db, depth, p_len, w_ck)
    vc_c = cache_swa_v.reshape(db, depth, p_len, w_ck)
    sizes = dict(n_ctx=n_ctx, t_lat=t_lat)

    nat_k, nat_v, gla_s, swa_k, swa_v = [], [], [], [], []
    for l in range(depth):
        w_l = w_in[l]
        w_main = jnp.concatenate([w_l[:, :z0], w_l[:, z1:]], axis=1).astype(BF16)
        wz = jnp.pad(w_l[:, z0:z1], ((0, 0), (0, LANES - 2 * rank))).astype(BF16)
        u, z, nk, nv, sk, sv = _inproj(x, mod[l], norm_mix_g[l][None, :], w_main, wz, seq=seq, w_a=w_a, w_kvc=w_ck,
                                       **sizes)
        nat_k.append(nk); nat_v.append(nv); swa_k.append(sk); swa_v.append(sv)

        oa = _ctx_attn(u, None, n_rows_out=n, nb=nb, seq=seq, qcol=0, kcol=1, vcol=2, wq=w_a, wk=w_a, hkv=h_a, g=1)
        oa = _na_attn(oa, u, kc_a, vc_a, _na_bias(na_rpb[l], t_lat), l, n_ctx=n_ctx, db=db, t_lat=t_lat, w_a=w_a,
                      heads=h_a)
        gw = jnp.zeros((2, LANES, w_bk), F32)
        gw = gw.at[0, 0:rank].set(gla_w_gate[l, 0]).at[1, rank:2 * rank].set(gla_w_gate[l, 1])
        gb = gla_b_gate[l][:, None, :]
        gn = gla_norm_g[l].reshape(1, w_bv)
        gla_args = dict(bcol=a_w // b_w, heads=h_b, dk=dk_b, dv=dv_b, n_rows_out=n)
        ob, st = _gla(None, u, z, gw, gb, gn, None, l, row0=0, nb=nb, seq_len=seq, out_state=True, **gla_args)
        (ob,) = _gla(ob, u, z, gw, gb, gn, state_gla, l, row0=n_ctx, nb=db, seq_len=t_lat, out_state=False, **gla_args)
        gla_s.append(st)
        c0 = a_w + b_w
        cols = dict(qcol=c0 // w_cq, kcol=(c0 + w_cq) // w_ck, vcol=(c0 + w_cq) // w_ck + 1, hkv=hkv_c, g=g_c)
        oc = _ctx_attn(u, swa_sink[l], n_rows_out=n, nb=nb, seq=seq, wq=w_cq, wk=w_ck, **cols)
        oc = _swa_attn(oc, u, swa_sink[l], kc_c, vc_c, cos, sin, l, n_ctx=n_ctx, db=db, t_lat=t_lat, **cols)

        moe = l % 2 == 1
        i = l // 2
        router = None
        if moe:
            wr = jnp.pad(moe_w_router[i], ((0, 0), (0, LANES - n_exp)))
            br = jnp.concatenate([moe_b_router[i], jnp.full((LANES - n_exp,), NEG_INF, F32)])[None, :]
            router = (wr, br, n_exp)
        res = _outproj(x, oa, ob, oc, w_out[l].astype(BF16), mod[l], norm_ffn_g[l][None, :], router, **sizes)
        fg = final_norm_g[None, :] if l == depth - 1 else None
        if moe:
            xn, h2, gates = res
            x = _ffn(h2, xn, gates, moe_w_gate[i].astype(BF16), moe_w_up[i].astype(BF16), moe_w_down[i].astype(BF16),
                     mod[l], fg, tf=256, **sizes)
        else:
            xn, h2 = res
            x = _ffn(h2, xn, None, ffn_w_gate[i][None].astype(BF16), ffn_w_up[i][None].astype(BF16),
                     ffn_w_down[i][None].astype(BF16), mod[l], fg, tf=512, **sizes)

    y_prompt = x[:n_ctx].reshape(nb, seq, d)
    y_sample = x[n_ctx:].reshape(db, t_lat, d)
    new_nat_k = jnp.stack(nat_k, axis=1).reshape(nb, depth, seq, h_a, HEAD_DIM)
    new_nat_v = jnp.stack(nat_v, axis=1).reshape(nb, depth, seq, h_a, HEAD_DIM)
    new_state = jnp.stack(gla_s, axis=1)
    new_swa_k = jnp.stack(swa_k, axis=1).reshape(nb, depth, seq, hkv_c, HEAD_DIM)
    new_swa_v = jnp.stack(swa_v, axis=1).reshape(nb, depth, seq, hkv_c, HEAD_DIM)
    return (y_prompt, y_sample, new_nat_k, new_nat_v, new_state, new_swa_k, new_swa_v)
```

```python
import functools

import numpy as np
import jax
import jax.numpy as jnp
from jax import lax
from jax.experimental import pallas as pl
from jax.experimental.pallas import tpu as pltpu

F32 = jnp.float32
BF16 = jnp.bfloat16
HIGHEST = lax.Precision.HIGHEST

EPS = 1e-6
NEG_INF = -1e30
GRID_W = 64
HEAD_DIM = 64
NA_ROWS = 8
NA_COLS = 16
SWA_WIN = 128
SWA_QBLOCK = 128
GLA_CHUNK = 64
GLA_TAU = 16.0
ROPE_BASE = 10000.0
LANES = 128
VMEM_LIMIT = 56 * 1024 * 1024


def _cparams(n_axes):
    return pltpu.CompilerParams(dimension_semantics=("arbitrary",) * n_axes, vmem_limit_bytes=VMEM_LIMIT)


def _dot(a, b, **kw):
    return jnp.dot(a, b, preferred_element_type=F32, **kw)


def _dot_nt(a, b, **kw):
    return lax.dot_general(a, b, (((1,), (1,)), ((), ())), preferred_element_type=F32, **kw)


def _dot_tn(a, b, **kw):
    return lax.dot_general(a, b, (((0,), (0,)), ((), ())), preferred_element_type=F32, **kw)


def _silu(x):
    return x / (1.0 + jnp.exp(-x))


def _rmsnorm(x, g):
    return x * lax.rsqrt(jnp.mean(x * x, axis=-1, keepdims=True) + EPS) * g


def _adaln_kernel(cond_ref, w_ref, b_ref, o_ref):
    s = _silu(cond_ref[...]).astype(BF16)
    o_ref[0] = _dot(s, w_ref[0].astype(BF16)) + b_ref[0]


def _adaln(cond8, w_ada, b_ada, tn=1024):
    depth, d, n6 = w_ada.shape
    return pl.pallas_call(
        _adaln_kernel,
        grid=(depth, n6 // tn),
        in_specs=[
            pl.BlockSpec((8, d), lambda l, j: (0, 0)),
            pl.BlockSpec((1, d, tn), lambda l, j: (l, 0, j)),
            pl.BlockSpec((1, 1, tn), lambda l, j: (l, 0, j)),
        ],
        out_specs=pl.BlockSpec((1, 8, tn), lambda l, j: (l, 0, j)),
        out_shape=jax.ShapeDtypeStruct((depth, 8, n6), F32),
        compiler_params=_cparams(2),
        name="adaln",
    )(cond8, w_ada, b_ada.reshape(depth, 1, n6))


def _inproj_kernel(x_ref, mod_ref, g_ref, w_ref, wz_ref, u_ref, z_ref, nk_ref, nv_ref, sk_ref, sv_ref, h_scr,
                   *, nct, seq, j_ka, j_va, j_kvc, w_kvc):
    i = pl.program_id(0)
    j = pl.program_id(1)

    @pl.when(j == 0)
    def _():
        h = _rmsnorm(x_ref[...], g_ref[...]) * (1.0 + mod_ref[0, 1:2, :]) + mod_ref[0, 0:1, :]
        hb = h.astype(BF16)
        h_scr[...] = hb
        z_ref[...] = _dot(hb, wz_ref[...])

    acc = _dot(h_scr[...], w_ref[...])
    u_ref[...] = acc.astype(BF16)
    is_ctx = i < nct
    nseq = nk_ref.shape[0]

    @pl.when(is_ctx & (j == j_ka))
    def _():
        for bb in range(nseq):
            nk_ref[bb] = acc[bb * seq:(bb + 1) * seq, :]

    @pl.when(is_ctx & (j == j_va))
    def _():
        for bb in range(nseq):
            nv_ref[bb] = acc[bb * seq:(bb + 1) * seq, :]

    @pl.when(is_ctx & (j == j_kvc))
    def _():
        for bb in range(nseq):
            sk_ref[bb] = acc[bb * seq:(bb + 1) * seq, 0:w_kvc]
            sv_ref[bb] = acc[bb * seq:(bb + 1) * seq, w_kvc:2 * w_kvc]


def _inproj(x, mod_l, g, w_main, wz, *, n_ctx, seq, t_lat, w_a, w_kvc, tm=512, tn=512):
    n, d = x.shape
    n_main = w_main.shape[1]
    nct = n_ctx // tm
    nseq = tm // seq
    nb = n_ctx // seq
    assert n_ctx % tm == 0 and tm % seq == 0 and t_lat % tm == 0 and w_a == tn and 2 * w_kvc == tn

    def cond_idx(i):
        return jnp.where(i < nct, 0, 1 + ((i - nct) * tm) // t_lat)

    kv_idx = lambda i, j: (jnp.minimum(i, nct - 1), 0, 0)
    kern = functools.partial(_inproj_kernel, nct=nct, seq=seq, j_ka=1, j_va=2, j_kvc=n_main // tn - 1, w_kvc=w_kvc)
    return pl.pallas_call(
        kern,
        grid=(n // tm, n_main // tn),
        in_specs=[
            pl.BlockSpec((tm, d), lambda i, j: (i, 0)),
            pl.BlockSpec((1, 6, d), lambda i, j: (cond_idx(i), 0, 0)),
            pl.BlockSpec((1, d), lambda i, j: (0, 0)),
            pl.BlockSpec((d, tn), lambda i, j: (0, j)),
            pl.BlockSpec((d, LANES), lambda i, j: (0, 0)),
        ],
        out_specs=[
            pl.BlockSpec((tm, tn), lambda i, j: (i, j)),
            pl.BlockSpec((tm, LANES), lambda i, j: (i, 0)),
            pl.BlockSpec((nseq, seq, w_a), kv_idx),
            pl.BlockSpec((nseq, seq, w_a), kv_idx),
            pl.BlockSpec((nseq, seq, w_kvc), kv_idx),
            pl.BlockSpec((nseq, seq, w_kvc), kv_idx),
        ],
        out_shape=[
            jax.ShapeDtypeStruct((n, n_main), BF16),
            jax.ShapeDtypeStruct((n, LANES), F32),
            jax.ShapeDtypeStruct((nb, seq, w_a), F32),
            jax.ShapeDtypeStruct((nb, seq, w_a), F32),
            jax.ShapeDtypeStruct((nb, seq, w_kvc), F32),
            jax.ShapeDtypeStruct((nb, seq, w_kvc), F32),
        ],
        scratch_shapes=[pltpu.VMEM((tm, d), BF16)],
        compiler_params=_cparams(2),
        name="inproj",
    )(x, mod_l, g, w_main, wz)


def _ctx_attn_kernel(*refs, hkv, g, has_sink):
    if has_sink:
        sink_ref, q_ref, k_ref, v_ref, o_ref = refs
    else:
        q_ref, k_ref, v_ref, o_ref = refs
    scale = HEAD_DIM ** -0.5
    q = q_ref[...]
    k = k_ref[...]
    v = v_ref[...]
    outs = []
    for kh in range(hkv):
        k_h = k[:, kh * HEAD_DIM:(kh + 1) * HEAD_DIM]
        v_h = v[:, kh * HEAD_DIM:(kh + 1) * HEAD_DIM]
        for gi in range(g):
            hq = kh * g + gi
            q_h = q[:, hq * HEAD_DIM:(hq + 1) * HEAD_DIM]
            s = _dot_nt(q_h, k_h) * scale
            m = jnp.max(s, axis=-1, keepdims=True)
            if has_sink:
                sk = sink_ref[kh, gi]
                m = jnp.maximum(m, sk)
            e = jnp.exp(s - m)
            den = jnp.sum(e, axis=-1, keepdims=True)
            if has_sink:
                den = den + jnp.exp(sk - m)
            outs.append(_dot(e.astype(BF16), v_h) / den)
    o_ref[...] = jnp.concatenate(outs, axis=-1).astype(BF16)


def _ctx_attn(u, sink, *, nb, seq, qcol, kcol, vcol, wq, wk, hkv, g):
    has_sink = sink is not None
    kern = functools.partial(_ctx_attn_kernel, hkv=hkv, g=g, has_sink=has_sink)
    in_specs = [
        pl.BlockSpec((seq, wq), lambda b: (b, qcol)),
        pl.BlockSpec((seq, wk), lambda b: (b, kcol)),
        pl.BlockSpec((seq, wk), lambda b: (b, vcol)),
    ]
    args = [u, u, u]
    if has_sink:
        in_specs = [pl.BlockSpec(memory_space=pltpu.SMEM)] + in_specs
        args = [sink] + args
    return pl.pallas_call(
        kern,
        grid=(nb,),
        in_specs=in_specs,
        out_specs=pl.BlockSpec((seq, wq), lambda b: (b, 0)),
        out_shape=jax.ShapeDtypeStruct((nb * seq, wq), BF16),
        compiler_params=_cparams(1),
        name="ctx_attn_sink" if has_sink else "ctx_attn",
    )(*args)


def _na_kernel(q_ref, k_ref, v_ref, kc_ref, vc_ref, bias_ref, o_ref, *, heads):
    scale = HEAD_DIM ** -0.5
    q = q_ref[...]
    k = k_ref[...]
    v = v_ref[...]
    kc = kc_ref[0, 0].astype(BF16)
    vc = vc_ref[0, 0].astype(BF16)
    outs = []
    for h in range(heads):
        sl = slice(h * HEAD_DIM, (h + 1) * HEAD_DIM)
        q_h = q[:, sl]
        s_w = _dot_nt(q_h, k[:, sl]) * scale + bias_ref[h]
        s_c = _dot_nt(q_h, kc[:, sl]) * scale
        m = jnp.maximum(jnp.max(s_w, axis=-1, keepdims=True), jnp.max(s_c, axis=-1, keepdims=True))
        e_w = jnp.exp(s_w - m)
        e_c = jnp.exp(s_c - m)
        den = jnp.sum(e_w, axis=-1, keepdims=True) + jnp.sum(e_c, axis=-1, keepdims=True)
        o = _dot(e_w.astype(BF16), v[:, sl]) + _dot(e_c.astype(BF16), vc[:, sl])
        outs.append(o / den)
    o_ref[...] = jnp.concatenate(outs, axis=-1).astype(BF16)


def _na_attn(u, kc, vc, bias, l, *, n_ctx, db, t_lat, w_a, heads, tq=256):
    nq = t_lat // tq
    p = kc.shape[2]
    kern = functools.partial(_na_kernel, heads=heads)
    return pl.pallas_call(
        kern,
        grid=(db, nq),
        in_specs=[
            pl.BlockSpec((tq, w_a), lambda b, i: (n_ctx // tq + b * nq + i, 0)),
            pl.BlockSpec((t_lat, w_a), lambda b, i: (n_ctx // t_lat + b, 1)),
            pl.BlockSpec((t_lat, w_a), lambda b, i: (n_ctx // t_lat + b, 2)),
            pl.BlockSpec((1, 1, p, w_a), lambda b, i: (b, l, 0, 0)),
            pl.BlockSpec((1, 1, p, w_a), lambda b, i: (b, l, 0, 0)),
            pl.BlockSpec((heads, tq, t_lat), lambda b, i: (0, i, 0)),
        ],
        out_specs=pl.BlockSpec((tq, w_a), lambda b, i: (b * nq + i, 0)),
        out_shape=jax.ShapeDtypeStruct((db * t_lat, w_a), BF16),
        compiler_params=_cparams(2),
        name="na_attn",
    )(u, u, u, kc, vc, bias)


def _na_bias(rpb, t_lat):
    heads = rpb.shape[0]
    rows = t_lat // GRID_W
    kh = min(NA_ROWS, rows)
    r = np.arange(rows)
    row_start = np.clip(r - kh // 2, 0, rows - kh)
    col = np.arange(GRID_W)
    col_start = np.clip(col - NA_COLS // 2, 0, GRID_W - NA_COLS)
    row_ok = (r[None, :] >= row_start[:, None]) & (r[None, :] < row_start[:, None] + kh)
    col_ok = (col[None, :] >= col_start[:, None]) & (col[None, :] < col_start[:, None] + NA_COLS)
    padc = GRID_W
    rp = jnp.pad(rpb.astype(F32), ((0, 0), (0, 0), (padc, padc)))
    cols = jnp.stack([rp[:, :, padc + NA_COLS - 1 - c: padc + NA_COLS - 1 - c + GRID_W] for c in range(GRID_W)], axis=2)
    padr = rows
    cp = jnp.pad(cols, ((0, 0), (padr, padr), (0, 0), (0, 0)))
    full = jnp.stack([cp[:, padr + NA_ROWS - 1 - rr: padr + NA_ROWS - 1 - rr + rows] for rr in range(rows)], axis=1)
    full = full.transpose(0, 1, 3, 2, 4)
    ok = row_ok[:, None, :, None] & col_ok[None, :, None, :]
    full = jnp.where(jnp.asarray(ok)[None], full, NEG_INF)
    return full.reshape(heads, t_lat, t_lat)


def _rope(x, cos, sin_signed):
    w = x.shape[-1]
    q4 = HEAD_DIM // 4
    lane = lax.broadcasted_iota(jnp.int32, x.shape, 1)
    first = (lane % (2 * q4)) < q4
    swapped = jnp.where(first, pltpu.roll(x, w - q4, 1), pltpu.roll(x, q4, 1))
    return x * cos + swapped * sin_signed


def _swa_kernel(sink_ref, q_ref, k_ref, v_ref, kc_ref, vc_ref, cos_ref, sin_ref, o_ref, k_scr, *, hkv, g, t_lat):
    n = pl.program_id(1)
    scale = HEAD_DIM ** -0.5
    qb = SWA_QBLOCK
    wk = hkv * HEAD_DIM
    span = 3 * qb

    @pl.when(n == 0)
    def _():
        k_scr[...] = _rope(k_ref[...].astype(F32), cos_ref[...], sin_ref[...]).astype(BF16)

    q0 = pl.multiple_of(n * qb, qb)
    start = pl.multiple_of(jnp.clip(n * qb - qb, 0, t_lat - span), qb)
    cos_q = cos_ref[pl.ds(q0, qb), :]
    sin_q = sin_ref[pl.ds(q0, qb), :]
    k_loc = k_scr[pl.ds(start, span), :]
    v_loc = v_ref[pl.ds(start, span), :]
    kc = kc_ref[0, 0].astype(BF16)
    vc = vc_ref[0, 0].astype(BF16)
    qpos = q0 + lax.broadcasted_iota(jnp.int32, (qb, span), 0)
    kpos = start + lax.broadcasted_iota(jnp.int32, (qb, span), 1)
    valid1 = jnp.abs(qpos - kpos) <= SWA_WIN
    valid = jnp.concatenate([valid1] * g, axis=0)
    outs = []
    for kh in range(hkv):
        q_r = _rope(q_ref[:, kh * wk:(kh + 1) * wk].astype(F32), cos_q, sin_q).astype(BF16)
        q_st = jnp.concatenate([q_r[:, gi * HEAD_DIM:(gi + 1) * HEAD_DIM] for gi in range(g)], axis=0)
        sk = jnp.concatenate([jnp.full((qb, 1), sink_ref[kh, gi], F32) for gi in range(g)], axis=0)
        ksl = slice(kh * HEAD_DIM, (kh + 1) * HEAD_DIM)
        s_l = jnp.where(valid, _dot_nt(q_st, k_loc[:, ksl]) * scale, NEG_INF)
        s_c = _dot_nt(q_st, kc[:, ksl]) * scale
        m = jnp.maximum(jnp.maximum(jnp.max(s_l, axis=-1, keepdims=True), jnp.max(s_c, axis=-1, keepdims=True)), sk)
        e_l = jnp.exp(s_l - m)
        e_c = jnp.exp(s_c - m)
        den = jnp.sum(e_l, axis=-1, keepdims=True) + jnp.sum(e_c, axis=-1, keepdims=True) + jnp.exp(sk - m)
        o = (_dot(e_l.astype(BF16), v_loc[:, ksl]) + _dot(e_c.astype(BF16), vc[:, ksl])) / den
        outs.extend(o[gi * qb:(gi + 1) * qb] for gi in range(g))
    o_ref[...] = jnp.concatenate(outs, axis=-1).astype(BF16)


def _swa_attn(u, sink, kc, vc, cos, sin, l, *, n_ctx, db, t_lat, qcol, kcol, vcol, hkv, g):
    qb = SWA_QBLOCK
    nq = t_lat // qb
    wq = hkv * g * HEAD_DIM
    wk = hkv * HEAD_DIM
    assert g * HEAD_DIM == wk, "rotary tables are shared between the q groups and k"
    p = kc.shape[2]
    kern = functools.partial(_swa_kernel, hkv=hkv, g=g, t_lat=t_lat)
    return pl.pallas_call(
        kern,
        grid=(db, nq),
        in_specs=[
            pl.BlockSpec(memory_space=pltpu.SMEM),
            pl.BlockSpec((qb, wq), lambda b, i: (n_ctx // qb + b * nq + i, qcol)),
            pl.BlockSpec((t_lat, wk), lambda b, i: (n_ctx // t_lat + b, kcol)),
            pl.BlockSpec((t_lat, wk), lambda b, i: (n_ctx // t_lat + b, vcol)),
            pl.BlockSpec((1, 1, p, wk), lambda b, i: (b, l, 0, 0)),
            pl.BlockSpec((1, 1, p, wk), lambda b, i: (b, l, 0, 0)),
            pl.BlockSpec((t_lat, wk), lambda b, i: (0, 0)),
            pl.BlockSpec((t_lat, wk), lambda b, i: (0, 0)),
        ],
        out_specs=pl.BlockSpec((qb, wq), lambda b, i: (b * nq + i, 0)),
        out_shape=jax.ShapeDtypeStruct((db * t_lat, wq), BF16),
        scratch_shapes=[pltpu.VMEM((t_lat, wk), BF16)],
        compiler_params=_cparams(2),
        name="swa_attn",
    )(sink, u, u, u, kc, vc, cos, sin)


def _rope_tables(t_lat, width):
    quarter = HEAD_DIM // 4
    pos = np.arange(t_lat)
    freqs = ROPE_BASE ** (-np.arange(quarter, dtype=np.float32) / quarter)
    ang_r = (pos // GRID_W).astype(np.float32)[:, None] * freqs[None, :]
    ang_c = (pos % GRID_W).astype(np.float32)[:, None] * freqs[None, :]
    ang_r, ang_c = jnp.asarray(ang_r, F32), jnp.asarray(ang_c, F32)
    cos = jnp.concatenate([jnp.cos(ang_r), jnp.cos(ang_r), jnp.cos(ang_c), jnp.cos(ang_c)], axis=-1)
    sin = jnp.concatenate([-jnp.sin(ang_r), jnp.sin(ang_r), -jnp.sin(ang_c), jnp.sin(ang_c)], axis=-1)
    reps = width // HEAD_DIM
    return jnp.tile(cos, (1, reps)), jnp.tile(sin, (1, reps))


def _log_sigmoid(x):
    return jnp.minimum(x, 0.0) - jnp.log(1.0 + jnp.exp(-jnp.abs(x)))


def _gla_kernel(*refs, heads, dk, dv, seq_len, has_s0, out_state):
    refs = list(refs)
    ub_ref, z_ref, gw_ref, gb_ref, gn_ref = refs[:5]
    refs = refs[5:]
    s0_ref = refs.pop(0) if has_s0 else None
    o_ref = refs.pop(0)
    st_ref = refs.pop(0) if out_state else None
    la_scr, o_scr, st_scr = refs
    c_len = GLA_CHUNK
    nc = seq_len // c_len
    wk = heads * dk
    wv = heads * dv

    z = z_ref[...]
    for d in range(2):
        pre = _dot(z, gw_ref[d], precision=HIGHEST) + gb_ref[d]
        la_scr[d] = _log_sigmoid(pre) / GLA_TAU

    eye_v = (lax.broadcasted_iota(jnp.int32, (dv, dv), 0) == lax.broadcasted_iota(jnp.int32, (dv, dv), 1)).astype(F32)
    eye_k = (lax.broadcasted_iota(jnp.int32, (dk, dk), 0) == lax.broadcasted_iota(jnp.int32, (dk, dk), 1)).astype(F32)
    for d in range(2):
        for h in range(heads):
            if has_s0:
                st_scr[d, h] = _dot_nt(eye_v, s0_ref[0, 0, d, h], precision=HIGHEST)
            else:
                st_scr[d, h] = jnp.zeros((dv, dk), F32)

    ti = lax.broadcasted_iota(jnp.int32, (c_len, c_len), 0)
    si = lax.broadcasted_iota(jnp.int32, (c_len, c_len), 1)

    def chunk(c, d):
        reverse = d == 1
        keep = (si >= ti) if reverse else (si <= ti)
        tri = keep.astype(F32)
        r0 = pl.multiple_of(c * c_len, c_len)
        rows = pl.ds(r0, c_len)
        b = _dot(tri, la_scr[d, rows, :], precision=HIGHEST)
        b_end = b[0:1] if reverse else b[c_len - 1:c_len]
        q = ub_ref[rows, 0:wk].astype(F32) * (dk ** -0.5)
        k = ub_ref[rows, wk:2 * wk].astype(F32)
        v = ub_ref[rows, 2 * wk:2 * wk + wv]
        qt = (q * jnp.exp(b)).astype(BF16)
        kt = (k * jnp.exp(-b)).astype(BF16)
        ke = (k * jnp.exp(b_end - b)).astype(BF16)
        dec = jnp.exp(b_end)
        for h in range(heads):
            ks = slice(h * dk, (h + 1) * dk)
            vs = slice(h * dv, (h + 1) * dv)
            a = jnp.where(keep, _dot_nt(qt[:, ks], kt[:, ks]), 0.0).astype(BF16)
            st = st_scr[d, h]
            o = _dot_nt(qt[:, ks], st.astype(BF16)) + _dot(a, v[:, vs])
            st_scr[d, h] = st * dec[:, ks] + _dot_tn(v[:, vs], ke[:, ks])
            if reverse:
                o_scr[rows, vs] += o
            else:
                o_scr[rows, vs] = o

    lax.fori_loop(0, nc, lambda c, carry: (chunk(c, 0), carry)[1], 0)
    lax.fori_loop(0, nc, lambda c, carry: (chunk(nc - 1 - c, 1), carry)[1], 0)

    r = ub_ref[:, 2 * wk + wv:2 * wk + 2 * wv].astype(F32)
    gn = gn_ref[...]
    for h in range(heads):
        vs = slice(h * dv, (h + 1) * dv)
        o_ref[:, vs] = (_rmsnorm(o_scr[:, vs], gn[:, vs]) * _silu(r[:, vs])).astype(BF16)
    if out_state:
        for d in range(2):
            for h in range(heads):
                st_ref[0, d, h] = _dot_nt(eye_k, st_scr[d, h], precision=HIGHEST)


def _gla(u, z, gw, gb, gn, s0, l, *, row0, nb, seq_len, bcol, heads, dk, dv, out_state):
    wk, wv = heads * dk, heads * dv
    wb = 2 * wk + 2 * wv
    has_s0 = s0 is not None
    rb = row0 // seq_len
    kern = functools.partial(_gla_kernel, heads=heads, dk=dk, dv=dv, seq_len=seq_len, has_s0=has_s0,
                             out_state=out_state)
    in_specs = [
        pl.BlockSpec((seq_len, wb), lambda b: (rb + b, bcol)),
        pl.BlockSpec((seq_len, LANES), lambda b: (rb + b, 0)),
        pl.BlockSpec((2, LANES, wk), lambda b: (0, 0, 0)),
        pl.BlockSpec((2, 1, wk), lambda b: (0, 0, 0)),
        pl.BlockSpec((1, wv), lambda b: (0, 0)),
    ]
    args = [u, z, gw, gb, gn]
    if has_s0:
        in_specs.append(pl.BlockSpec((1, 1, 2, heads, dk, dv), lambda b: (b, l, 0, 0, 0, 0)))
        args.append(s0)
    out_specs = [pl.BlockSpec((seq_len, wv), lambda b: (b, 0))]
    out_shape = [jax.ShapeDtypeStruct((nb * seq_len, wv), BF16)]
    if out_state:
        out_specs.append(pl.BlockSpec((1, 2, heads, dk, dv), lambda b: (b, 0, 0, 0, 0)))
        out_shape.append(jax.ShapeDtypeStruct((nb, 2, heads, dk, dv), F32))
    res = pl.pallas_call(
        kern,
        grid=(nb,),
        in_specs=in_specs,
        out_specs=out_specs,
        out_shape=out_shape,
        scratch_shapes=[
            pltpu.VMEM((2, seq_len, wk), F32),
            pltpu.VMEM((seq_len, wv), F32),
            pltpu.VMEM((2, heads, dv, dk), F32),
        ],
        compiler_params=_cparams(1),
        name="gla_lat" if has_s0 else "gla_ctx",
    )(*args)
    return res


ROUTE_I1, ROUTE_I2, ROUTE_W1, ROUTE_W2, ROUTE_R1, ROUTE_R2 = range(6)


def _pack_bf16_pairs(h):
    c = h.shape[1] // 2
    lo = lax.bitcast_convert_type(h[:, :c].astype(BF16).astype(F32), jnp.uint32)
    hi = lax.bitcast_convert_type(h[:, c:].astype(BF16).astype(F32), jnp.uint32)
    return hi | (lo >> 16)


def _unpack_bf16_pairs(w):
    lo = lax.bitcast_convert_type(w << 16, F32).astype(BF16)
    hi = lax.bitcast_convert_type(w & jnp.uint32(0xFFFF0000), F32).astype(BF16)
    return jnp.concatenate([lo, hi], axis=-1)


def _outproj_kernel(*refs, n_experts, top_k, nct):
    x_ref, ctx_refs, lat_refs, refs = refs[0], refs[1:4], refs[4:7], refs[7:]
    if n_experts:
        w_ref, mod_ref, g_ref, wr_ref, br_ref, xn_ref, h_ref, route_ref, counts_ref, run_scr = refs
    else:
        w_ref, mod_ref, g_ref, xn_ref, h_ref = refs
    is_ctx = pl.program_id(0) < nct
    mixed = jnp.concatenate([jnp.where(is_ctx, c[...], t[...]) for c, t in zip(ctx_refs, lat_refs)], axis=-1)
    xn = x_ref[...] + mod_ref[0, 2:3, :] * _dot(mixed, w_ref[...])
    xn_ref[...] = xn
    h = _rmsnorm(xn, g_ref[...]) * (1.0 + mod_ref[0, 4:5, :]) + mod_ref[0, 3:4, :]
    if not n_experts:
        h_ref[...] = h.astype(BF16)
        return
    assert top_k == 2
    h_ref[...] = _pack_bf16_pairs(h)
    h_hi = h.astype(BF16)
    h_lo = (h - h_hi.astype(F32)).astype(BF16)
    wr = wr_ref[...]
    w_hi = wr.astype(BF16)
    w_lo = (wr - w_hi.astype(F32)).astype(BF16)
    logits = _dot(h_hi, w_hi) + (_dot(h_hi, w_lo) + _dot(h_lo, w_hi)) + br_ref[...]
    tm = logits.shape[0]
    lane = lax.broadcasted_iota(jnp.int32, logits.shape, 1)
    l1 = jnp.max(logits, axis=-1, keepdims=True)
    i1 = jnp.min(jnp.where(logits == l1, lane, LANES), axis=-1, keepdims=True)
    rest = jnp.where(lane == i1, -jnp.inf, logits)
    l2 = jnp.max(rest, axis=-1, keepdims=True)
    i2 = jnp.min(jnp.where(rest == l2, lane, LANES), axis=-1, keepdims=True)
    e2 = jnp.exp(l2 - l1)
    w1 = 1.0 / (1.0 + e2)
    w2 = e2 * w1
    @pl.when(pl.program_id(0) == 0)
    def _():
        run_scr[...] = jnp.zeros(run_scr.shape, F32)

    oh1 = lane == i1
    oh2 = lane == i2
    earlier = (lax.broadcasted_iota(jnp.int32, (tm, tm), 1) < lax.broadcasted_iota(jnp.int32, (tm, tm), 0)).astype(BF16)
    c1 = _dot(earlier, oh1.astype(BF16))
    c2 = _dot(earlier, oh2.astype(BF16))
    n1 = jnp.sum(oh1.astype(F32), axis=0, keepdims=True)
    n2 = jnp.sum(oh2.astype(F32), axis=0, keepdims=True)
    run = run_scr[0:1, :]
    r1 = jnp.sum(jnp.where(oh1, run + c1, 0.0), axis=-1, keepdims=True)
    r2 = jnp.sum(jnp.where(oh2, run + n1 + c2, 0.0), axis=-1, keepdims=True)
    total = run + n1 + n2
    run_scr[...] = jnp.broadcast_to(total, run_scr.shape)
    counts_ref[...] = jnp.broadcast_to(total, counts_ref.shape)
    rec = jnp.zeros(logits.shape, F32)
    for slot, val in ((ROUTE_I1, i1.astype(F32)), (ROUTE_I2, i2.astype(F32)), (ROUTE_W1, w1), (ROUTE_W2, w2),
                      (ROUTE_R1, r1), (ROUTE_R2, r2)):
        rec = jnp.where(lane == slot, val, rec)
    route_ref[...] = rec


def _outproj(x, mixed_ctx, mixed_lat, w_out, mod_l, g, router, *, n_ctx, t_lat, tm=512):
    n, d = x.shape
    nct = n_ctx // tm
    nlt = n // tm - nct

    def cond_idx(i):
        return jnp.where(i < nct, 0, 1 + ((i - nct) * tm) // t_lat)

    row = lambda i: (i, 0)
    const = lambda i: (0, 0)
    ctx_row = lambda i: (jnp.minimum(i, nct - 1), 0)
    lat_row = lambda i: (jnp.clip(i - nct, 0, nlt - 1), 0)
    in_specs = (
        [pl.BlockSpec((tm, d), row)]
        + [pl.BlockSpec((tm, o.shape[1]), ctx_row) for o in mixed_ctx]
        + [pl.BlockSpec((tm, o.shape[1]), lat_row) for o in mixed_lat]
        + [pl.BlockSpec((d, d), const), pl.BlockSpec((1, 6, d), lambda i: (cond_idx(i), 0, 0)), pl.BlockSpec((1, d), const)]
    )
    args = [x, *mixed_ctx, *mixed_lat, w_out, mod_l, g]
    n_experts = 0
    scratch = []
    if router is None:
        out_specs = [pl.BlockSpec((tm, d), row), pl.BlockSpec((tm, d), row)]
        out_shape = [jax.ShapeDtypeStruct((n, d), F32), jax.ShapeDtypeStruct((n, d), BF16)]
    else:
        wr, br, n_experts = router
        in_specs += [pl.BlockSpec((d, LANES), const), pl.BlockSpec((1, LANES), const)]
        args += [wr, br]
        out_specs = [pl.BlockSpec((tm, d), row), pl.BlockSpec((tm, d // 2), row), pl.BlockSpec((tm, LANES), row),
                     pl.BlockSpec((8, LANES), const)]
        out_shape = [jax.ShapeDtypeStruct((n, d), F32), jax.ShapeDtypeStruct((n, d // 2), jnp.uint32),
                     jax.ShapeDtypeStruct((n, LANES), F32), jax.ShapeDtypeStruct((8, LANES), F32)]
        scratch = [pltpu.VMEM((8, LANES), F32)]
    kern = functools.partial(_outproj_kernel, n_experts=n_experts, top_k=2, nct=nct)
    return pl.pallas_call(
        kern,
        grid=(n // tm,),
        in_specs=in_specs,
        out_specs=out_specs,
        out_shape=out_shape,
        scratch_shapes=scratch,
        compiler_params=_cparams(1),
        name="outproj_router" if n_experts else "outproj",
    )(*args)


def _row_copy(src_ref, src_row, dst_ref, dst_row, sem):
    return pltpu.make_async_copy(src_ref.at[pl.ds(src_row, 1), :], dst_ref.at[pl.ds(dst_row, 1), :], sem)


def _dispatch_kernel(pos_ref, hp_ref, hs_in_ref, hs_ref, sem, *, n, tm):
    del hs_in_ref
    base = pl.program_id(0) * tm

    def copies(k):
        return (_row_copy(hp_ref, k, hs_ref, pos_ref[base + k], sem),
                _row_copy(hp_ref, k, hs_ref, pos_ref[n + base + k], sem))

    def start(k, carry):
        for cp in copies(k):
            cp.start()
        return carry

    def wait(k, carry):
        for cp in copies(k):
            cp.wait()
        return carry

    lax.fori_loop(0, tm, start, 0, unroll=8)
    lax.fori_loop(0, tm, wait, 0, unroll=8)


def _dispatch(pos, hp, n_rows_sorted, tm=512):
    n, c = hp.shape
    hs0 = jnp.zeros((n_rows_sorted, c), hp.dtype)
    return pl.pallas_call(
        functools.partial(_dispatch_kernel, n=n, tm=tm),
        grid_spec=pltpu.PrefetchScalarGridSpec(
            num_scalar_prefetch=1,
            grid=(n // tm,),
            in_specs=[pl.BlockSpec((tm, c), lambda i, pos: (i, 0)), pl.BlockSpec(memory_space=pl.ANY)],
            out_specs=pl.BlockSpec(memory_space=pl.ANY),
            scratch_shapes=[pltpu.SemaphoreType.DMA(())],
        ),
        out_shape=jax.ShapeDtypeStruct((n_rows_sorted, c), hp.dtype),
        input_output_aliases={2: 0},
        compiler_params=_cparams(1),
        name="moe_dispatch",
    )(pos, hp, hs0)


def _moe_ffn_kernel(texp_ref, trow_ref, nact_ref, hs_ref, wg_ref, wu_ref, wd_ref, ys_ref, h_scr):
    del texp_ref, trow_ref
    i = pl.program_id(0)
    f = pl.program_id(1)
    active = i < nact_ref[0]

    @pl.when(active & (f == 0))
    def _():
        h_scr[...] = _unpack_bf16_pairs(hs_ref[...])

    def contribution():
        h = h_scr[...]
        t = _silu(_dot(h, wg_ref[0])) * _dot(h, wu_ref[0])
        return _dot(t.astype(BF16), wd_ref[0])

    @pl.when(active & (f == 0))
    def _():
        ys_ref[...] = contribution()

    @pl.when(active & (f > 0))
    def _():
        ys_ref[...] += contribution()

    @pl.when(jnp.logical_not(active) & (f == 0))
    def _():
        ys_ref[...] = jnp.zeros(ys_ref.shape, F32)


def _moe_ffn(tile_exp, tile_row, n_active, hs, wg, wu, wd, *, tm, tf):
    r, c = hs.shape
    n_e, d, ff = wg.shape
    nf = ff // tf

    def f_eff(i, f, nact):
        return jnp.where(i < nact[0], f, nf - 1)

    return pl.pallas_call(
        _moe_ffn_kernel,
        grid_spec=pltpu.PrefetchScalarGridSpec(
            num_scalar_prefetch=3,
            grid=(r // tm, nf),
            in_specs=[
                pl.BlockSpec((tm, c), lambda i, f, te, tr, na: (tr[i], 0)),
                pl.BlockSpec((1, d, tf), lambda i, f, te, tr, na: (te[i], 0, f_eff(i, f, na))),
                pl.BlockSpec((1, d, tf), lambda i, f, te, tr, na: (te[i], 0, f_eff(i, f, na))),
                pl.BlockSpec((1, tf, d), lambda i, f, te, tr, na: (te[i], f_eff(i, f, na), 0)),
            ],
            out_specs=pl.BlockSpec((tm, d), lambda i, f, te, tr, na: (i, 0)),
            scratch_shapes=[pltpu.VMEM((tm, d), BF16)],
        ),
        out_shape=jax.ShapeDtypeStruct((r, d), F32),
        compiler_params=_cparams(2),
        name="moe_ffn",
    )(tile_exp, tile_row, n_active, hs, wg, wu, wd)


def _combine_kernel(*refs, n, tm, final_norm):
    refs = list(refs)
    pos_ref, ys_ref, x_ref, route_ref, mod_ref = refs[:5]
    refs = refs[5:]
    fg_ref = refs.pop(0) if final_norm else None
    o_ref, buf, sem = refs
    base = pl.program_id(0) * tm

    def copies(k):
        return (_row_copy(ys_ref, pos_ref[base + k], buf.at[0], k, sem),
                _row_copy(ys_ref, pos_ref[n + base + k], buf.at[1], k, sem))

    def start(k, carry):
        for cp in copies(k):
            cp.start()
        return carry

    def wait(k, carry):
        for cp in copies(k):
            cp.wait()
        return carry

    lax.fori_loop(0, tm, start, 0, unroll=8)
    lax.fori_loop(0, tm, wait, 0, unroll=8)
    route = route_ref[...]
    lane = lax.broadcasted_iota(jnp.int32, route.shape, 1)
    w1 = jnp.sum(jnp.where(lane == ROUTE_W1, route, 0.0), axis=-1, keepdims=True)
    w2 = jnp.sum(jnp.where(lane == ROUTE_W2, route, 0.0), axis=-1, keepdims=True)
    y = x_ref[...] + mod_ref[0, 5:6, :] * (w1 * buf[0] + w2 * buf[1])
    if final_norm:
        y = _rmsnorm(y, fg_ref[...])
    o_ref[...] = y


def _combine(pos, ys, x, route, mod_l, final_g, *, n_ctx, t_lat, tm=512):
    n, d = x.shape
    nct = n_ctx // tm

    def cond_idx(i):
        return jnp.where(i < nct, 0, 1 + ((i - nct) * tm) // t_lat)

    in_specs = [
        pl.BlockSpec(memory_space=pl.ANY),
        pl.BlockSpec((tm, d), lambda i, pos: (i, 0)),
        pl.BlockSpec((tm, LANES), lambda i, pos: (i, 0)),
        pl.BlockSpec((1, 6, d), lambda i, pos: (cond_idx(i), 0, 0)),
    ]
    args = [ys, x, route, mod_l]
    if final_g is not None:
        in_specs.append(pl.BlockSpec((1, d), lambda i, pos: (0, 0)))
        args.append(final_g)
    return pl.pallas_call(
        functools.partial(_combine_kernel, n=n, tm=tm, final_norm=final_g is not None),
        grid_spec=pltpu.PrefetchScalarGridSpec(
            num_scalar_prefetch=1,
            grid=(n // tm,),
            in_specs=in_specs,
            out_specs=pl.BlockSpec((tm, d), lambda i, pos: (i, 0)),
            scratch_shapes=[pltpu.VMEM((2, tm, d), F32), pltpu.SemaphoreType.DMA(())],
        ),
        out_shape=jax.ShapeDtypeStruct((n, d), F32),
        compiler_params=_cparams(1),
        name="moe_combine",
    )(pos, *args)


def _routing_tables(route, counts, n_experts, tm):
    n = route.shape[0]
    n_tiles = 2 * n // tm + n_experts
    cnt = counts[0, :n_experts].astype(jnp.int32)
    tiles_e = (cnt + tm - 1) // tm
    tile_end = jnp.cumsum(tiles_e)
    offs = (tile_end - tiles_e) * tm
    i1 = route[:, ROUTE_I1].astype(jnp.int32)
    i2 = route[:, ROUTE_I2].astype(jnp.int32)
    eidx = jnp.arange(n_experts, dtype=jnp.int32)[None, :]
    off1 = jnp.sum(jnp.where(i1[:, None] == eidx, offs[None, :], 0), axis=1)
    off2 = jnp.sum(jnp.where(i2[:, None] == eidx, offs[None, :], 0), axis=1)
    pos = jnp.concatenate([off1 + route[:, ROUTE_R1].astype(jnp.int32), off2 + route[:, ROUTE_R2].astype(jnp.int32)])
    n_active = tile_end[-1]
    t = jnp.arange(n_tiles, dtype=jnp.int32)
    t_eff = jnp.minimum(t, n_active - 1)
    tile_exp = jnp.sum((t_eff[:, None] >= tile_end[None, :]).astype(jnp.int32), axis=1)
    return pos, tile_exp.astype(jnp.int32), t_eff.astype(jnp.int32), n_active.reshape(1).astype(jnp.int32), n_tiles


def _ffn_kernel(*refs, final_norm):
    refs = list(refs)
    h_ref, x_ref, wg_ref, wu_ref, wd_ref, mod_ref = refs[:6]
    refs = refs[6:]
    fg_ref = refs.pop(0) if final_norm else None
    (o_ref,) = refs
    f = pl.program_id(1)

    @pl.when(f == 0)
    def _():
        o_ref[...] = jnp.zeros(o_ref.shape, F32)

    h = h_ref[...]
    t = _silu(_dot(h, wg_ref[...])) * _dot(h, wu_ref[...])
    o_ref[...] += _dot(t.astype(BF16), wd_ref[...])

    @pl.when(f == pl.num_programs(1) - 1)
    def _():
        y = x_ref[...] + mod_ref[0, 5:6, :] * o_ref[...]
        if final_norm:
            y = _rmsnorm(y, fg_ref[...])
        o_ref[...] = y


def _ffn(h, x, wg, wu, wd, mod_l, final_g, *, n_ctx, t_lat, tm=512, tf=512):
    n, d = x.shape
    ff = wg.shape[1]
    nct = n_ctx // tm

    def cond_idx(i):
        return jnp.where(i < nct, 0, 1 + ((i - nct) * tm) // t_lat)

    row = lambda i, f: (i, 0)
    in_specs = [
        pl.BlockSpec((tm, d), row),
        pl.BlockSpec((tm, d), row),
        pl.BlockSpec((d, tf), lambda i, f: (0, f)),
        pl.BlockSpec((d, tf), lambda i, f: (0, f)),
        pl.BlockSpec((tf, d), lambda i, f: (f, 0)),
        pl.BlockSpec((1, 6, d), lambda i, f: (cond_idx(i), 0, 0)),
    ]
    args = [h, x, wg, wu, wd, mod_l]
    if final_g is not None:
        in_specs.append(pl.BlockSpec((1, d), lambda i, f: (0, 0)))
        args.append(final_g)
    return pl.pallas_call(
        functools.partial(_ffn_kernel, final_norm=final_g is not None),
        grid=(n // tm, ff // tf),
        in_specs=in_specs,
        out_specs=pl.BlockSpec((tm, d), row),
        out_shape=jax.ShapeDtypeStruct((n, d), F32),
        compiler_params=_cparams(2),
        name="ffn_dense",
    )(*args)


def kernel(x_prompt, x_sample, cache_nat_k, cache_nat_v, state_gla, cache_swa_k, cache_swa_v, c, c_ctx, w_ada, b_ada,
           norm_mix_g, norm_ffn_g, w_in, na_rpb, gla_w_gate, gla_b_gate, gla_norm_g, swa_sink, w_out, ffn_w_gate,
           ffn_w_up, ffn_w_down, moe_w_router, moe_b_router, moe_w_gate, moe_w_up, moe_w_down, final_norm_g):
    nb, seq, d = x_prompt.shape
    db, t_lat, _ = x_sample.shape
    depth = w_in.shape[0]
    p_len, h_a = cache_nat_k.shape[2], cache_nat_k.shape[3]
    h_b, dk_b, dv_b = state_gla.shape[3:6]
    hkv_c, g_c = swa_sink.shape[1], swa_sink.shape[2]
    rank = gla_w_gate.shape[2]
    n_exp = moe_w_router.shape[-1]
    w_a = h_a * HEAD_DIM
    w_bk = h_b * dk_b
    w_bv = h_b * dv_b
    w_cq = hkv_c * g_c * HEAD_DIM
    w_ck = hkv_c * HEAD_DIM
    n_ctx = nb * seq
    n_lat = db * t_lat
    n = n_ctx + n_lat
    z0 = 3 * w_a + 2 * w_bk + 2 * w_bv
    z1 = z0 + 2 * rank
    a_w = 3 * w_a
    b_w = 2 * w_bk + 2 * w_bv
    assert a_w == b_w and (a_w + b_w) % w_cq == 0 and (a_w + b_w + w_cq) % w_ck == 0 and 2 * rank <= LANES
    assert db + 1 <= 8

    x = jnp.concatenate([x_prompt.reshape(n_ctx, d), x_sample.reshape(n_lat, d)], axis=0)
    cond8 = jnp.concatenate([c_ctx[None, :], c, jnp.zeros((7 - db, d), F32)], axis=0)
    mod = _adaln(cond8, w_ada, b_ada).reshape(depth, 8, 6, d)

    cos, sin = _rope_tables(t_lat, w_ck)
    kc_a = cache_nat_k.reshape(db, depth, p_len, w_a)
    vc_a = cache_nat_v.reshape(db, depth, p_len, w_a)
    kc_c = cache_swa_k.reshape(db, depth, p_len, w_ck)
    vc_c = cache_swa_v.reshape(db, depth, p_len, w_ck)
    sizes = dict(n_ctx=n_ctx, t_lat=t_lat)

    nat_k, nat_v, gla_s, swa_k, swa_v = [], [], [], [], []
    for l in range(depth):
        w_l = w_in[l]
        w_main = jnp.concatenate([w_l[:, :z0], w_l[:, z1:]], axis=1).astype(BF16)
        wz = jnp.pad(w_l[:, z0:z1], ((0, 0), (0, LANES - 2 * rank))).astype(BF16)
        u, z, nk, nv, sk, sv = _inproj(x, mod[l], norm_mix_g[l][None, :], w_main, wz, seq=seq, w_a=w_a, w_kvc=w_ck,
                                       **sizes)
        nat_k.append(nk); nat_v.append(nv); swa_k.append(sk); swa_v.append(sv)

        oa_c = _ctx_attn(u, None, nb=nb, seq=seq, qcol=0, kcol=1, vcol=2, wq=w_a, wk=w_a, hkv=h_a, g=1)
        oa_l = _na_attn(u, kc_a, vc_a, _na_bias(na_rpb[l], t_lat), l, n_ctx=n_ctx, db=db, t_lat=t_lat, w_a=w_a,
                        heads=h_a)
        gw = jnp.zeros((2, LANES, w_bk), F32)
        gw = gw.at[0, 0:rank].set(gla_w_gate[l, 0]).at[1, rank:2 * rank].set(gla_w_gate[l, 1])
        gb = gla_b_gate[l][:, None, :]
        gn = gla_norm_g[l].reshape(1, w_bv)
        gla_args = dict(bcol=a_w // b_w, heads=h_b, dk=dk_b, dv=dv_b)
        ob_c, st = _gla(u, z, gw, gb, gn, None, l, row0=0, nb=nb, seq_len=seq, out_state=True, **gla_args)
        (ob_l,) = _gla(u, z, gw, gb, gn, state_gla, l, row0=n_ctx, nb=db, seq_len=t_lat, out_state=False, **gla_args)
        gla_s.append(st)
        c0 = a_w + b_w
        cols = dict(qcol=c0 // w_cq, kcol=(c0 + w_cq) // w_ck, vcol=(c0 + w_cq) // w_ck + 1, hkv=hkv_c, g=g_c)
        oc_c = _ctx_attn(u, swa_sink[l], nb=nb, seq=seq, wq=w_cq, wk=w_ck, **cols)
        oc_l = _swa_attn(u, swa_sink[l], kc_c, vc_c, cos, sin, l, n_ctx=n_ctx, db=db, t_lat=t_lat, **cols)

        moe = l % 2 == 1
        i = l // 2
        router = None
        if moe:
            wr = jnp.pad(moe_w_router[i], ((0, 0), (0, LANES - n_exp)))
            br = jnp.concatenate([moe_b_router[i], jnp.full((LANES - n_exp,), NEG_INF, F32)])[None, :]
            router = (wr, br, n_exp)
        res = _outproj(x, (oa_c, ob_c, oc_c), (oa_l, ob_l, oc_l), w_out[l].astype(BF16), mod[l],
                       norm_ffn_g[l][None, :], router, **sizes)
        fg = final_norm_g[None, :] if l == depth - 1 else None
        if moe:
            xn, hp, route, counts = res
            tm_e = 512
            pos, tile_exp, tile_row, n_active, n_tiles = _routing_tables(route, counts, n_exp, tm_e)
            hs = _dispatch(pos, hp, n_tiles * tm_e)
            ys = _moe_ffn(tile_exp, tile_row, n_active, hs, moe_w_gate[i].astype(BF16), moe_w_up[i].astype(BF16),
                          moe_w_down[i].astype(BF16), tm=tm_e, tf=moe_w_gate.shape[-1] // 2)
            x = _combine(pos, ys, xn, route, mod[l], fg, **sizes)
        else:
            xn, h2 = res
            x = _ffn(h2, xn, ffn_w_gate[i].astype(BF16), ffn_w_up[i].astype(BF16), ffn_w_down[i].astype(BF16),
                     mod[l], fg, **sizes)

    y_prompt = x[:n_ctx].reshape(nb, seq, d)
    y_sample = x[n_ctx:].reshape(db, t_lat, d)
    new_nat_k = jnp.stack(nat_k, axis=1).reshape(nb, depth, seq, h_a, HEAD_DIM)
    new_nat_v = jnp.stack(nat_v, axis=1).reshape(nb, depth, seq, h_a, HEAD_DIM)
    new_state = jnp.stack(gla_s, axis=1)
    new_swa_k = jnp.stack(swa_k, axis=1).reshape(nb, depth, seq, hkv_c, HEAD_DIM)
    new_swa_v = jnp.stack(swa_v, axis=1).reshape(nb, depth, seq, hkv_c, HEAD_DIM)
    return (y_prompt, y_sample, new_nat_k, new_nat_v, new_state, new_swa_k, new_swa_v)
```

```python
import functools

import numpy as np
import jax
import jax.numpy as jnp
from jax import lax
from jax.experimental import pallas as pl
from jax.experimental.pallas import tpu as pltpu

F32 = jnp.float32
BF16 = jnp.bfloat16
HIGHEST = lax.Precision.HIGHEST

EPS = 1e-6
NEG_INF = -1e30
GRID_W = 64
HEAD_DIM = 64
NA_ROWS = 8
NA_COLS = 16
SWA_WIN = 128
SWA_QBLOCK = 128
GLA_CHUNK = 64
GLA_TAU = 16.0
ROPE_BASE = 10000.0
LANES = 128
VMEM_LIMIT = 56 * 1024 * 1024


def _cparams(n_axes):
    return pltpu.CompilerParams(dimension_semantics=("arbitrary",) * n_axes, vmem_limit_bytes=VMEM_LIMIT)


def _dot(a, b, **kw):
    return jnp.dot(a, b, preferred_element_type=F32, **kw)


def _dot_nt(a, b, **kw):
    return lax.dot_general(a, b, (((1,), (1,)), ((), ())), preferred_element_type=F32, **kw)


def _dot_tn(a, b, **kw):
    return lax.dot_general(a, b, (((0,), (0,)), ((), ())), preferred_element_type=F32, **kw)


def _silu(x):
    return x / (1.0 + jnp.exp(-x))


def _rmsnorm(x, g):
    return x * lax.rsqrt(jnp.mean(x * x, axis=-1, keepdims=True) + EPS) * g


def _adaln_kernel(cond_ref, w_ref, b_ref, o_ref):
    s = _silu(cond_ref[...]).astype(BF16)
    o_ref[0] = _dot(s, w_ref[0].astype(BF16)) + b_ref[0]


def _adaln(cond8, w_ada, b_ada, tn=1024):
    depth, d, n6 = w_ada.shape
    return pl.pallas_call(
        _adaln_kernel,
        grid=(depth, n6 // tn),
        in_specs=[
            pl.BlockSpec((8, d), lambda l, j: (0, 0)),
            pl.BlockSpec((1, d, tn), lambda l, j: (l, 0, j)),
            pl.BlockSpec((1, 1, tn), lambda l, j: (l, 0, j)),
        ],
        out_specs=pl.BlockSpec((1, 8, tn), lambda l, j: (l, 0, j)),
        out_shape=jax.ShapeDtypeStruct((depth, 8, n6), F32),
        compiler_params=_cparams(2),
        name="adaln",
    )(cond8, w_ada, b_ada.reshape(depth, 1, n6))


def _inproj_kernel(*refs, nct, seq, w_a, w_cq, w_kvc, split_x):
    refs = list(refs)
    x_ref = refs.pop(0)
    xl_ref = refs.pop(0) if split_x else None
    mod_ref, g_ref, w_ref, wz_ref = refs[:4]
    u_ref, z_ref, nk_ref, nv_ref, sk_ref, sv_ref, h_scr = refs[8:]
    i = pl.program_id(0)
    j = pl.program_id(1)
    is_ctx = i < nct

    @pl.when(j == 0)
    def _():
        x = jnp.where(is_ctx, x_ref[...], xl_ref[...]) if split_x else x_ref[...]
        h = _rmsnorm(x, g_ref[...]) * (1.0 + mod_ref[0, 1:2, :]) + mod_ref[0, 0:1, :]
        hb = h.astype(BF16)
        h_scr[...] = hb
        z_ref[...] = _dot(hb, wz_ref[...])

    acc = _dot(h_scr[...], w_ref[...])
    u_ref[...] = acc.astype(BF16)
    nseq = nk_ref.shape[0]

    @pl.when(is_ctx & (j == 0))
    def _():
        for bb in range(nseq):
            nk_ref[bb, 0] = acc[bb * seq:(bb + 1) * seq, w_a:2 * w_a]
            nv_ref[bb, 0] = acc[bb * seq:(bb + 1) * seq, 2 * w_a:3 * w_a]

    @pl.when(is_ctx & (j == 2))
    def _():
        for bb in range(nseq):
            sk_ref[bb, 0] = acc[bb * seq:(bb + 1) * seq, w_cq:w_cq + w_kvc]
            sv_ref[bb, 0] = acc[bb * seq:(bb + 1) * seq, w_cq + w_kvc:w_cq + 2 * w_kvc]


def _inproj(x, x_lat, mod_l, g, w_main, wz, kv_bufs, l, *, n_ctx, seq, t_lat, w_a, w_cq, w_kvc, tm=512):
    split_x = x_lat is not None
    d = x.shape[1]
    n = x.shape[0] + (x_lat.shape[0] if split_x else 0)
    n_main = w_main.shape[1]
    tn = 3 * w_a
    nct = n_ctx // tm
    nlt = n // tm - nct
    nseq = tm // seq
    assert n_ctx % tm == 0 and tm % seq == 0 and t_lat % tm == 0 and n_main == 3 * tn and w_cq + 2 * w_kvc == tn

    def cond_idx(i):
        return jnp.where(i < nct, 0, 1 + ((i - nct) * tm) // t_lat)

    kv_idx = lambda i, j: (jnp.minimum(i, nct - 1), l, 0, 0)
    if split_x:
        x_specs = [pl.BlockSpec((tm, d), lambda i, j: (jnp.minimum(i, nct - 1), 0)),
                   pl.BlockSpec((tm, d), lambda i, j: (jnp.clip(i - nct, 0, nlt - 1), 0))]
        x_args = [x, x_lat]
    else:
        x_specs = [pl.BlockSpec((tm, d), lambda i, j: (i, 0))]
        x_args = [x]
    n_in = len(x_args) + 4
    kern = functools.partial(_inproj_kernel, nct=nct, seq=seq, w_a=w_a, w_cq=w_cq, w_kvc=w_kvc, split_x=split_x)
    return pl.pallas_call(
        kern,
        grid=(n // tm, n_main // tn),
        in_specs=x_specs + [
            pl.BlockSpec((1, 6, d), lambda i, j: (cond_idx(i), 0, 0)),
            pl.BlockSpec((1, d), lambda i, j: (0, 0)),
            pl.BlockSpec((d, tn), lambda i, j: (0, j)),
            pl.BlockSpec((d, LANES), lambda i, j: (0, 0)),
        ] + [pl.BlockSpec(memory_space=pl.ANY)] * 4,
        out_specs=[
            pl.BlockSpec((tm, tn), lambda i, j: (i, j)),
            pl.BlockSpec((tm, LANES), lambda i, j: (i, 0)),
        ] + [pl.BlockSpec((nseq, 1, seq, b.shape[-1]), kv_idx) for b in kv_bufs],
        out_shape=[
            jax.ShapeDtypeStruct((n, n_main), BF16),
            jax.ShapeDtypeStruct((n, LANES), F32),
        ] + [jax.ShapeDtypeStruct(b.shape, b.dtype) for b in kv_bufs],
        scratch_shapes=[pltpu.VMEM((tm, d), BF16)],
        input_output_aliases={n_in + k: 2 + k for k in range(4)},
        compiler_params=_cparams(2),
        name="inproj",
    )(*x_args, mod_l, g, w_main, wz, *kv_bufs)


def _ctx_attn_kernel(*refs, hkv, g, has_sink):
    if has_sink:
        sink_ref, q_ref, k_ref, v_ref, o_ref = refs
    else:
        q_ref, k_ref, v_ref, o_ref = refs
    scale = HEAD_DIM ** -0.5
    q = q_ref[...]
    k = k_ref[...]
    v = v_ref[...]
    outs = []
    for kh in range(hkv):
        k_h = k[:, kh * HEAD_DIM:(kh + 1) * HEAD_DIM]
        v_h = v[:, kh * HEAD_DIM:(kh + 1) * HEAD_DIM]
        for gi in range(g):
            hq = kh * g + gi
            q_h = q[:, hq * HEAD_DIM:(hq + 1) * HEAD_DIM]
            s = _dot_nt(q_h, k_h) * scale
            m = jnp.max(s, axis=-1, keepdims=True)
            if has_sink:
                sk = sink_ref[kh, gi]
                m = jnp.maximum(m, sk)
            e = jnp.exp(s - m)
            den = jnp.sum(e, axis=-1, keepdims=True)
            if has_sink:
                den = den + jnp.exp(sk - m)
            outs.append(_dot(e.astype(BF16), v_h) / den)
    o_ref[...] = jnp.concatenate(outs, axis=-1).astype(BF16)


def _ctx_attn(u, sink, *, nb, seq, qcol, kcol, vcol, wq, wk, hkv, g):
    has_sink = sink is not None
    kern = functools.partial(_ctx_attn_kernel, hkv=hkv, g=g, has_sink=has_sink)
    in_specs = [
        pl.BlockSpec((seq, wq), lambda b: (b, qcol)),
        pl.BlockSpec((seq, wk), lambda b: (b, kcol)),
        pl.BlockSpec((seq, wk), lambda b: (b, vcol)),
    ]
    args = [u, u, u]
    if has_sink:
        in_specs = [pl.BlockSpec(memory_space=pltpu.SMEM)] + in_specs
        args = [sink] + args
    return pl.pallas_call(
        kern,
        grid=(nb,),
        in_specs=in_specs,
        out_specs=pl.BlockSpec((seq, wq), lambda b: (b, 0)),
        out_shape=jax.ShapeDtypeStruct((nb * seq, wq), BF16),
        compiler_params=_cparams(1),
        name="ctx_attn_sink" if has_sink else "ctx_attn",
    )(*args)


NA_QROWS = 4
NA_WIN_ROWS = 12


def _na_kernel(q_ref, k_ref, v_ref, kc_ref, vc_ref, bias_ref, o_ref, *, heads, rows):
    scale = HEAD_DIM ** -0.5
    qi = pl.program_id(1)
    r0 = qi * NA_QROWS
    w0 = jnp.clip(r0 - NA_ROWS // 2, 0, rows - NA_WIN_ROWS)
    k0 = pl.multiple_of(w0 * GRID_W, NA_QROWS * GRID_W)
    tq = NA_QROWS * GRID_W
    tk = NA_WIN_ROWS * GRID_W
    q = q_ref[...]
    k = k_ref[pl.ds(k0, tk), :]
    v = v_ref[pl.ds(k0, tk), :]
    kc = kc_ref[0, 0].astype(BF16)
    vc = vc_ref[0, 0].astype(BF16)
    qrow = r0 + lax.broadcasted_iota(jnp.int32, (tq, tk), 0) // GRID_W
    krow = w0 + lax.broadcasted_iota(jnp.int32, (tq, tk), 1) // GRID_W
    band0 = jnp.clip(qrow - NA_ROWS // 2, 0, rows - NA_ROWS)
    in_band = (krow >= band0) & (krow < band0 + NA_ROWS)
    outs = []
    for h in range(heads):
        sl = slice(h * HEAD_DIM, (h + 1) * HEAD_DIM)
        q_h = q[:, sl]
        bias = jnp.concatenate(
            [jnp.concatenate([bias_ref[h, w0 + 2 * p - (r0 + rq) + 2 * NA_ROWS - 1] for p in range(NA_WIN_ROWS // 2)],
                             axis=1) for rq in range(NA_QROWS)], axis=0)
        s_w = jnp.where(in_band, _dot_nt(q_h, k[:, sl]) * scale + bias, NEG_INF)
        s_c = _dot_nt(q_h, kc[:, sl]) * scale
        m = jnp.maximum(jnp.max(s_w, axis=-1, keepdims=True), jnp.max(s_c, axis=-1, keepdims=True))
        e_w = jnp.exp(s_w - m)
        e_c = jnp.exp(s_c - m)
        den = jnp.sum(e_w, axis=-1, keepdims=True) + jnp.sum(e_c, axis=-1, keepdims=True)
        o = _dot(e_w.astype(BF16), v[:, sl]) + _dot(e_c.astype(BF16), vc[:, sl])
        outs.append(o / den)
    o_ref[...] = jnp.concatenate(outs, axis=-1).astype(BF16)


def _na_attn(u, kc, vc, bias, l, *, n_ctx, db, t_lat, w_a, heads):
    tq = NA_QROWS * GRID_W
    rows = t_lat // GRID_W
    nq = t_lat // tq
    p = kc.shape[2]
    assert rows >= NA_WIN_ROWS and NA_WIN_ROWS >= NA_QROWS + NA_ROWS - 1 and (rows - NA_WIN_ROWS) % NA_QROWS == 0
    kern = functools.partial(_na_kernel, heads=heads, rows=rows)
    return pl.pallas_call(
        kern,
        grid=(db, nq),
        in_specs=[
            pl.BlockSpec((tq, w_a), lambda b, i: (n_ctx // tq + b * nq + i, 0)),
            pl.BlockSpec((t_lat, w_a), lambda b, i: (n_ctx // t_lat + b, 1)),
            pl.BlockSpec((t_lat, w_a), lambda b, i: (n_ctx // t_lat + b, 2)),
            pl.BlockSpec((1, 1, p, w_a), lambda b, i: (b, l, 0, 0)),
            pl.BlockSpec((1, 1, p, w_a), lambda b, i: (b, l, 0, 0)),
            pl.BlockSpec((None,) + bias.shape[1:], lambda b, i: (l, 0, 0, 0, 0)),
        ],
        out_specs=pl.BlockSpec((tq, w_a), lambda b, i: (b * nq + i, 0)),
        out_shape=jax.ShapeDtypeStruct((db * t_lat, w_a), BF16),
        compiler_params=_cparams(2),
        name="na_attn",
    )(u, u, u, kc, vc, bias)


def _na_bias_tables(rpb):
    col = np.arange(GRID_W)
    col_start = np.clip(col - NA_COLS // 2, 0, GRID_W - NA_COLS)
    col_ok = (col[None, :] >= col_start[:, None]) & (col[None, :] < col_start[:, None] + NA_COLS)
    dc = np.clip(col[None, :] - col[:, None] + NA_COLS - 1, 0, 2 * NA_COLS - 2)
    pick = (np.arange(2 * NA_COLS - 1)[:, None, None] == dc[None]).astype(np.float32)
    t1 = jnp.einsum("lhdj,jck->lhdck", rpb.astype(F32), jnp.asarray(pick), precision=HIGHEST)
    t1 = jnp.where(jnp.asarray(col_ok), t1, NEG_INF)
    t1 = jnp.pad(t1, ((0, 0), (0, 0), (NA_ROWS, NA_ROWS), (0, 0), (0, 0)), constant_values=NEG_INF)
    return jnp.concatenate([t1[:, :, :-1], t1[:, :, 1:]], axis=-1)


def _rope(x, cos, sin_signed):
    w = x.shape[-1]
    q4 = HEAD_DIM // 4
    lane = lax.broadcasted_iota(jnp.int32, x.shape, 1)
    first = (lane % (2 * q4)) < q4
    swapped = jnp.where(first, pltpu.roll(x, w - q4, 1), pltpu.roll(x, q4, 1))
    return x * cos + swapped * sin_signed


def _swa_kernel(sink_ref, q_ref, k_ref, v_ref, kc_ref, vc_ref, cos_ref, sin_ref, o_ref, k_scr, *, hkv, g, t_lat):
    n = pl.program_id(1)
    scale = HEAD_DIM ** -0.5
    qb = SWA_QBLOCK
    wk = hkv * HEAD_DIM
    span = 3 * qb

    @pl.when(n == 0)
    def _():
        k_scr[...] = _rope(k_ref[...].astype(F32), cos_ref[...], sin_ref[...]).astype(BF16)

    q0 = pl.multiple_of(n * qb, qb)
    start = pl.multiple_of(jnp.clip(n * qb - qb, 0, t_lat - span), qb)
    cos_q = cos_ref[pl.ds(q0, qb), :]
    sin_q = sin_ref[pl.ds(q0, qb), :]
    k_loc = k_scr[pl.ds(start, span), :]
    v_loc = v_ref[pl.ds(start, span), :]
    kc = kc_ref[0, 0].astype(BF16)
    vc = vc_ref[0, 0].astype(BF16)
    qpos = q0 + lax.broadcasted_iota(jnp.int32, (qb, span), 0)
    kpos = start + lax.broadcasted_iota(jnp.int32, (qb, span), 1)
    valid1 = jnp.abs(qpos - kpos) <= SWA_WIN
    valid = jnp.concatenate([valid1] * g, axis=0)
    outs = []
    for kh in range(hkv):
        q_r = _rope(q_ref[:, kh * wk:(kh + 1) * wk].astype(F32), cos_q, sin_q).astype(BF16)
        q_st = jnp.concatenate([q_r[:, gi * HEAD_DIM:(gi + 1) * HEAD_DIM] for gi in range(g)], axis=0)
        sk = jnp.concatenate([jnp.full((qb, 1), sink_ref[kh, gi], F32) for gi in range(g)], axis=0)
        ksl = slice(kh * HEAD_DIM, (kh + 1) * HEAD_DIM)
        s_l = jnp.where(valid, _dot_nt(q_st, k_loc[:, ksl]) * scale, NEG_INF)
        s_c = _dot_nt(q_st, kc[:, ksl]) * scale
        m = jnp.maximum(jnp.maximum(jnp.max(s_l, axis=-1, keepdims=True), jnp.max(s_c, axis=-1, keepdims=True)), sk)
        e_l = jnp.exp(s_l - m)
        e_c = jnp.exp(s_c - m)
        den = jnp.sum(e_l, axis=-1, keepdims=True) + jnp.sum(e_c, axis=-1, keepdims=True) + jnp.exp(sk - m)
        o = (_dot(e_l.astype(BF16), v_loc[:, ksl]) + _dot(e_c.astype(BF16), vc[:, ksl])) / den
        outs.extend(o[gi * qb:(gi + 1) * qb] for gi in range(g))
    o_ref[...] = jnp.concatenate(outs, axis=-1).astype(BF16)


def _swa_attn(u, sink, kc, vc, cos, sin, l, *, n_ctx, db, t_lat, qcol, kcol, vcol, hkv, g):
    qb = SWA_QBLOCK
    nq = t_lat // qb
    wq = hkv * g * HEAD_DIM
    wk = hkv * HEAD_DIM
    assert g * HEAD_DIM == wk, "rotary tables are shared between the q groups and k"
    p = kc.shape[2]
    kern = functools.partial(_swa_kernel, hkv=hkv, g=g, t_lat=t_lat)
    return pl.pallas_call(
        kern,
        grid=(db, nq),
        in_specs=[
            pl.BlockSpec(memory_space=pltpu.SMEM),
            pl.BlockSpec((qb, wq), lambda b, i: (n_ctx // qb + b * nq + i, qcol)),
            pl.BlockSpec((t_lat, wk), lambda b, i: (n_ctx // t_lat + b, kcol)),
            pl.BlockSpec((t_lat, wk), lambda b, i: (n_ctx // t_lat + b, vcol)),
            pl.BlockSpec((1, 1, p, wk), lambda b, i: (b, l, 0, 0)),
            pl.BlockSpec((1, 1, p, wk), lambda b, i: (b, l, 0, 0)),
            pl.BlockSpec((t_lat, wk), lambda b, i: (0, 0)),
            pl.BlockSpec((t_lat, wk), lambda b, i: (0, 0)),
        ],
        out_specs=pl.BlockSpec((qb, wq), lambda b, i: (b * nq + i, 0)),
        out_shape=jax.ShapeDtypeStruct((db * t_lat, wq), BF16),
        scratch_shapes=[pltpu.VMEM((t_lat, wk), BF16)],
        compiler_params=_cparams(2),
        name="swa_attn",
    )(sink, u, u, u, kc, vc, cos, sin)


def _rope_tables(t_lat, width):
    quarter = HEAD_DIM // 4
    pos = np.arange(t_lat)
    freqs = ROPE_BASE ** (-np.arange(quarter, dtype=np.float32) / quarter)
    ang_r = (pos // GRID_W).astype(np.float32)[:, None] * freqs[None, :]
    ang_c = (pos % GRID_W).astype(np.float32)[:, None] * freqs[None, :]
    ang_r, ang_c = jnp.asarray(ang_r, F32), jnp.asarray(ang_c, F32)
    cos = jnp.concatenate([jnp.cos(ang_r), jnp.cos(ang_r), jnp.cos(ang_c), jnp.cos(ang_c)], axis=-1)
    sin = jnp.concatenate([-jnp.sin(ang_r), jnp.sin(ang_r), -jnp.sin(ang_c), jnp.sin(ang_c)], axis=-1)
    reps = width // HEAD_DIM
    return jnp.tile(cos, (1, reps)), jnp.tile(sin, (1, reps))


def _log_sigmoid(x):
    return jnp.minimum(x, 0.0) - jnp.log(1.0 + jnp.exp(-jnp.abs(x)))


def _gla_kernel(*refs, heads, dk, dv, seq_len, has_s0, out_state):
    refs = list(refs)
    ub_ref, z_ref, gw_ref, gb_ref, gn_ref = refs[:5]
    refs = refs[5:]
    s0_ref = refs.pop(0) if has_s0 else None
    if out_state:
        refs.pop(0)
    o_ref = refs.pop(0)
    st_ref = refs.pop(0) if out_state else None
    la_scr, o_scr, st_scr = refs
    c_len = GLA_CHUNK
    nc = seq_len // c_len
    wk = heads * dk
    wv = heads * dv

    z = z_ref[...]
    for d in range(2):
        pre = _dot(z, gw_ref[d], precision=HIGHEST) + gb_ref[d]
        la_scr[d] = _log_sigmoid(pre) / GLA_TAU

    eye_v = (lax.broadcasted_iota(jnp.int32, (dv, dv), 0) == lax.broadcasted_iota(jnp.int32, (dv, dv), 1)).astype(F32)
    eye_k = (lax.broadcasted_iota(jnp.int32, (dk, dk), 0) == lax.broadcasted_iota(jnp.int32, (dk, dk), 1)).astype(F32)
    for d in range(2):
        for h in range(heads):
            if has_s0:
                st_scr[d, h] = _dot_nt(eye_v, s0_ref[0, 0, d, h], precision=HIGHEST)
            else:
                st_scr[d, h] = jnp.zeros((dv, dk), F32)

    ti = lax.broadcasted_iota(jnp.int32, (c_len, c_len), 0)
    si = lax.broadcasted_iota(jnp.int32, (c_len, c_len), 1)

    def chunk(c, d):
        reverse = d == 1
        keep = (si >= ti) if reverse else (si <= ti)
        tri = keep.astype(F32)
        r0 = pl.multiple_of(c * c_len, c_len)
        rows = pl.ds(r0, c_len)
        b = _dot(tri, la_scr[d, rows, :], precision=HIGHEST)
        b_end = b[0:1] if reverse else b[c_len - 1:c_len]
        q = ub_ref[rows, 0:wk].astype(F32) * (dk ** -0.5)
        k = ub_ref[rows, wk:2 * wk].astype(F32)
        v = ub_ref[rows, 2 * wk:2 * wk + wv]
        qt = (q * jnp.exp(b)).astype(BF16)
        kt = (k * jnp.exp(-b)).astype(BF16)
        ke = (k * jnp.exp(b_end - b)).astype(BF16)
        dec = jnp.exp(b_end)
        for h in range(heads):
            ks = slice(h * dk, (h + 1) * dk)
            vs = slice(h * dv, (h + 1) * dv)
            a = jnp.where(keep, _dot_nt(qt[:, ks], kt[:, ks]), 0.0).astype(BF16)
            st = st_scr[d, h]
            o = _dot_nt(qt[:, ks], st.astype(BF16)) + _dot(a, v[:, vs])
            st_scr[d, h] = st * dec[:, ks] + _dot_tn(v[:, vs], ke[:, ks])
            if reverse:
                o_scr[rows, vs] += o
            else:
                o_scr[rows, vs] = o

    lax.fori_loop(0, nc, lambda c, carry: (chunk(c, 0), carry)[1], 0)
    lax.fori_loop(0, nc, lambda c, carry: (chunk(nc - 1 - c, 1), carry)[1], 0)

    r = ub_ref[:, 2 * wk + wv:2 * wk + 2 * wv].astype(F32)
    gn = gn_ref[...]
    for h in range(heads):
        vs = slice(h * dv, (h + 1) * dv)
        o_ref[:, vs] = (_rmsnorm(o_scr[:, vs], gn[:, vs]) * _silu(r[:, vs])).astype(BF16)
    if out_state:
        for d in range(2):
            for h in range(heads):
                st_ref[0, 0, d, h] = _dot_nt(eye_k, st_scr[d, h], precision=HIGHEST)


def _gla(u, z, gw, gb, gn, s0, st_buf, l, *, row0, nb, seq_len, bcol, heads, dk, dv):
    wk, wv = heads * dk, heads * dv
    wb = 2 * wk + 2 * wv
    has_s0 = s0 is not None
    out_state = st_buf is not None
    rb = row0 // seq_len
    kern = functools.partial(_gla_kernel, heads=heads, dk=dk, dv=dv, seq_len=seq_len, has_s0=has_s0,
                             out_state=out_state)
    in_specs = [
        pl.BlockSpec((seq_len, wb), lambda b: (rb + b, bcol)),
        pl.BlockSpec((seq_len, LANES), lambda b: (rb + b, 0)),
        pl.BlockSpec((2, LANES, wk), lambda b: (0, 0, 0)),
        pl.BlockSpec((2, 1, wk), lambda b: (0, 0, 0)),
        pl.BlockSpec((1, wv), lambda b: (0, 0)),
    ]
    args = [u, z, gw, gb, gn]
    if has_s0:
        in_specs.append(pl.BlockSpec((1, 1, 2, heads, dk, dv), lambda b: (b, l, 0, 0, 0, 0)))
        args.append(s0)
    out_specs = [pl.BlockSpec((seq_len, wv), lambda b: (b, 0))]
    out_shape = [jax.ShapeDtypeStruct((nb * seq_len, wv), BF16)]
    aliases = {}
    if out_state:
        aliases = {len(args): 1}
        in_specs.append(pl.BlockSpec(memory_space=pl.ANY))
        args.append(st_buf)
        out_specs.append(pl.BlockSpec((1, 1, 2, heads, dk, dv), lambda b: (b, l, 0, 0, 0, 0)))
        out_shape.append(jax.ShapeDtypeStruct(st_buf.shape, st_buf.dtype))
    res = pl.pallas_call(
        kern,
        grid=(nb,),
        in_specs=in_specs,
        out_specs=out_specs,
        out_shape=out_shape,
        scratch_shapes=[
            pltpu.VMEM((2, seq_len, wk), F32),
            pltpu.VMEM((seq_len, wv), F32),
            pltpu.VMEM((2, heads, dv, dk), F32),
        ],
        input_output_aliases=aliases,
        compiler_params=_cparams(1),
        name="gla_lat" if has_s0 else "gla_ctx",
    )(*args)
    return res


ROUTE_I1, ROUTE_I2, ROUTE_W1, ROUTE_W2, ROUTE_R1, ROUTE_R2 = range(6)


def _pack_bf16_pairs(h):
    c = h.shape[1] // 2
    lo = lax.bitcast_convert_type(h[:, :c].astype(BF16).astype(F32), jnp.uint32)
    hi = lax.bitcast_convert_type(h[:, c:].astype(BF16).astype(F32), jnp.uint32)
    return hi | (lo >> 16)


def _unpack_bf16_pairs(w):
    lo = lax.bitcast_convert_type(w << 16, F32).astype(BF16)
    hi = lax.bitcast_convert_type(w & jnp.uint32(0xFFFF0000), F32).astype(BF16)
    return jnp.concatenate([lo, hi], axis=-1)


def _outproj_kernel(*refs, n_experts, top_k, nct, split_x):
    refs = list(refs)
    x_ref = refs.pop(0)
    xl_ref = refs.pop(0) if split_x else None
    ctx_refs, lat_refs, refs = refs[0:3], refs[3:6], refs[6:]
    if n_experts:
        w_ref, mod_ref, g_ref, wr_ref, br_ref, xn_ref, h_ref, route_ref, counts_ref, run_scr = refs
    else:
        w_ref, mod_ref, g_ref, xn_ref, h_ref = refs
    is_ctx = pl.program_id(0) < nct
    mixed = jnp.concatenate([jnp.where(is_ctx, c[...], t[...]) for c, t in zip(ctx_refs, lat_refs)], axis=-1)
    x = jnp.where(is_ctx, x_ref[...], xl_ref[...]) if split_x else x_ref[...]
    xn = x + mod_ref[0, 2:3, :] * _dot(mixed, w_ref[...])
    xn_ref[...] = xn
    h = _rmsnorm(xn, g_ref[...]) * (1.0 + mod_ref[0, 4:5, :]) + mod_ref[0, 3:4, :]
    if not n_experts:
        h_ref[...] = h.astype(BF16)
        return
    assert top_k == 2
    h_ref[...] = _pack_bf16_pairs(h)
    h_hi = h.astype(BF16)
    h_lo = (h - h_hi.astype(F32)).astype(BF16)
    wr = wr_ref[...]
    w_hi = wr.astype(BF16)
    w_lo = (wr - w_hi.astype(F32)).astype(BF16)
    logits = _dot(h_hi, w_hi) + (_dot(h_hi, w_lo) + _dot(h_lo, w_hi)) + br_ref[...]
    tm = logits.shape[0]
    lane = lax.broadcasted_iota(jnp.int32, logits.shape, 1)
    l1 = jnp.max(logits, axis=-1, keepdims=True)
    i1 = jnp.min(jnp.where(logits == l1, lane, LANES), axis=-1, keepdims=True)
    rest = jnp.where(lane == i1, -jnp.inf, logits)
    l2 = jnp.max(rest, axis=-1, keepdims=True)
    i2 = jnp.min(jnp.where(rest == l2, lane, LANES), axis=-1, keepdims=True)
    e2 = jnp.exp(l2 - l1)
    w1 = 1.0 / (1.0 + e2)
    w2 = e2 * w1
    @pl.when(pl.program_id(0) == 0)
    def _():
        run_scr[...] = jnp.zeros(run_scr.shape, F32)

    oh1 = lane == i1
    oh2 = lane == i2
    earlier = (lax.broadcasted_iota(jnp.int32, (tm, tm), 1) < lax.broadcasted_iota(jnp.int32, (tm, tm), 0)).astype(BF16)
    c1 = _dot(earlier, oh1.astype(BF16))
    c2 = _dot(earlier, oh2.astype(BF16))
    n1 = jnp.sum(oh1.astype(F32), axis=0, keepdims=True)
    n2 = jnp.sum(oh2.astype(F32), axis=0, keepdims=True)
    run = run_scr[0:1, :]
    r1 = jnp.sum(jnp.where(oh1, run + c1, 0.0), axis=-1, keepdims=True)
    r2 = jnp.sum(jnp.where(oh2, run + n1 + c2, 0.0), axis=-1, keepdims=True)
    total = run + n1 + n2
    run_scr[...] = jnp.broadcast_to(total, run_scr.shape)
    counts_ref[...] = jnp.broadcast_to(total, counts_ref.shape)
    rec = jnp.zeros(logits.shape, F32)
    for slot, val in ((ROUTE_I1, i1.astype(F32)), (ROUTE_I2, i2.astype(F32)), (ROUTE_W1, w1), (ROUTE_W2, w2),
                      (ROUTE_R1, r1), (ROUTE_R2, r2)):
        rec = jnp.where(lane == slot, val, rec)
    route_ref[...] = rec


def _outproj(x, x_lat, mixed_ctx, mixed_lat, w_out, mod_l, g, router, *, n_ctx, t_lat, tm=512):
    split_x = x_lat is not None
    d = x.shape[1]
    n = x.shape[0] + (x_lat.shape[0] if split_x else 0)
    nct = n_ctx // tm
    nlt = n // tm - nct

    def cond_idx(i):
        return jnp.where(i < nct, 0, 1 + ((i - nct) * tm) // t_lat)

    row = lambda i: (i, 0)
    const = lambda i: (0, 0)
    ctx_row = lambda i: (jnp.minimum(i, nct - 1), 0)
    lat_row = lambda i: (jnp.clip(i - nct, 0, nlt - 1), 0)
    x_specs = [pl.BlockSpec((tm, d), ctx_row), pl.BlockSpec((tm, d), lat_row)] if split_x else [pl.BlockSpec((tm, d), row)]
    x_args = [x, x_lat] if split_x else [x]
    in_specs = (
        x_specs
        + [pl.BlockSpec((tm, o.shape[1]), ctx_row) for o in mixed_ctx]
        + [pl.BlockSpec((tm, o.shape[1]), lat_row) for o in mixed_lat]
        + [pl.BlockSpec((d, d), const), pl.BlockSpec((1, 6, d), lambda i: (cond_idx(i), 0, 0)), pl.BlockSpec((1, d), const)]
    )
    args = [*x_args, *mixed_ctx, *mixed_lat, w_out, mod_l, g]
    n_experts = 0
    scratch = []
    if router is None:
        out_specs = [pl.BlockSpec((tm, d), row), pl.BlockSpec((tm, d), row)]
        out_shape = [jax.ShapeDtypeStruct((n, d), F32), jax.ShapeDtypeStruct((n, d), BF16)]
    else:
        wr, br, n_experts = router
        in_specs += [pl.BlockSpec((d, LANES), const), pl.BlockSpec((1, LANES), const)]
        args += [wr, br]
        out_specs = [pl.BlockSpec((tm, d), row), pl.BlockSpec((tm, d // 2), row), pl.BlockSpec((tm, LANES), row),
                     pl.BlockSpec((8, LANES), const)]
        out_shape = [jax.ShapeDtypeStruct((n, d), F32), jax.ShapeDtypeStruct((n, d // 2), jnp.uint32),
                     jax.ShapeDtypeStruct((n, LANES), F32), jax.ShapeDtypeStruct((8, LANES), F32)]
        scratch = [pltpu.VMEM((8, LANES), F32)]
    kern = functools.partial(_outproj_kernel, n_experts=n_experts, top_k=2, nct=nct, split_x=split_x)
    return pl.pallas_call(
        kern,
        grid=(n // tm,),
        in_specs=in_specs,
        out_specs=out_specs,
        out_shape=out_shape,
        scratch_shapes=scratch,
        compiler_params=_cparams(1),
        name="outproj_router" if n_experts else "outproj",
    )(*args)


def _row_copy(src_ref, src_row, dst_ref, dst_row, sem):
    return pltpu.make_async_copy(src_ref.at[pl.ds(src_row, 1), :], dst_ref.at[pl.ds(dst_row, 1), :], sem)


def _dispatch_kernel(pos_ref, hp_ref, hs_in_ref, hs_ref, sem, *, n, tm):
    del hs_in_ref
    base = pl.program_id(0) * tm

    def copies(k):
        return (_row_copy(hp_ref, k, hs_ref, pos_ref[base + k], sem),
                _row_copy(hp_ref, k, hs_ref, pos_ref[n + base + k], sem))

    def start(k, carry):
        for cp in copies(k):
            cp.start()
        return carry

    def wait(k, carry):
        for cp in copies(k):
            cp.wait()
        return carry

    lax.fori_loop(0, tm, start, 0, unroll=8)
    lax.fori_loop(0, tm, wait, 0, unroll=8)


def _dispatch(pos, hp, n_rows_sorted, tm=512):
    n, c = hp.shape
    hs0 = jnp.zeros((n_rows_sorted, c), hp.dtype)
    return pl.pallas_call(
        functools.partial(_dispatch_kernel, n=n, tm=tm),
        grid_spec=pltpu.PrefetchScalarGridSpec(
            num_scalar_prefetch=1,
            grid=(n // tm,),
            in_specs=[pl.BlockSpec((tm, c), lambda i, pos: (i, 0)), pl.BlockSpec(memory_space=pl.ANY)],
            out_specs=pl.BlockSpec(memory_space=pl.ANY),
            scratch_shapes=[pltpu.SemaphoreType.DMA(())],
        ),
        out_shape=jax.ShapeDtypeStruct((n_rows_sorted, c), hp.dtype),
        input_output_aliases={2: 0},
        compiler_params=_cparams(1),
        name="moe_dispatch",
    )(pos, hp, hs0)


def _moe_ffn_kernel(texp_ref, trow_ref, nact_ref, hs_ref, wg_ref, wu_ref, wd_ref, ys_ref, h_scr):
    del texp_ref, trow_ref
    i = pl.program_id(0)
    f = pl.program_id(1)
    active = i < nact_ref[0]

    @pl.when(active & (f == 0))
    def _():
        h_scr[...] = _unpack_bf16_pairs(hs_ref[...])

    def contribution():
        h = h_scr[...]
        t = _silu(_dot(h, wg_ref[0])) * _dot(h, wu_ref[0])
        return _dot(t.astype(BF16), wd_ref[0])

    @pl.when(active & (f == 0))
    def _():
        ys_ref[...] = contribution()

    @pl.when(active & (f > 0))
    def _():
        ys_ref[...] += contribution()

    @pl.when(jnp.logical_not(active) & (f == 0))
    def _():
        ys_ref[...] = jnp.zeros(ys_ref.shape, F32)


def _moe_ffn(tile_exp, tile_row, n_active, hs, wg, wu, wd, *, tm, tf):
    r, c = hs.shape
    n_e, d, ff = wg.shape
    nf = ff // tf

    def f_eff(i, f, nact):
        return jnp.where(i < nact[0], f, nf - 1)

    return pl.pallas_call(
        _moe_ffn_kernel,
        grid_spec=pltpu.PrefetchScalarGridSpec(
            num_scalar_prefetch=3,
            grid=(r // tm, nf),
            in_specs=[
                pl.BlockSpec((tm, c), lambda i, f, te, tr, na: (tr[i], 0)),
                pl.BlockSpec((1, d, tf), lambda i, f, te, tr, na: (te[i], 0, f_eff(i, f, na))),
                pl.BlockSpec((1, d, tf), lambda i, f, te, tr, na: (te[i], 0, f_eff(i, f, na))),
                pl.BlockSpec((1, tf, d), lambda i, f, te, tr, na: (te[i], f_eff(i, f, na), 0)),
            ],
            out_specs=pl.BlockSpec((tm, d), lambda i, f, te, tr, na: (i, 0)),
            scratch_shapes=[pltpu.VMEM((tm, d), BF16)],
        ),
        out_shape=jax.ShapeDtypeStruct((r, d), F32),
        compiler_params=_cparams(2),
        name="moe_ffn",
    )(tile_exp, tile_row, n_active, hs, wg, wu, wd)


def _combine_kernel(*refs, n, tm, final_norm, nct):
    refs = list(refs)
    pos_ref, ys_ref, x_ref, route_ref, mod_ref = refs[:5]
    refs = refs[5:]
    fg_ref = refs.pop(0) if final_norm else None
    o_ref = refs.pop(0)
    ol_ref = refs.pop(0) if final_norm else None
    buf, sem = refs
    base = pl.program_id(0) * tm

    def copies(k):
        return (_row_copy(ys_ref, pos_ref[base + k], buf.at[0], k, sem),
                _row_copy(ys_ref, pos_ref[n + base + k], buf.at[1], k, sem))

    def start(k, carry):
        for cp in copies(k):
            cp.start()
        return carry

    def wait(k, carry):
        for cp in copies(k):
            cp.wait()
        return carry

    lax.fori_loop(0, tm, start, 0, unroll=8)
    lax.fori_loop(0, tm, wait, 0, unroll=8)
    route = route_ref[...]
    lane = lax.broadcasted_iota(jnp.int32, route.shape, 1)
    w1 = jnp.sum(jnp.where(lane == ROUTE_W1, route, 0.0), axis=-1, keepdims=True)
    w2 = jnp.sum(jnp.where(lane == ROUTE_W2, route, 0.0), axis=-1, keepdims=True)
    y = x_ref[...] + mod_ref[0, 5:6, :] * (w1 * buf[0] + w2 * buf[1])
    if not final_norm:
        o_ref[...] = y
        return
    y = _rmsnorm(y, fg_ref[...])
    is_ctx = pl.program_id(0) < nct

    @pl.when(is_ctx)
    def _():
        o_ref[...] = y

    @pl.when(jnp.logical_not(is_ctx))
    def _():
        ol_ref[...] = y


def _combine(pos, ys, x, route, mod_l, final_g, *, n_ctx, t_lat, tm=512):
    n, d = x.shape
    nct = n_ctx // tm
    nlt = n // tm - nct

    def cond_idx(i):
        return jnp.where(i < nct, 0, 1 + ((i - nct) * tm) // t_lat)

    in_specs = [
        pl.BlockSpec(memory_space=pl.ANY),
        pl.BlockSpec((tm, d), lambda i, pos: (i, 0)),
        pl.BlockSpec((tm, LANES), lambda i, pos: (i, 0)),
        pl.BlockSpec((1, 6, d), lambda i, pos: (cond_idx(i), 0, 0)),
    ]
    args = [ys, x, route, mod_l]
    if final_g is None:
        out_specs = pl.BlockSpec((tm, d), lambda i, pos: (i, 0))
        out_shape = jax.ShapeDtypeStruct((n, d), F32)
    else:
        in_specs.append(pl.BlockSpec((1, d), lambda i, pos: (0, 0)))
        args.append(final_g)
        out_specs = [pl.BlockSpec((tm, d), lambda i, pos: (jnp.minimum(i, nct - 1), 0)),
                     pl.BlockSpec((tm, d), lambda i, pos: (jnp.clip(i - nct, 0, nlt - 1), 0))]
        out_shape = [jax.ShapeDtypeStruct((n_ctx, d), F32), jax.ShapeDtypeStruct((n - n_ctx, d), F32)]
    return pl.pallas_call(
        functools.partial(_combine_kernel, n=n, tm=tm, final_norm=final_g is not None, nct=nct),
        grid_spec=pltpu.PrefetchScalarGridSpec(
            num_scalar_prefetch=1,
            grid=(n // tm,),
            in_specs=in_specs,
            out_specs=out_specs,
            scratch_shapes=[pltpu.VMEM((2, tm, d), F32), pltpu.SemaphoreType.DMA(())],
        ),
        out_shape=out_shape,
        compiler_params=_cparams(1),
        name="moe_combine",
    )(pos, *args)


def _routing_tables(route, counts, n_experts, tm):
    n = route.shape[0]
    n_tiles = 2 * n // tm + n_experts
    cnt = counts[0, :n_experts].astype(jnp.int32)
    tiles_e = (cnt + tm - 1) // tm
    tile_end = jnp.cumsum(tiles_e)
    offs = (tile_end - tiles_e) * tm
    i1 = route[:, ROUTE_I1].astype(jnp.int32)
    i2 = route[:, ROUTE_I2].astype(jnp.int32)
    eidx = jnp.arange(n_experts, dtype=jnp.int32)[None, :]
    off1 = jnp.sum(jnp.where(i1[:, None] == eidx, offs[None, :], 0), axis=1)
    off2 = jnp.sum(jnp.where(i2[:, None] == eidx, offs[None, :], 0), axis=1)
    pos = jnp.concatenate([off1 + route[:, ROUTE_R1].astype(jnp.int32), off2 + route[:, ROUTE_R2].astype(jnp.int32)])
    n_active = tile_end[-1]
    t = jnp.arange(n_tiles, dtype=jnp.int32)
    t_eff = jnp.minimum(t, n_active - 1)
    tile_exp = jnp.sum((t_eff[:, None] >= tile_end[None, :]).astype(jnp.int32), axis=1)
    return pos, tile_exp.astype(jnp.int32), t_eff.astype(jnp.int32), n_active.reshape(1).astype(jnp.int32), n_tiles


def _ffn_kernel(*refs, final_norm):
    refs = list(refs)
    h_ref, x_ref, wg_ref, wu_ref, wd_ref, mod_ref = refs[:6]
    refs = refs[6:]
    fg_ref = refs.pop(0) if final_norm else None
    (o_ref,) = refs
    f = pl.program_id(1)

    @pl.when(f == 0)
    def _():
        o_ref[...] = jnp.zeros(o_ref.shape, F32)

    h = h_ref[...]
    t = _silu(_dot(h, wg_ref[...])) * _dot(h, wu_ref[...])
    o_ref[...] += _dot(t.astype(BF16), wd_ref[...])

    @pl.when(f == pl.num_programs(1) - 1)
    def _():
        y = x_ref[...] + mod_ref[0, 5:6, :] * o_ref[...]
        if final_norm:
            y = _rmsnorm(y, fg_ref[...])
        o_ref[...] = y


def _ffn(h, x, wg, wu, wd, mod_l, final_g, *, n_ctx, t_lat, tm=512, tf=512):
    n, d = x.shape
    ff = wg.shape[1]
    nct = n_ctx // tm

    def cond_idx(i):
        return jnp.where(i < nct, 0, 1 + ((i - nct) * tm) // t_lat)

    row = lambda i, f: (i, 0)
    in_specs = [
        pl.BlockSpec((tm, d), row),
        pl.BlockSpec((tm, d), row),
        pl.BlockSpec((d, tf), lambda i, f: (0, f)),
        pl.BlockSpec((d, tf), lambda i, f: (0, f)),
        pl.BlockSpec((tf, d), lambda i, f: (f, 0)),
        pl.BlockSpec((1, 6, d), lambda i, f: (cond_idx(i), 0, 0)),
    ]
    args = [h, x, wg, wu, wd, mod_l]
    if final_g is not None:
        in_specs.append(pl.BlockSpec((1, d), lambda i, f: (0, 0)))
        args.append(final_g)
    return pl.pallas_call(
        functools.partial(_ffn_kernel, final_norm=final_g is not None),
        grid=(n // tm, ff // tf),
        in_specs=in_specs,
        out_specs=pl.BlockSpec((tm, d), row),
        out_shape=jax.ShapeDtypeStruct((n, d), F32),
        compiler_params=_cparams(2),
        name="ffn_dense",
    )(*args)


def kernel(x_prompt, x_sample, cache_nat_k, cache_nat_v, state_gla, cache_swa_k, cache_swa_v, c, c_ctx, w_ada, b_ada,
           norm_mix_g, norm_ffn_g, w_in, na_rpb, gla_w_gate, gla_b_gate, gla_norm_g, swa_sink, w_out, ffn_w_gate,
           ffn_w_up, ffn_w_down, moe_w_router, moe_b_router, moe_w_gate, moe_w_up, moe_w_down, final_norm_g):
    nb, seq, d = x_prompt.shape
    db, t_lat, _ = x_sample.shape
    depth = w_in.shape[0]
    p_len, h_a = cache_nat_k.shape[2], cache_nat_k.shape[3]
    h_b, dk_b, dv_b = state_gla.shape[3:6]
    hkv_c, g_c = swa_sink.shape[1], swa_sink.shape[2]
    rank = gla_w_gate.shape[2]
    n_exp = moe_w_router.shape[-1]
    w_a = h_a * HEAD_DIM
    w_bk = h_b * dk_b
    w_bv = h_b * dv_b
    w_cq = hkv_c * g_c * HEAD_DIM
    w_ck = hkv_c * HEAD_DIM
    n_ctx = nb * seq
    n_lat = db * t_lat
    n = n_ctx + n_lat
    z0 = 3 * w_a + 2 * w_bk + 2 * w_bv
    z1 = z0 + 2 * rank
    a_w = 3 * w_a
    b_w = 2 * w_bk + 2 * w_bv
    assert a_w == b_w and (a_w + b_w) % w_cq == 0 and (a_w + b_w + w_cq) % w_ck == 0 and 2 * rank <= LANES
    assert db + 1 <= 8

    x, x_lat = x_prompt.reshape(n_ctx, d), x_sample.reshape(n_lat, d)
    cond8 = jnp.concatenate([c_ctx[None, :], c, jnp.zeros((7 - db, d), F32)], axis=0)
    mod = _adaln(cond8, w_ada, b_ada).reshape(depth, 8, 6, d)

    cos, sin = _rope_tables(t_lat, w_ck)
    na_tab = _na_bias_tables(na_rpb)
    kc_a = cache_nat_k.reshape(db, depth, p_len, w_a)
    vc_a = cache_nat_v.reshape(db, depth, p_len, w_a)
    kc_c = cache_swa_k.reshape(db, depth, p_len, w_ck)
    vc_c = cache_swa_v.reshape(db, depth, p_len, w_ck)
    sizes = dict(n_ctx=n_ctx, t_lat=t_lat)

    kv_bufs = [jnp.zeros((nb, depth, seq, w), F32) for w in (w_a, w_a, w_ck, w_ck)]
    gla_st = jnp.zeros((nb, depth, 2, h_b, dk_b, dv_b), F32)
    for l in range(depth):
        w_l = w_in[l]
        w_main = jnp.concatenate([w_l[:, :z0], w_l[:, z1:]], axis=1).astype(BF16)
        wz = jnp.pad(w_l[:, z0:z1], ((0, 0), (0, LANES - 2 * rank))).astype(BF16)
        u, z, *kv_bufs = _inproj(x, x_lat, mod[l], norm_mix_g[l][None, :], w_main, wz, kv_bufs, l, seq=seq, w_a=w_a,
                                 w_cq=w_cq, w_kvc=w_ck, **sizes)

        oa_c = _ctx_attn(u, None, nb=nb, seq=seq, qcol=0, kcol=1, vcol=2, wq=w_a, wk=w_a, hkv=h_a, g=1)
        oa_l = _na_attn(u, kc_a, vc_a, na_tab, l, n_ctx=n_ctx, db=db, t_lat=t_lat, w_a=w_a, heads=h_a)
        gw = jnp.zeros((2, LANES, w_bk), F32)
        gw = gw.at[0, 0:rank].set(gla_w_gate[l, 0]).at[1, rank:2 * rank].set(gla_w_gate[l, 1])
        gb = gla_b_gate[l][:, None, :]
        gn = gla_norm_g[l].reshape(1, w_bv)
        gla_args = dict(bcol=a_w // b_w, heads=h_b, dk=dk_b, dv=dv_b)
        ob_c, gla_st = _gla(u, z, gw, gb, gn, None, gla_st, l, row0=0, nb=nb, seq_len=seq, **gla_args)
        (ob_l,) = _gla(u, z, gw, gb, gn, state_gla, None, l, row0=n_ctx, nb=db, seq_len=t_lat, **gla_args)
        c0 = a_w + b_w
        cols = dict(qcol=c0 // w_cq, kcol=(c0 + w_cq) // w_ck, vcol=(c0 + w_cq) // w_ck + 1, hkv=hkv_c, g=g_c)
        oc_c = _ctx_attn(u, swa_sink[l], nb=nb, seq=seq, wq=w_cq, wk=w_ck, **cols)
        oc_l = _swa_attn(u, swa_sink[l], kc_c, vc_c, cos, sin, l, n_ctx=n_ctx, db=db, t_lat=t_lat, **cols)

        moe = l % 2 == 1
        i = l // 2
        router = None
        if moe:
            wr = jnp.pad(moe_w_router[i], ((0, 0), (0, LANES - n_exp)))
            br = jnp.concatenate([moe_b_router[i], jnp.full((LANES - n_exp,), NEG_INF, F32)])[None, :]
            router = (wr, br, n_exp)
        res = _outproj(x, x_lat, (oa_c, ob_c, oc_c), (oa_l, ob_l, oc_l), w_out[l].astype(BF16), mod[l],
                       norm_ffn_g[l][None, :], router, **sizes)
        x_lat = None
        fg = final_norm_g[None, :] if l == depth - 1 else None
        if moe:
            xn, hp, route, counts = res
            tm_e = 512
            pos, tile_exp, tile_row, n_active, n_tiles = _routing_tables(route, counts, n_exp, tm_e)
            hs = _dispatch(pos, hp, n_tiles * tm_e)
            ys = _moe_ffn(tile_exp, tile_row, n_active, hs, moe_w_gate[i].astype(BF16), moe_w_up[i].astype(BF16),
                          moe_w_down[i].astype(BF16), tm=tm_e, tf=moe_w_gate.shape[-1] // 2)
            x = _combine(pos, ys, xn, route, mod[l], fg, **sizes)
        else:
            xn, h2 = res
            x = _ffn(h2, xn, ffn_w_gate[i].astype(BF16), ffn_w_up[i].astype(BF16), ffn_w_down[i].astype(BF16),
                     mod[l], fg, **sizes)

    y_ctx, y_lat = x if isinstance(x, (list, tuple)) else (x[:n_ctx], x[n_ctx:])
    nat_k, nat_v, swa_k, swa_v = kv_bufs
    return (y_ctx.reshape(nb, seq, d), y_lat.reshape(db, t_lat, d),
            nat_k.reshape(nb, depth, seq, h_a, HEAD_DIM), nat_v.reshape(nb, depth, seq, h_a, HEAD_DIM), gla_st,
            swa_k.reshape(nb, depth, seq, hkv_c, HEAD_DIM), swa_v.reshape(nb, depth, seq, hkv_c, HEAD_DIM))
```

```python
import functools

import numpy as np
import jax
import jax.numpy as jnp
from jax import lax
from jax.experimental import pallas as pl
from jax.experimental.pallas import tpu as pltpu

F32 = jnp.float32
BF16 = jnp.bfloat16
HIGHEST = lax.Precision.HIGHEST

EPS = 1e-6
NEG_INF = -1e30
GRID_W = 64
HEAD_DIM = 64
NA_ROWS = 8
NA_COLS = 16
SWA_WIN = 128
SWA_QBLOCK = 128
GLA_CHUNK = 64
GLA_GROUP = 4
GLA_TAU = 16.0
ROPE_BASE = 10000.0
LANES = 128
FFN_TF = 512
VMEM_LIMIT = 56 * 1024 * 1024


def _cparams(n_axes):
    return pltpu.CompilerParams(dimension_semantics=("arbitrary",) * n_axes, vmem_limit_bytes=VMEM_LIMIT)


def _dot(a, b, **kw):
    return jnp.dot(a, b, preferred_element_type=F32, **kw)


def _dot_nt(a, b, **kw):
    return lax.dot_general(a, b, (((1,), (1,)), ((), ())), preferred_element_type=F32, **kw)


def _dot_tn(a, b, **kw):
    return lax.dot_general(a, b, (((0,), (0,)), ((), ())), preferred_element_type=F32, **kw)


def _silu(x):
    return x / (1.0 + jnp.exp(-x))


def _split_bf16(x):
    hi = x.astype(BF16)
    return hi, (x - hi.astype(F32)).astype(BF16)


def _tile_columns(w, tf):
    *lead, d, f = w.shape
    return jnp.moveaxis(w.reshape(*lead, d, f // tf, tf), -2, -3).astype(BF16)


def _rmsnorm(x, g):
    return x * lax.rsqrt(jnp.mean(x * x, axis=-1, keepdims=True) + EPS) * g


def _adaln_kernel(cond_ref, w_ref, b_ref, o_ref):
    s = _silu(cond_ref[...]).astype(BF16)
    o_ref[0] = _dot(s, w_ref[0].astype(BF16)) + b_ref[0]


def _adaln(cond8, w_ada, b_ada, tn=1024):
    depth, d, n6 = w_ada.shape
    return pl.pallas_call(
        _adaln_kernel,
        grid=(depth, n6 // tn),
        in_specs=[
            pl.BlockSpec((8, d), lambda l, j: (0, 0)),
            pl.BlockSpec((1, d, tn), lambda l, j: (l, 0, j)),
            pl.BlockSpec((1, 1, tn), lambda l, j: (l, 0, j)),
        ],
        out_specs=pl.BlockSpec((1, 8, tn), lambda l, j: (l, 0, j)),
        out_shape=jax.ShapeDtypeStruct((depth, 8, n6), F32),
        compiler_params=_cparams(2),
        name="adaln",
    )(cond8, w_ada, b_ada.reshape(depth, 1, n6))


def _inproj_kernel(*refs, nct, seq, w_a, w_cq, w_kvc, split_x):
    refs = list(refs)
    x_ref = refs.pop(0)
    xl_ref = refs.pop(0) if split_x else None
    mod_ref, g_ref, w_ref, wz_ref = refs[:4]
    u_ref, z_ref, nk_ref, nv_ref, sk_ref, sv_ref, h_scr = refs[8:]
    i = pl.program_id(0)
    j = pl.program_id(1)
    is_ctx = i < nct

    @pl.when(j == 0)
    def _():
        x = jnp.where(is_ctx, x_ref[...], xl_ref[...]) if split_x else x_ref[...]
        h = _rmsnorm(x, g_ref[...]) * (1.0 + mod_ref[0, 1:2, :]) + mod_ref[0, 0:1, :]
        hb = h.astype(BF16)
        h_scr[...] = hb
        z_ref[...] = _dot(hb, wz_ref[...])

    acc = _dot(h_scr[...], w_ref[...])
    u_ref[...] = acc.astype(BF16)
    nseq = nk_ref.shape[0]

    @pl.when(is_ctx & (j == 0))
    def _():
        for bb in range(nseq):
            nk_ref[bb, 0] = acc[bb * seq:(bb + 1) * seq, w_a:2 * w_a]
            nv_ref[bb, 0] = acc[bb * seq:(bb + 1) * seq, 2 * w_a:3 * w_a]

    @pl.when(is_ctx & (j == 2))
    def _():
        for bb in range(nseq):
            sk_ref[bb, 0] = acc[bb * seq:(bb + 1) * seq, w_cq:w_cq + w_kvc]
            sv_ref[bb, 0] = acc[bb * seq:(bb + 1) * seq, w_cq + w_kvc:w_cq + 2 * w_kvc]


def _inproj(x, x_lat, mod_l, g, w_main, wz, kv_bufs, l, *, n_ctx, seq, t_lat, w_a, w_cq, w_kvc, tm=512):
    split_x = x_lat is not None
    d = x.shape[1]
    n = x.shape[0] + (x_lat.shape[0] if split_x else 0)
    n_main = w_main.shape[2]
    tn = 3 * w_a
    nct = n_ctx // tm
    nlt = n // tm - nct
    nseq = tm // seq
    assert n_ctx % tm == 0 and tm % seq == 0 and t_lat % tm == 0 and n_main == 3 * tn and w_cq + 2 * w_kvc == tn

    def cond_idx(i):
        return jnp.where(i < nct, 0, 1 + ((i - nct) * tm) // t_lat)

    kv_idx = lambda i, j: (jnp.minimum(i, nct - 1), l, 0, 0)
    if split_x:
        x_specs = [pl.BlockSpec((tm, d), lambda i, j: (jnp.minimum(i, nct - 1), 0)),
                   pl.BlockSpec((tm, d), lambda i, j: (jnp.clip(i - nct, 0, nlt - 1), 0))]
        x_args = [x, x_lat]
    else:
        x_specs = [pl.BlockSpec((tm, d), lambda i, j: (i, 0))]
        x_args = [x]
    n_in = len(x_args) + 4
    kern = functools.partial(_inproj_kernel, nct=nct, seq=seq, w_a=w_a, w_cq=w_cq, w_kvc=w_kvc, split_x=split_x)
    return pl.pallas_call(
        kern,
        grid=(n // tm, n_main // tn),
        in_specs=x_specs + [
            pl.BlockSpec((1, 6, d), lambda i, j: (cond_idx(i), 0, 0)),
            pl.BlockSpec((1, d), lambda i, j: (0, 0)),
            pl.BlockSpec((None, d, tn), lambda i, j: (l, 0, j)),
            pl.BlockSpec((None, d, LANES), lambda i, j: (l, 0, 0)),
        ] + [pl.BlockSpec(memory_space=pl.ANY)] * 4,
        out_specs=[
            pl.BlockSpec((tm, tn), lambda i, j: (i, j)),
            pl.BlockSpec((tm, LANES), lambda i, j: (i, 0)),
        ] + [pl.BlockSpec((nseq, 1, seq, b.shape[-1]), kv_idx) for b in kv_bufs],
        out_shape=[
            jax.ShapeDtypeStruct((n, n_main), BF16),
            jax.ShapeDtypeStruct((n, LANES), F32),
        ] + [jax.ShapeDtypeStruct(b.shape, b.dtype) for b in kv_bufs],
        scratch_shapes=[pltpu.VMEM((tm, d), BF16)],
        input_output_aliases={n_in + k: 2 + k for k in range(4)},
        compiler_params=_cparams(2),
        name="inproj",
    )(*x_args, mod_l, g, w_main, wz, *kv_bufs)


def _ctx_attn_kernel(*refs, hkv, g, has_sink):
    if has_sink:
        sink_ref, q_ref, k_ref, v_ref, o_ref = refs
    else:
        q_ref, k_ref, v_ref, o_ref = refs
    scale = HEAD_DIM ** -0.5
    q = q_ref[...]
    k = k_ref[...]
    v = v_ref[...]
    outs = []
    for kh in range(hkv):
        k_h = k[:, kh * HEAD_DIM:(kh + 1) * HEAD_DIM]
        v_h = v[:, kh * HEAD_DIM:(kh + 1) * HEAD_DIM]
        for gi in range(g):
            hq = kh * g + gi
            q_h = q[:, hq * HEAD_DIM:(hq + 1) * HEAD_DIM]
            s = _dot_nt(q_h, k_h) * scale
            m = jnp.max(s, axis=-1, keepdims=True)
            if has_sink:
                sk = sink_ref[kh, gi]
                m = jnp.maximum(m, sk)
            e = jnp.exp(s - m)
            den = jnp.sum(e, axis=-1, keepdims=True)
            if has_sink:
                den = den + jnp.exp(sk - m)
            outs.append(_dot(e.astype(BF16), v_h) / den)
    o_ref[...] = jnp.concatenate(outs, axis=-1).astype(BF16)


def _ctx_attn(u, sink, *, nb, seq, qcol, kcol, vcol, wq, wk, hkv, g):
    has_sink = sink is not None
    kern = functools.partial(_ctx_attn_kernel, hkv=hkv, g=g, has_sink=has_sink)
    in_specs = [
        pl.BlockSpec((seq, wq), lambda b: (b, qcol)),
        pl.BlockSpec((seq, wk), lambda b: (b, kcol)),
        pl.BlockSpec((seq, wk), lambda b: (b, vcol)),
    ]
    args = [u, u, u]
    if has_sink:
        in_specs = [pl.BlockSpec(memory_space=pltpu.SMEM)] + in_specs
        args = [sink] + args
    return pl.pallas_call(
        kern,
        grid=(nb,),
        in_specs=in_specs,
        out_specs=pl.BlockSpec((seq, wq), lambda b: (b, 0)),
        out_shape=jax.ShapeDtypeStruct((nb * seq, wq), BF16),
        compiler_params=_cparams(1),
        name="ctx_attn_sink" if has_sink else "ctx_attn",
    )(*args)


NA_QROWS = 4
NA_WIN_ROWS = 12


def _na_kernel(q_ref, k_ref, v_ref, kc_ref, vc_ref, bias_ref, o_ref, *, heads, rows):
    scale = HEAD_DIM ** -0.5
    qi = pl.program_id(1)
    r0 = qi * NA_QROWS
    w0 = jnp.clip(r0 - NA_ROWS // 2, 0, rows - NA_WIN_ROWS)
    k0 = pl.multiple_of(w0 * GRID_W, NA_QROWS * GRID_W)
    tq = NA_QROWS * GRID_W
    tk = NA_WIN_ROWS * GRID_W
    q = q_ref[...]
    k = k_ref[pl.ds(k0, tk), :]
    v = v_ref[pl.ds(k0, tk), :]
    kc = kc_ref[0, 0].astype(BF16)
    vc = vc_ref[0, 0].astype(BF16)
    qrow = r0 + lax.broadcasted_iota(jnp.int32, (tq, tk), 0) // GRID_W
    krow = w0 + lax.broadcasted_iota(jnp.int32, (tq, tk), 1) // GRID_W
    band0 = jnp.clip(qrow - NA_ROWS // 2, 0, rows - NA_ROWS)
    in_band = (krow >= band0) & (krow < band0 + NA_ROWS)
    outs = []
    for h in range(heads):
        sl = slice(h * HEAD_DIM, (h + 1) * HEAD_DIM)
        q_h = q[:, sl]
        bias = jnp.concatenate(
            [jnp.concatenate([bias_ref[h, w0 + 2 * p - (r0 + rq) + 2 * NA_ROWS - 1] for p in range(NA_WIN_ROWS // 2)],
                             axis=1) for rq in range(NA_QROWS)], axis=0)
        s_w = jnp.where(in_band, _dot_nt(q_h, k[:, sl]) * scale + bias, NEG_INF)
        s_c = _dot_nt(q_h, kc[:, sl]) * scale
        m = jnp.maximum(jnp.max(s_w, axis=-1, keepdims=True), jnp.max(s_c, axis=-1, keepdims=True))
        e_w = jnp.exp(s_w - m)
        e_c = jnp.exp(s_c - m)
        den = jnp.sum(e_w, axis=-1, keepdims=True) + jnp.sum(e_c, axis=-1, keepdims=True)
        o = _dot(e_w.astype(BF16), v[:, sl]) + _dot(e_c.astype(BF16), vc[:, sl])
        outs.append(o / den)
    o_ref[...] = jnp.concatenate(outs, axis=-1).astype(BF16)


def _na_attn(u, kc, vc, bias, l, *, n_ctx, db, t_lat, w_a, heads):
    tq = NA_QROWS * GRID_W
    rows = t_lat // GRID_W
    nq = t_lat // tq
    p = kc.shape[2]
    assert rows >= NA_WIN_ROWS and NA_WIN_ROWS >= NA_QROWS + NA_ROWS - 1 and (rows - NA_WIN_ROWS) % NA_QROWS == 0
    kern = functools.partial(_na_kernel, heads=heads, rows=rows)
    return pl.pallas_call(
        kern,
        grid=(db, nq),
        in_specs=[
            pl.BlockSpec((tq, w_a), lambda b, i: (n_ctx // tq + b * nq + i, 0)),
            pl.BlockSpec((t_lat, w_a), lambda b, i: (n_ctx // t_lat + b, 1)),
            pl.BlockSpec((t_lat, w_a), lambda b, i: (n_ctx // t_lat + b, 2)),
            pl.BlockSpec((1, 1, p, w_a), lambda b, i: (b, l, 0, 0)),
            pl.BlockSpec((1, 1, p, w_a), lambda b, i: (b, l, 0, 0)),
            pl.BlockSpec((None,) + bias.shape[1:], lambda b, i: (l, 0, 0, 0, 0)),
        ],
        out_specs=pl.BlockSpec((tq, w_a), lambda b, i: (b * nq + i, 0)),
        out_shape=jax.ShapeDtypeStruct((db * t_lat, w_a), BF16),
        compiler_params=_cparams(2),
        name="na_attn",
    )(u, u, u, kc, vc, bias)


def _na_bias_tables(rpb):
    col = np.arange(GRID_W)
    col_start = np.clip(col - NA_COLS // 2, 0, GRID_W - NA_COLS)
    col_ok = (col[None, :] >= col_start[:, None]) & (col[None, :] < col_start[:, None] + NA_COLS)
    dc = np.clip(col[None, :] - col[:, None] + NA_COLS - 1, 0, 2 * NA_COLS - 2)
    pick = (np.arange(2 * NA_COLS - 1)[:, None, None] == dc[None]).astype(np.float32)
    t1 = jnp.einsum("lhdj,jck->lhdck", rpb.astype(F32), jnp.asarray(pick), precision=HIGHEST)
    t1 = jnp.where(jnp.asarray(col_ok), t1, NEG_INF)
    t1 = jnp.pad(t1, ((0, 0), (0, 0), (NA_ROWS, NA_ROWS), (0, 0), (0, 0)), constant_values=NEG_INF)
    return jnp.concatenate([t1[:, :, :-1], t1[:, :, 1:]], axis=-1)


def _rope(x, cos, sin_signed):
    w = x.shape[-1]
    q4 = HEAD_DIM // 4
    lane = lax.broadcasted_iota(jnp.int32, x.shape, 1)
    first = (lane % (2 * q4)) < q4
    swapped = jnp.where(first, pltpu.roll(x, w - q4, 1), pltpu.roll(x, q4, 1))
    return x * cos + swapped * sin_signed


def _swa_kernel(sink_ref, q_ref, k_ref, v_ref, kc_ref, vc_ref, cos_ref, sin_ref, o_ref, k_scr, *, hkv, g, t_lat):
    n = pl.program_id(1)
    scale = HEAD_DIM ** -0.5
    qb = SWA_QBLOCK
    wk = hkv * HEAD_DIM
    span = 3 * qb

    @pl.when(n == 0)
    def _():
        k_scr[...] = _rope(k_ref[...].astype(F32), cos_ref[...], sin_ref[...]).astype(BF16)

    q0 = pl.multiple_of(n * qb, qb)
    start = pl.multiple_of(jnp.clip(n * qb - qb, 0, t_lat - span), qb)
    cos_q = cos_ref[pl.ds(q0, qb), :]
    sin_q = sin_ref[pl.ds(q0, qb), :]
    k_loc = k_scr[pl.ds(start, span), :]
    v_loc = v_ref[pl.ds(start, span), :]
    kc = kc_ref[0, 0].astype(BF16)
    vc = vc_ref[0, 0].astype(BF16)
    qpos = q0 + lax.broadcasted_iota(jnp.int32, (qb, span), 0)
    kpos = start + lax.broadcasted_iota(jnp.int32, (qb, span), 1)
    valid1 = jnp.abs(qpos - kpos) <= SWA_WIN
    valid = jnp.concatenate([valid1] * g, axis=0)
    outs = []
    for kh in range(hkv):
        q_r = _rope(q_ref[:, kh * wk:(kh + 1) * wk].astype(F32), cos_q, sin_q).astype(BF16)
        q_st = jnp.concatenate([q_r[:, gi * HEAD_DIM:(gi + 1) * HEAD_DIM] for gi in range(g)], axis=0)
        sk = jnp.concatenate([jnp.full((qb, 1), sink_ref[kh, gi], F32) for gi in range(g)], axis=0)
        ksl = slice(kh * HEAD_DIM, (kh + 1) * HEAD_DIM)
        s_l = jnp.where(valid, _dot_nt(q_st, k_loc[:, ksl]) * scale, NEG_INF)
        s_c = _dot_nt(q_st, kc[:, ksl]) * scale
        m = jnp.maximum(jnp.maximum(jnp.max(s_l, axis=-1, keepdims=True), jnp.max(s_c, axis=-1, keepdims=True)), sk)
        e_l = jnp.exp(s_l - m)
        e_c = jnp.exp(s_c - m)
        den = jnp.sum(e_l, axis=-1, keepdims=True) + jnp.sum(e_c, axis=-1, keepdims=True) + jnp.exp(sk - m)
        o = (_dot(e_l.astype(BF16), v_loc[:, ksl]) + _dot(e_c.astype(BF16), vc[:, ksl])) / den
        outs.extend(o[gi * qb:(gi + 1) * qb] for gi in range(g))
    o_ref[...] = jnp.concatenate(outs, axis=-1).astype(BF16)


def _swa_attn(u, sink, kc, vc, cos, sin, l, *, n_ctx, db, t_lat, qcol, kcol, vcol, hkv, g):
    qb = SWA_QBLOCK
    nq = t_lat // qb
    wq = hkv * g * HEAD_DIM
    wk = hkv * HEAD_DIM
    assert g * HEAD_DIM == wk, "rotary tables are shared between the q groups and k"
    p = kc.shape[2]
    kern = functools.partial(_swa_kernel, hkv=hkv, g=g, t_lat=t_lat)
    return pl.pallas_call(
        kern,
        grid=(db, nq),
        in_specs=[
            pl.BlockSpec(memory_space=pltpu.SMEM),
            pl.BlockSpec((qb, wq), lambda b, i: (n_ctx // qb + b * nq + i, qcol)),
            pl.BlockSpec((t_lat, wk), lambda b, i: (n_ctx // t_lat + b, kcol)),
            pl.BlockSpec((t_lat, wk), lambda b, i: (n_ctx // t_lat + b, vcol)),
            pl.BlockSpec((1, 1, p, wk), lambda b, i: (b, l, 0, 0)),
            pl.BlockSpec((1, 1, p, wk), lambda b, i: (b, l, 0, 0)),
            pl.BlockSpec((t_lat, wk), lambda b, i: (0, 0)),
            pl.BlockSpec((t_lat, wk), lambda b, i: (0, 0)),
        ],
        out_specs=pl.BlockSpec((qb, wq), lambda b, i: (b * nq + i, 0)),
        out_shape=jax.ShapeDtypeStruct((db * t_lat, wq), BF16),
        scratch_shapes=[pltpu.VMEM((t_lat, wk), BF16)],
        compiler_params=_cparams(2),
        name="swa_attn",
    )(sink, u, u, u, kc, vc, cos, sin)


def _rope_tables(t_lat, width):
    quarter = HEAD_DIM // 4
    pos = np.arange(t_lat)
    freqs = ROPE_BASE ** (-np.arange(quarter, dtype=np.float32) / quarter)
    ang_r = (pos // GRID_W).astype(np.float32)[:, None] * freqs[None, :]
    ang_c = (pos % GRID_W).astype(np.float32)[:, None] * freqs[None, :]
    ang_r, ang_c = jnp.asarray(ang_r, F32), jnp.asarray(ang_c, F32)
    cos = jnp.concatenate([jnp.cos(ang_r), jnp.cos(ang_r), jnp.cos(ang_c), jnp.cos(ang_c)], axis=-1)
    sin = jnp.concatenate([-jnp.sin(ang_r), jnp.sin(ang_r), -jnp.sin(ang_c), jnp.sin(ang_c)], axis=-1)
    reps = width // HEAD_DIM
    return jnp.tile(cos, (1, reps)), jnp.tile(sin, (1, reps))


def _log_sigmoid(x):
    return jnp.minimum(x, 0.0) - jnp.log(1.0 + jnp.exp(-jnp.abs(x)))


def _gla_kernel(*refs, heads, dk, dv, seq_len, has_s0, out_state):
    refs = list(refs)
    ub_ref, z_ref, gw_ref, gb_ref, gn_ref = refs[:5]
    refs = refs[5:]
    s0_ref = refs.pop(0) if has_s0 else None
    if out_state:
        refs.pop(0)
    o_ref = refs.pop(0)
    st_ref = refs.pop(0) if out_state else None
    la_scr, o_scr, st_scr = refs
    c_len = GLA_CHUNK
    nc = seq_len // c_len
    wk = heads * dk
    wv = heads * dv

    z_hi, z_lo = _split_bf16(z_ref[...])
    for d in range(2):
        g_hi, g_lo = _split_bf16(gw_ref[d])
        pre = _dot(z_hi, g_hi) + (_dot(z_hi, g_lo) + _dot(z_lo, g_hi)) + gb_ref[d]
        la_scr[d] = _log_sigmoid(pre) / GLA_TAU

    eye_v = (lax.broadcasted_iota(jnp.int32, (dv, dv), 0) == lax.broadcasted_iota(jnp.int32, (dv, dv), 1)).astype(F32)
    for d in range(2):
        for h in range(heads):
            if has_s0:
                s0_rows = jnp.concatenate(
                    [s0_ref[0, 0, d, h] if hh == h else jnp.zeros((dk, dv), F32) for hh in range(heads)], axis=0)
                st_scr[d, h] = _dot_nt(eye_v, s0_rows, precision=HIGHEST)
            else:
                st_scr[d, h] = jnp.zeros((dv, wk), F32)

    ng = GLA_GROUP
    gr = ng * c_len
    ti = lax.broadcasted_iota(jnp.int32, (gr, gr), 0)
    si = lax.broadcasted_iota(jnp.int32, (gr, gr), 1)
    same_chunk = (ti // c_len) == (si // c_len)
    lane_head = lax.broadcasted_iota(jnp.int32, (gr, wk), 1) // dk
    row_chunk = lax.broadcasted_iota(jnp.int32, (gr, dv), 0) // c_len

    def group(gi, d):
        reverse = d == 1
        keep = same_chunk & ((si >= ti) if reverse else (si <= ti))
        sums = jnp.concatenate([keep.astype(BF16), same_chunk.astype(BF16)], axis=0)
        rows = pl.ds(pl.multiple_of(gi * gr, gr), gr)
        la_hi, la_lo = _split_bf16(la_scr[d, rows, :])
        res = _dot(sums, la_hi) + _dot(sums, la_lo)
        b, b_end = res[:gr], res[gr:]
        q = ub_ref[rows, 0:wk].astype(F32) * (dk ** -0.5)
        k = ub_ref[rows, wk:2 * wk].astype(F32)
        qt = q * jnp.exp(b)
        kt = (k * jnp.exp(-b)).astype(BF16)
        ke = (k * jnp.exp(b_end - b)).astype(BF16)
        dec = jnp.exp(b_end)
        order = range(ng - 1, -1, -1) if reverse else range(ng)
        for h in range(heads):
            vs = slice(h * dv, (h + 1) * dv)
            qm = jnp.where(lane_head == h, qt, 0.0).astype(BF16)
            a = jnp.where(keep, _dot_nt(qm, kt), 0.0).astype(BF16)
            v_h = ub_ref[rows, 2 * wk + h * dv:2 * wk + (h + 1) * dv]
            o = _dot(a, v_h)
            v_f = v_h.astype(F32)
            v_blk = jnp.concatenate([jnp.where(row_chunk == c, v_f, 0.0).astype(BF16) for c in range(ng)], axis=1)
            ut = _dot_tn(v_blk, ke)
            st = st_scr[d, h]
            inter = [None] * ng
            for c in order:
                inter[c] = _dot_nt(qm[c * c_len:(c + 1) * c_len], st.astype(BF16))
                st = st * dec[c * c_len:c * c_len + 1, :] + ut[c * dv:(c + 1) * dv]
            st_scr[d, h] = st
            o = o + jnp.concatenate(inter, axis=0)
            if reverse:
                o_scr[rows, vs] += o
            else:
                o_scr[rows, vs] = o

    n_groups = seq_len // gr
    lax.fori_loop(0, n_groups, lambda g, carry: (group(g, 0), carry)[1], 0)
    lax.fori_loop(0, n_groups, lambda g, carry: (group(n_groups - 1 - g, 1), carry)[1], 0)

    r = ub_ref[:, 2 * wk + wv:2 * wk + 2 * wv].astype(F32)
    gn = gn_ref[...]
    for h in range(heads):
        vs = slice(h * dv, (h + 1) * dv)
        o_ref[:, vs] = (_rmsnorm(o_scr[:, vs], gn[:, vs]) * _silu(r[:, vs])).astype(BF16)
    if out_state:
        for d in range(2):
            for h in range(heads):
                st_ref[0, 0, d, h] = st_scr[d, h].T[h * dk:(h + 1) * dk, :]


def _gla(u, z, gw, gb, gn, s0, st_buf, l, *, row0, nb, seq_len, bcol, heads, dk, dv):
    wk, wv = heads * dk, heads * dv
    wb = 2 * wk + 2 * wv
    has_s0 = s0 is not None
    out_state = st_buf is not None
    rb = row0 // seq_len
    kern = functools.partial(_gla_kernel, heads=heads, dk=dk, dv=dv, seq_len=seq_len, has_s0=has_s0,
                             out_state=out_state)
    in_specs = [
        pl.BlockSpec((seq_len, wb), lambda b: (rb + b, bcol)),
        pl.BlockSpec((seq_len, LANES), lambda b: (rb + b, 0)),
        pl.BlockSpec((2, LANES, wk), lambda b: (0, 0, 0)),
        pl.BlockSpec((2, 1, wk), lambda b: (0, 0, 0)),
        pl.BlockSpec((1, wv), lambda b: (0, 0)),
    ]
    args = [u, z, gw, gb, gn]
    if has_s0:
        in_specs.append(pl.BlockSpec((1, 1, 2, heads, dk, dv), lambda b: (b, l, 0, 0, 0, 0)))
        args.append(s0)
    out_specs = [pl.BlockSpec((seq_len, wv), lambda b: (b, 0))]
    out_shape = [jax.ShapeDtypeStruct((nb * seq_len, wv), BF16)]
    aliases = {}
    if out_state:
        aliases = {len(args): 1}
        in_specs.append(pl.BlockSpec(memory_space=pl.ANY))
        args.append(st_buf)
        out_specs.append(pl.BlockSpec((1, 1, 2, heads, dk, dv), lambda b: (b, l, 0, 0, 0, 0)))
        out_shape.append(jax.ShapeDtypeStruct(st_buf.shape, st_buf.dtype))
    res = pl.pallas_call(
        kern,
        grid=(nb,),
        in_specs=in_specs,
        out_specs=out_specs,
        out_shape=out_shape,
        scratch_shapes=[
            pltpu.VMEM((2, seq_len, wk), F32),
            pltpu.VMEM((seq_len, wv), F32),
            pltpu.VMEM((2, heads, dv, wk), F32),
        ],
        input_output_aliases=aliases,
        compiler_params=_cparams(1),
        name="gla_lat" if has_s0 else "gla_ctx",
    )(*args)
    return res


ROUTE_I1, ROUTE_I2, ROUTE_W1, ROUTE_W2, ROUTE_R1, ROUTE_R2 = range(6)


def _pack_bf16_pairs(h):
    c = h.shape[1] // 2
    lo = lax.bitcast_convert_type(h[:, :c].astype(BF16).astype(F32), jnp.uint32)
    hi = lax.bitcast_convert_type(h[:, c:].astype(BF16).astype(F32), jnp.uint32)
    return hi | (lo >> 16)


def _unpack_bf16_pairs(w):
    lo = lax.bitcast_convert_type(w << 16, F32).astype(BF16)
    hi = lax.bitcast_convert_type(w & jnp.uint32(0xFFFF0000), F32).astype(BF16)
    return jnp.concatenate([lo, hi], axis=-1)


def _outproj_kernel(*refs, n_experts, top_k, nct, split_x):
    refs = list(refs)
    x_ref = refs.pop(0)
    xl_ref = refs.pop(0) if split_x else None
    ctx_refs, lat_refs, refs = refs[0:3], refs[3:6], refs[6:]
    if n_experts:
        w_ref, mod_ref, g_ref, wr_ref, br_ref, xn_ref, h_ref, route_ref, counts_ref, run_scr = refs
    else:
        w_ref, mod_ref, g_ref, xn_ref, h_ref = refs
    is_ctx = pl.program_id(0) < nct
    mixed = jnp.concatenate([jnp.where(is_ctx, c[...], t[...]) for c, t in zip(ctx_refs, lat_refs)], axis=-1)
    x = jnp.where(is_ctx, x_ref[...], xl_ref[...]) if split_x else x_ref[...]
    xn = x + mod_ref[0, 2:3, :] * _dot(mixed, w_ref[...])
    xn_ref[...] = xn
    h = _rmsnorm(xn, g_ref[...]) * (1.0 + mod_ref[0, 4:5, :]) + mod_ref[0, 3:4, :]
    if not n_experts:
        h_ref[...] = h.astype(BF16)
        return
    assert top_k == 2
    h_ref[...] = _pack_bf16_pairs(h)
    h_hi = h.astype(BF16)
    h_lo = (h - h_hi.astype(F32)).astype(BF16)
    wr = wr_ref[...]
    w_hi = wr.astype(BF16)
    w_lo = (wr - w_hi.astype(F32)).astype(BF16)
    logits = _dot(h_hi, w_hi) + (_dot(h_hi, w_lo) + _dot(h_lo, w_hi)) + br_ref[...]
    tm = logits.shape[0]
    lane = lax.broadcasted_iota(jnp.int32, logits.shape, 1)
    l1 = jnp.max(logits, axis=-1, keepdims=True)
    i1 = jnp.min(jnp.where(logits == l1, lane, LANES), axis=-1, keepdims=True)
    rest = jnp.where(lane == i1, -jnp.inf, logits)
    l2 = jnp.max(rest, axis=-1, keepdims=True)
    i2 = jnp.min(jnp.where(rest == l2, lane, LANES), axis=-1, keepdims=True)
    e2 = jnp.exp(l2 - l1)
    w1 = 1.0 / (1.0 + e2)
    w2 = e2 * w1
    @pl.when(pl.program_id(0) == 0)
    def _():
        run_scr[...] = jnp.zeros(run_scr.shape, F32)

    oh1 = lane == i1
    oh2 = lane == i2
    earlier = (lax.broadcasted_iota(jnp.int32, (tm, tm), 1) < lax.broadcasted_iota(jnp.int32, (tm, tm), 0)).astype(BF16)
    c1 = _dot(earlier, oh1.astype(BF16))
    c2 = _dot(earlier, oh2.astype(BF16))
    n1 = jnp.sum(oh1.astype(F32), axis=0, keepdims=True)
    n2 = jnp.sum(oh2.astype(F32), axis=0, keepdims=True)
    run = run_scr[0:1, :]
    r1 = jnp.sum(jnp.where(oh1, run + c1, 0.0), axis=-1, keepdims=True)
    r2 = jnp.sum(jnp.where(oh2, run + n1 + c2, 0.0), axis=-1, keepdims=True)
    total = run + n1 + n2
    run_scr[...] = jnp.broadcast_to(total, run_scr.shape)
    counts_ref[...] = jnp.broadcast_to(total, counts_ref.shape)
    rec = jnp.zeros(logits.shape, F32)
    for slot, val in ((ROUTE_I1, i1.astype(F32)), (ROUTE_I2, i2.astype(F32)), (ROUTE_W1, w1), (ROUTE_W2, w2),
                      (ROUTE_R1, r1), (ROUTE_R2, r2)):
        rec = jnp.where(lane == slot, val, rec)
    route_ref[...] = rec


def _outproj(x, x_lat, mixed_ctx, mixed_lat, w_out, l, mod_l, g, router, *, n_ctx, t_lat, tm=512):
    split_x = x_lat is not None
    d = x.shape[1]
    n = x.shape[0] + (x_lat.shape[0] if split_x else 0)
    nct = n_ctx // tm
    nlt = n // tm - nct

    def cond_idx(i):
        return jnp.where(i < nct, 0, 1 + ((i - nct) * tm) // t_lat)

    row = lambda i: (i, 0)
    const = lambda i: (0, 0)
    ctx_row = lambda i: (jnp.minimum(i, nct - 1), 0)
    lat_row = lambda i: (jnp.clip(i - nct, 0, nlt - 1), 0)
    x_specs = [pl.BlockSpec((tm, d), ctx_row), pl.BlockSpec((tm, d), lat_row)] if split_x else [pl.BlockSpec((tm, d), row)]
    x_args = [x, x_lat] if split_x else [x]
    in_specs = (
        x_specs
        + [pl.BlockSpec((tm, o.shape[1]), ctx_row) for o in mixed_ctx]
        + [pl.BlockSpec((tm, o.shape[1]), lat_row) for o in mixed_lat]
        + [pl.BlockSpec((None, d, d), lambda i: (l, 0, 0)), pl.BlockSpec((1, 6, d), lambda i: (cond_idx(i), 0, 0)),
           pl.BlockSpec((1, d), const)]
    )
    args = [*x_args, *mixed_ctx, *mixed_lat, w_out, mod_l, g]
    n_experts = 0
    scratch = []
    if router is None:
        out_specs = [pl.BlockSpec((tm, d), row), pl.BlockSpec((tm, d), row)]
        out_shape = [jax.ShapeDtypeStruct((n, d), F32), jax.ShapeDtypeStruct((n, d), BF16)]
    else:
        wr, br, n_experts = router
        in_specs += [pl.BlockSpec((d, LANES), const), pl.BlockSpec((1, LANES), const)]
        args += [wr, br]
        out_specs = [pl.BlockSpec((tm, d), row), pl.BlockSpec((tm, d // 2), row), pl.BlockSpec((tm, LANES), row),
                     pl.BlockSpec((8, LANES), const)]
        out_shape = [jax.ShapeDtypeStruct((n, d), F32), jax.ShapeDtypeStruct((n, d // 2), jnp.uint32),
                     jax.ShapeDtypeStruct((n, LANES), F32), jax.ShapeDtypeStruct((8, LANES), F32)]
        scratch = [pltpu.VMEM((8, LANES), F32)]
    kern = functools.partial(_outproj_kernel, n_experts=n_experts, top_k=2, nct=nct, split_x=split_x)
    return pl.pallas_call(
        kern,
        grid=(n // tm,),
        in_specs=in_specs,
        out_specs=out_specs,
        out_shape=out_shape,
        scratch_shapes=scratch,
        compiler_params=_cparams(1),
        name="outproj_router" if n_experts else "outproj",
    )(*args)


def _row_copy(src_ref, src_row, dst_ref, dst_row, sem):
    return pltpu.make_async_copy(src_ref.at[pl.ds(src_row, 1), :], dst_ref.at[pl.ds(dst_row, 1), :], sem)


def _dispatch_kernel(pos_ref, hp_ref, hs_in_ref, hs_ref, sem, *, n, tm):
    del hs_in_ref
    base = pl.program_id(0) * tm

    def copies(k):
        return (_row_copy(hp_ref, k, hs_ref, pos_ref[base + k], sem),
                _row_copy(hp_ref, k, hs_ref, pos_ref[n + base + k], sem))

    def start(k, carry):
        for cp in copies(k):
            cp.start()
        return carry

    def wait(k, carry):
        for cp in copies(k):
            cp.wait()
        return carry

    lax.fori_loop(0, tm, start, 0, unroll=8)
    lax.fori_loop(0, tm, wait, 0, unroll=8)


def _dispatch(pos, hp, n_rows_sorted, tm=512):
    n, c = hp.shape
    hs0 = jnp.zeros((n_rows_sorted, c), hp.dtype)
    return pl.pallas_call(
        functools.partial(_dispatch_kernel, n=n, tm=tm),
        grid_spec=pltpu.PrefetchScalarGridSpec(
            num_scalar_prefetch=1,
            grid=(n // tm,),
            in_specs=[pl.BlockSpec((tm, c), lambda i, pos: (i, 0)), pl.BlockSpec(memory_space=pl.ANY)],
            out_specs=pl.BlockSpec(memory_space=pl.ANY),
            scratch_shapes=[pltpu.SemaphoreType.DMA(())],
        ),
        out_shape=jax.ShapeDtypeStruct((n_rows_sorted, c), hp.dtype),
        input_output_aliases={2: 0},
        compiler_params=_cparams(1),
        name="moe_dispatch",
    )(pos, hp, hs0)


def _moe_ffn_kernel(texp_ref, trow_ref, nact_ref, hs_ref, wg_ref, wu_ref, wd_ref, ys_ref, h_scr):
    del texp_ref, trow_ref
    i = pl.program_id(0)
    f = pl.program_id(1)
    active = i < nact_ref[0]

    @pl.when(active & (f == 0))
    def _():
        h_scr[...] = _unpack_bf16_pairs(hs_ref[...])

    def contribution():
        h = h_scr[...]
        t = _silu(_dot(h, wg_ref[...])) * _dot(h, wu_ref[...])
        return _dot(t.astype(BF16), wd_ref[...])

    @pl.when(active & (f == 0))
    def _():
        ys_ref[...] = contribution()

    @pl.when(active & (f > 0))
    def _():
        ys_ref[...] += contribution()

    @pl.when(jnp.logical_not(active) & (f == 0))
    def _():
        ys_ref[...] = jnp.zeros(ys_ref.shape, F32)


def _moe_ffn(tile_exp, tile_row, n_active, hs, wg, wu, wd, li, *, tm):
    r, c = hs.shape
    nf, d, tf = wg.shape[2:]

    def f_eff(i, f, nact):
        return jnp.where(i < nact[0], f, nf - 1)

    return pl.pallas_call(
        _moe_ffn_kernel,
        grid_spec=pltpu.PrefetchScalarGridSpec(
            num_scalar_prefetch=3,
            grid=(r // tm, nf),
            in_specs=[
                pl.BlockSpec((tm, c), lambda i, f, te, tr, na: (tr[i], 0)),
                pl.BlockSpec((None, None, None, d, tf), lambda i, f, te, tr, na: (li, te[i], f_eff(i, f, na), 0, 0)),
                pl.BlockSpec((None, None, None, d, tf), lambda i, f, te, tr, na: (li, te[i], f_eff(i, f, na), 0, 0)),
                pl.BlockSpec((None, None, tf, d), lambda i, f, te, tr, na: (li, te[i], f_eff(i, f, na), 0)),
            ],
            out_specs=pl.BlockSpec((tm, d), lambda i, f, te, tr, na: (i, 0)),
            scratch_shapes=[pltpu.VMEM((tm, d), BF16)],
        ),
        out_shape=jax.ShapeDtypeStruct((r, d), F32),
        compiler_params=_cparams(2),
        name="moe_ffn",
    )(tile_exp, tile_row, n_active, hs, wg, wu, wd)


def _combine_kernel(*refs, n, tm, final_norm, nct):
    refs = list(refs)
    pos_ref, ys_ref, x_ref, route_ref, mod_ref = refs[:5]
    refs = refs[5:]
    fg_ref = refs.pop(0) if final_norm else None
    o_ref = refs.pop(0)
    ol_ref = refs.pop(0) if final_norm else None
    buf, sem = refs
    base = pl.program_id(0) * tm

    def copies(k):
        return (_row_copy(ys_ref, pos_ref[base + k], buf.at[0], k, sem),
                _row_copy(ys_ref, pos_ref[n + base + k], buf.at[1], k, sem))

    def start(k, carry):
        for cp in copies(k):
            cp.start()
        return carry

    def wait(k, carry):
        for cp in copies(k):
            cp.wait()
        return carry

    lax.fori_loop(0, tm, start, 0, unroll=8)
    lax.fori_loop(0, tm, wait, 0, unroll=8)
    route = route_ref[...]
    lane = lax.broadcasted_iota(jnp.int32, route.shape, 1)
    w1 = jnp.sum(jnp.where(lane == ROUTE_W1, route, 0.0), axis=-1, keepdims=True)
    w2 = jnp.sum(jnp.where(lane == ROUTE_W2, route, 0.0), axis=-1, keepdims=True)
    y = x_ref[...] + mod_ref[0, 5:6, :] * (w1 * buf[0] + w2 * buf[1])
    if not final_norm:
        o_ref[...] = y
        return
    y = _rmsnorm(y, fg_ref[...])
    is_ctx = pl.program_id(0) < nct

    @pl.when(is_ctx)
    def _():
        o_ref[...] = y

    @pl.when(jnp.logical_not(is_ctx))
    def _():
        ol_ref[...] = y


def _combine(pos, ys, x, route, mod_l, final_g, *, n_ctx, t_lat, tm=512):
    n, d = x.shape
    nct = n_ctx // tm
    nlt = n // tm - nct

    def cond_idx(i):
        return jnp.where(i < nct, 0, 1 + ((i - nct) * tm) // t_lat)

    in_specs = [
        pl.BlockSpec(memory_space=pl.ANY),
        pl.BlockSpec((tm, d), lambda i, pos: (i, 0)),
        pl.BlockSpec((tm, LANES), lambda i, pos: (i, 0)),
        pl.BlockSpec((1, 6, d), lambda i, pos: (cond_idx(i), 0, 0)),
    ]
    args = [ys, x, route, mod_l]
    if final_g is None:
        out_specs = pl.BlockSpec((tm, d), lambda i, pos: (i, 0))
        out_shape = jax.ShapeDtypeStruct((n, d), F32)
    else:
        in_specs.append(pl.BlockSpec((1, d), lambda i, pos: (0, 0)))
        args.append(final_g)
        out_specs = [pl.BlockSpec((tm, d), lambda i, pos: (jnp.minimum(i, nct - 1), 0)),
                     pl.BlockSpec((tm, d), lambda i, pos: (jnp.clip(i - nct, 0, nlt - 1), 0))]
        out_shape = [jax.ShapeDtypeStruct((n_ctx, d), F32), jax.ShapeDtypeStruct((n - n_ctx, d), F32)]
    return pl.pallas_call(
        functools.partial(_combine_kernel, n=n, tm=tm, final_norm=final_g is not None, nct=nct),
        grid_spec=pltpu.PrefetchScalarGridSpec(
            num_scalar_prefetch=1,
            grid=(n // tm,),
            in_specs=in_specs,
            out_specs=out_specs,
            scratch_shapes=[pltpu.VMEM((2, tm, d), F32), pltpu.SemaphoreType.DMA(())],
        ),
        out_shape=out_shape,
        compiler_params=_cparams(1),
        name="moe_combine",
    )(pos, *args)


def _routing_tables(route, counts, n_experts, tm):
    n = route.shape[0]
    n_tiles = 2 * n // tm + n_experts
    cnt = counts[0, :n_experts].astype(jnp.int32)
    tiles_e = (cnt + tm - 1) // tm
    tile_end = jnp.cumsum(tiles_e)
    offs = (tile_end - tiles_e) * tm
    i1 = route[:, ROUTE_I1].astype(jnp.int32)
    i2 = route[:, ROUTE_I2].astype(jnp.int32)
    eidx = jnp.arange(n_experts, dtype=jnp.int32)[None, :]
    off1 = jnp.sum(jnp.where(i1[:, None] == eidx, offs[None, :], 0), axis=1)
    off2 = jnp.sum(jnp.where(i2[:, None] == eidx, offs[None, :], 0), axis=1)
    pos = jnp.concatenate([off1 + route[:, ROUTE_R1].astype(jnp.int32), off2 + route[:, ROUTE_R2].astype(jnp.int32)])
    n_active = tile_end[-1]
    t = jnp.arange(n_tiles, dtype=jnp.int32)
    t_eff = jnp.minimum(t, n_active - 1)
    tile_exp = jnp.sum((t_eff[:, None] >= tile_end[None, :]).astype(jnp.int32), axis=1)
    return pos, tile_exp.astype(jnp.int32), t_eff.astype(jnp.int32), n_active.reshape(1).astype(jnp.int32), n_tiles


def _ffn_kernel(*refs, final_norm):
    refs = list(refs)
    h_ref, x_ref, wg_ref, wu_ref, wd_ref, mod_ref = refs[:6]
    refs = refs[6:]
    fg_ref = refs.pop(0) if final_norm else None
    (o_ref,) = refs
    f = pl.program_id(1)

    @pl.when(f == 0)
    def _():
        o_ref[...] = jnp.zeros(o_ref.shape, F32)

    h = h_ref[...]
    t = _silu(_dot(h, wg_ref[...])) * _dot(h, wu_ref[...])
    o_ref[...] += _dot(t.astype(BF16), wd_ref[...])

    @pl.when(f == pl.num_programs(1) - 1)
    def _():
        y = x_ref[...] + mod_ref[0, 5:6, :] * o_ref[...]
        if final_norm:
            y = _rmsnorm(y, fg_ref[...])
        o_ref[...] = y


def _ffn(h, x, wg, wu, wd, li, mod_l, final_g, *, n_ctx, t_lat, tm=512):
    n, d = x.shape
    nf, _, tf = wg.shape[1:]
    nct = n_ctx // tm

    def cond_idx(i):
        return jnp.where(i < nct, 0, 1 + ((i - nct) * tm) // t_lat)

    row = lambda i, f: (i, 0)
    in_specs = [
        pl.BlockSpec((tm, d), row),
        pl.BlockSpec((tm, d), row),
        pl.BlockSpec((None, None, d, tf), lambda i, f: (li, f, 0, 0)),
        pl.BlockSpec((None, None, d, tf), lambda i, f: (li, f, 0, 0)),
        pl.BlockSpec((None, tf, d), lambda i, f: (li, f, 0)),
        pl.BlockSpec((1, 6, d), lambda i, f: (cond_idx(i), 0, 0)),
    ]
    args = [h, x, wg, wu, wd, mod_l]
    if final_g is not None:
        in_specs.append(pl.BlockSpec((1, d), lambda i, f: (0, 0)))
        args.append(final_g)
    return pl.pallas_call(
        functools.partial(_ffn_kernel, final_norm=final_g is not None),
        grid=(n // tm, nf),
        in_specs=in_specs,
        out_specs=pl.BlockSpec((tm, d), row),
        out_shape=jax.ShapeDtypeStruct((n, d), F32),
        compiler_params=_cparams(2),
        name="ffn_dense",
    )(*args)


def kernel(x_prompt, x_sample, cache_nat_k, cache_nat_v, state_gla, cache_swa_k, cache_swa_v, c, c_ctx, w_ada, b_ada,
           norm_mix_g, norm_ffn_g, w_in, na_rpb, gla_w_gate, gla_b_gate, gla_norm_g, swa_sink, w_out, ffn_w_gate,
           ffn_w_up, ffn_w_down, moe_w_router, moe_b_router, moe_w_gate, moe_w_up, moe_w_down, final_norm_g):
    nb, seq, d = x_prompt.shape
    db, t_lat, _ = x_sample.shape
    depth = w_in.shape[0]
    p_len, h_a = cache_nat_k.shape[2], cache_nat_k.shape[3]
    h_b, dk_b, dv_b = state_gla.shape[3:6]
    hkv_c, g_c = swa_sink.shape[1], swa_sink.shape[2]
    rank = gla_w_gate.shape[2]
    n_exp = moe_w_router.shape[-1]
    w_a = h_a * HEAD_DIM
    w_bk = h_b * dk_b
    w_bv = h_b * dv_b
    w_cq = hkv_c * g_c * HEAD_DIM
    w_ck = hkv_c * HEAD_DIM
    n_ctx = nb * seq
    n_lat = db * t_lat
    n = n_ctx + n_lat
    z0 = 3 * w_a + 2 * w_bk + 2 * w_bv
    z1 = z0 + 2 * rank
    a_w = 3 * w_a
    b_w = 2 * w_bk + 2 * w_bv
    assert a_w == b_w and (a_w + b_w) % w_cq == 0 and (a_w + b_w + w_cq) % w_ck == 0 and 2 * rank <= LANES
    assert db + 1 <= 8

    x, x_lat = x_prompt.reshape(n_ctx, d), x_sample.reshape(n_lat, d)
    cond8 = jnp.concatenate([c_ctx[None, :], c, jnp.zeros((7 - db, d), F32)], axis=0)
    mod = _adaln(cond8, w_ada, b_ada).reshape(depth, 8, 6, d)

    cos, sin = _rope_tables(t_lat, w_ck)
    na_tab = _na_bias_tables(na_rpb)
    kc_a = cache_nat_k.reshape(db, depth, p_len, w_a)
    vc_a = cache_nat_v.reshape(db, depth, p_len, w_a)
    kc_c = cache_swa_k.reshape(db, depth, p_len, w_ck)
    vc_c = cache_swa_v.reshape(db, depth, p_len, w_ck)
    sizes = dict(n_ctx=n_ctx, t_lat=t_lat)

    kv_bufs = [jnp.zeros((nb, depth, seq, w), F32) for w in (w_a, w_a, w_ck, w_ck)]
    gla_st = jnp.zeros((nb, depth, 2, h_b, dk_b, dv_b), F32)
    w_main = jnp.concatenate([w_in[:, :, :z0], w_in[:, :, z1:]], axis=2).astype(BF16)
    wz = jnp.pad(w_in[:, :, z0:z1], ((0, 0), (0, 0), (0, LANES - 2 * rank))).astype(BF16)
    w_out_b = w_out.astype(BF16)
    ffn_wg, ffn_wu = _tile_columns(ffn_w_gate, FFN_TF), _tile_columns(ffn_w_up, FFN_TF)
    ffn_wd = ffn_w_down.astype(BF16)
    moe_tf = moe_w_gate.shape[-1] // 2
    moe_wg, moe_wu = _tile_columns(moe_w_gate, moe_tf), _tile_columns(moe_w_up, moe_tf)
    moe_wd = moe_w_down.astype(BF16)
    for l in range(depth):
        u, z, *kv_bufs = _inproj(x, x_lat, mod[l], norm_mix_g[l][None, :], w_main, wz, kv_bufs, l, seq=seq, w_a=w_a,
                                 w_cq=w_cq, w_kvc=w_ck, **sizes)

        oa_c = _ctx_attn(u, None, nb=nb, seq=seq, qcol=0, kcol=1, vcol=2, wq=w_a, wk=w_a, hkv=h_a, g=1)
        oa_l = _na_attn(u, kc_a, vc_a, na_tab, l, n_ctx=n_ctx, db=db, t_lat=t_lat, w_a=w_a, heads=h_a)
        gw = jnp.zeros((2, LANES, w_bk), F32)
        gw = gw.at[0, 0:rank].set(gla_w_gate[l, 0]).at[1, rank:2 * rank].set(gla_w_gate[l, 1])
        gb = gla_b_gate[l][:, None, :]
        gn = gla_norm_g[l].reshape(1, w_bv)
        gla_args = dict(bcol=a_w // b_w, heads=h_b, dk=dk_b, dv=dv_b)
        ob_c, gla_st = _gla(u, z, gw, gb, gn, None, gla_st, l, row0=0, nb=nb, seq_len=seq, **gla_args)
        (ob_l,) = _gla(u, z, gw, gb, gn, state_gla, None, l, row0=n_ctx, nb=db, seq_len=t_lat, **gla_args)
        c0 = a_w + b_w
        cols = dict(qcol=c0 // w_cq, kcol=(c0 + w_cq) // w_ck, vcol=(c0 + w_cq) // w_ck + 1, hkv=hkv_c, g=g_c)
        oc_c = _ctx_attn(u, swa_sink[l], nb=nb, seq=seq, wq=w_cq, wk=w_ck, **cols)
        oc_l = _swa_attn(u, swa_sink[l], kc_c, vc_c, cos, sin, l, n_ctx=n_ctx, db=db, t_lat=t_lat, **cols)

        moe = l % 2 == 1
        i = l // 2
        router = None
        if moe:
            wr = jnp.pad(moe_w_router[i], ((0, 0), (0, LANES - n_exp)))
            br = jnp.concatenate([moe_b_router[i], jnp.full((LANES - n_exp,), NEG_INF, F32)])[None, :]
            router = (wr, br, n_exp)
        res = _outproj(x, x_lat, (oa_c, ob_c, oc_c), (oa_l, ob_l, oc_l), w_out_b, l, mod[l],
                       norm_ffn_g[l][None, :], router, **sizes)
        x_lat = None
        fg = final_norm_g[None, :] if l == depth - 1 else None
        if moe:
            xn, hp, route, counts = res
            tm_e = 512
            pos, tile_exp, tile_row, n_active, n_tiles = _routing_tables(route, counts, n_exp, tm_e)
            hs = _dispatch(pos, hp, n_tiles * tm_e)
            ys = _moe_ffn(tile_exp, tile_row, n_active, hs, moe_wg, moe_wu, moe_wd, i, tm=tm_e)
            x = _combine(pos, ys, xn, route, mod[l], fg, **sizes)
        else:
            xn, h2 = res
            x = _ffn(h2, xn, ffn_wg, ffn_wu, ffn_wd, i, mod[l], fg, **sizes)

    y_ctx, y_lat = x if isinstance(x, (list, tuple)) else (x[:n_ctx], x[n_ctx:])
    nat_k, nat_v, swa_k, swa_v = kv_bufs
    return (y_ctx.reshape(nb, seq, d), y_lat.reshape(db, t_lat, d),
            nat_k.reshape(nb, depth, seq, h_a, HEAD_DIM), nat_v.reshape(nb, depth, seq, h_a, HEAD_DIM), gla_st,
            swa_k.reshape(nb, depth, seq, hkv_c, HEAD_DIM), swa_v.reshape(nb, depth, seq, hkv_c, HEAD_DIM))
```

```python
import functools

import numpy as np
import jax
import jax.numpy as jnp
from jax import lax
from jax.experimental import pallas as pl
from jax.experimental.pallas import tpu as pltpu

F32 = jnp.float32
BF16 = jnp.bfloat16
HIGHEST = lax.Precision.HIGHEST

EPS = 1e-6
NEG_INF = -1e30
GRID_W = 64
HEAD_DIM = 64
NA_ROWS = 8
NA_COLS = 16
SWA_WIN = 128
SWA_QBLOCK = 128
GLA_CHUNK = 64
GLA_GROUP = 4
GLA_TAU = 16.0
ROPE_BASE = 10000.0
LANES = 128
FFN_TF = 512
CTX_HEAD_BATCH = 16
NA_HEAD_BATCH = 4
SWA_HEAD_BATCH = 16
VMEM_LIMIT = 56 * 1024 * 1024


def _cparams(n_axes):
    return pltpu.CompilerParams(dimension_semantics=("arbitrary",) * n_axes, vmem_limit_bytes=VMEM_LIMIT)


def _dot(a, b, **kw):
    return jnp.dot(a, b, preferred_element_type=F32, **kw)


def _dot_nt(a, b, **kw):
    return lax.dot_general(a, b, (((1,), (1,)), ((), ())), preferred_element_type=F32, **kw)


def _dot_tn(a, b, **kw):
    return lax.dot_general(a, b, (((0,), (0,)), ((), ())), preferred_element_type=F32, **kw)


def _silu(x):
    return x / (1.0 + jnp.exp(-x))


def _split_bf16(x):
    hi = x.astype(BF16)
    return hi, (x - hi.astype(F32)).astype(BF16)


def _rmsnorm(x, g):
    return x * lax.rsqrt(jnp.mean(x * x, axis=-1, keepdims=True) + EPS) * g


def _adaln_kernel(cond_ref, w_ref, b_ref, o_ref):
    s = _silu(cond_ref[...]).astype(BF16)
    o_ref[0] = _dot(s, w_ref[0].astype(BF16)) + b_ref[0]


def _adaln(cond8, w_ada, b_ada, tn=1024):
    depth, d, n6 = w_ada.shape
    return pl.pallas_call(
        _adaln_kernel,
        grid=(depth, n6 // tn),
        in_specs=[
            pl.BlockSpec((8, d), lambda l, j: (0, 0)),
            pl.BlockSpec((1, d, tn), lambda l, j: (l, 0, j)),
            pl.BlockSpec((1, 1, tn), lambda l, j: (l, 0, j)),
        ],
        out_specs=pl.BlockSpec((1, 8, tn), lambda l, j: (l, 0, j)),
        out_shape=jax.ShapeDtypeStruct((depth, 8, n6), F32),
        compiler_params=_cparams(2),
        name="adaln",
    )(cond8, w_ada, b_ada.reshape(depth, 1, n6))


def _inproj_kernel(*refs, nct, seq, w_a, w_cq, w_kvc, split_x):
    refs = list(refs)
    x_ref = refs.pop(0)
    xl_ref = refs.pop(0) if split_x else None
    mod_ref, g_ref, w_ref, wz_ref = refs[:4]
    u_ref, z_ref, nk_ref, nv_ref, sk_ref, sv_ref, h_scr = refs[8:]
    i = pl.program_id(0)
    j = pl.program_id(1)
    is_ctx = i < nct

    @pl.when(j == 0)
    def _():
        x = jnp.where(is_ctx, x_ref[...], xl_ref[...]) if split_x else x_ref[...]
        h = _rmsnorm(x, g_ref[...]) * (1.0 + mod_ref[0, 1:2, :]) + mod_ref[0, 0:1, :]
        hb = h.astype(BF16)
        h_scr[...] = hb
        z_ref[...] = _dot(hb, wz_ref[...])

    acc = _dot(h_scr[...], w_ref[...])
    u_ref[...] = acc.astype(BF16)
    nseq = nk_ref.shape[0]

    @pl.when(is_ctx & (j == 0))
    def _():
        for bb in range(nseq):
            nk_ref[bb, 0] = acc[bb * seq:(bb + 1) * seq, w_a:2 * w_a]
            nv_ref[bb, 0] = acc[bb * seq:(bb + 1) * seq, 2 * w_a:3 * w_a]

    @pl.when(is_ctx & (j == 2))
    def _():
        for bb in range(nseq):
            sk_ref[bb, 0] = acc[bb * seq:(bb + 1) * seq, w_cq:w_cq + w_kvc]
            sv_ref[bb, 0] = acc[bb * seq:(bb + 1) * seq, w_cq + w_kvc:w_cq + 2 * w_kvc]


def _inproj(x, x_lat, mod_l, g, w_main, wz, kv_bufs, l, *, n_ctx, seq, t_lat, w_a, w_cq, w_kvc, tm=512):
    split_x = x_lat is not None
    d = x.shape[1]
    n = x.shape[0] + (x_lat.shape[0] if split_x else 0)
    n_main = w_main.shape[2]
    tn = 3 * w_a
    nct = n_ctx // tm
    nlt = n // tm - nct
    nseq = tm // seq
    assert n_ctx % tm == 0 and tm % seq == 0 and t_lat % tm == 0 and n_main == 3 * tn and w_cq + 2 * w_kvc == tn

    def cond_idx(i):
        return jnp.where(i < nct, 0, 1 + ((i - nct) * tm) // t_lat)

    kv_idx = lambda i, j: (jnp.minimum(i, nct - 1), l, 0, 0)
    if split_x:
        x_specs = [pl.BlockSpec((tm, d), lambda i, j: (jnp.minimum(i, nct - 1), 0)),
                   pl.BlockSpec((tm, d), lambda i, j: (jnp.clip(i - nct, 0, nlt - 1), 0))]
        x_args = [x, x_lat]
    else:
        x_specs = [pl.BlockSpec((tm, d), lambda i, j: (i, 0))]
        x_args = [x]
    n_in = len(x_args) + 4
    kern = functools.partial(_inproj_kernel, nct=nct, seq=seq, w_a=w_a, w_cq=w_cq, w_kvc=w_kvc, split_x=split_x)
    return pl.pallas_call(
        kern,
        grid=(n // tm, n_main // tn),
        in_specs=x_specs + [
            pl.BlockSpec((1, 6, d), lambda i, j: (cond_idx(i), 0, 0)),
            pl.BlockSpec((1, d), lambda i, j: (0, 0)),
            pl.BlockSpec((None, d, tn), lambda i, j: (l, 0, j)),
            pl.BlockSpec((None, d, LANES), lambda i, j: (l, 0, 0)),
        ] + [pl.BlockSpec(memory_space=pl.ANY)] * 4,
        out_specs=[
            pl.BlockSpec((tm, tn), lambda i, j: (i, j)),
            pl.BlockSpec((tm, LANES), lambda i, j: (i, 0)),
        ] + [pl.BlockSpec((nseq, 1, seq, b.shape[-1]), kv_idx) for b in kv_bufs],
        out_shape=[
            jax.ShapeDtypeStruct((n, n_main), BF16),
            jax.ShapeDtypeStruct((n, LANES), F32),
        ] + [jax.ShapeDtypeStruct(b.shape, b.dtype) for b in kv_bufs],
        scratch_shapes=[pltpu.VMEM((tm, d), BF16)],
        input_output_aliases={n_in + k: 2 + k for k in range(4)},
        compiler_params=_cparams(2),
        name="inproj",
    )(*x_args, mod_l, g, w_main, wz, *kv_bufs)


def _swap_halves(x):
    return pltpu.roll(x.astype(F32), LANES // 2, 1).astype(x.dtype)


def _half_head_attention(heads, batch):
    if len(heads) > batch:
        return _half_head_attention(heads[:batch], batch) + _half_head_attention(heads[batch:], batch)
    scores =[[_dot_nt(qm, kp) if fn is None else fn(_dot_nt(qm, kp)) for kp, fn in zip(keys, fns)]
              for qm, keys, _, fns, _ in heads]
    maxes = []
    for ss, (_, _, _, _, sink) in zip(scores, heads):
        m = jnp.max(ss[0], axis=-1, keepdims=True)
        for s in ss[1:]:
            m = jnp.maximum(m, jnp.max(s, axis=-1, keepdims=True))
        maxes.append(m if sink is None else jnp.maximum(m, sink))
    probs = [[jnp.exp(s - m).astype(BF16) for s in ss] for ss, m in zip(scores, maxes)]
    outs = []
    for ps, m, (_, _, values_aug, _, sink) in zip(probs, maxes, heads):
        acc = _dot(ps[0], values_aug[0])
        for p, va in zip(ps[1:], values_aug[1:]):
            acc = acc + _dot(p, va)
        den = pltpu.roll(acc, LANES // 2, 1)
        if sink is not None:
            den = den + jnp.exp(sink - m)
        outs.append(acc / den)
    return outs


def _ctx_attn_kernel(*refs, hkv, g, has_sink):
    if has_sink:
        sink_ref, q_ref, k_ref, v_ref, o_ref = refs
    else:
        q_ref, k_ref, v_ref, o_ref = refs
    q = q_ref[...] * jnp.asarray(HEAD_DIM ** -0.5, BF16)
    k = k_ref[...]
    v = v_ref[...]
    rows = q.shape[0]
    low = lax.broadcasted_iota(jnp.int32, (rows, LANES), 1) < HEAD_DIM
    kv_pairs = {}

    def kv_pair(kh, half):
        if (kh, half) not in kv_pairs:
            sl = slice((kh // 2) * LANES, (kh // 2 + 1) * LANES)
            kp, vp = k[:, sl], v[:, sl]
            if kh % 2 != half:
                kp, vp = _swap_halves(kp), _swap_halves(vp)
            kv_pairs[(kh, half)] = (kp, vp)
        return kv_pairs[(kh, half)]

    heads = []
    for hq in range(hkv * g):
        pair, half = hq // 2, hq % 2
        kh, gi = hq // g, hq % g
        qp = q[:, pair * LANES:(pair + 1) * LANES]
        own = low if half == 0 else jnp.logical_not(low)
        kp, vp = kv_pair(kh, half)
        sk = sink_ref[kh, gi] if has_sink else None
        heads.append((jnp.where(own, qp, 0), [kp], [jnp.where(own, vp, 1)], [None], sk))
    res = _half_head_attention(heads, CTX_HEAD_BATCH)
    o_ref[...] = jnp.concatenate([jnp.where(low, res[2 * p], res[2 * p + 1]) for p in range(len(res) // 2)],
                                 axis=-1).astype(BF16)


def _ctx_attn(u, sink, *, nb, seq, qcol, kcol, vcol, wq, wk, hkv, g):
    has_sink = sink is not None
    kern = functools.partial(_ctx_attn_kernel, hkv=hkv, g=g, has_sink=has_sink)
    in_specs = [
        pl.BlockSpec((seq, wq), lambda b: (b, qcol)),
        pl.BlockSpec((seq, wk), lambda b: (b, kcol)),
        pl.BlockSpec((seq, wk), lambda b: (b, vcol)),
    ]
    args = [u, u, u]
    if has_sink:
        in_specs = [pl.BlockSpec(memory_space=pltpu.SMEM)] + in_specs
        args = [sink] + args
    return pl.pallas_call(
        kern,
        grid=(nb,),
        in_specs=in_specs,
        out_specs=pl.BlockSpec((seq, wq), lambda b: (b, 0)),
        out_shape=jax.ShapeDtypeStruct((nb * seq, wq), BF16),
        compiler_params=_cparams(1),
        name="ctx_attn_sink" if has_sink else "ctx_attn",
    )(*args)


NA_QROWS = 4
NA_WIN_ROWS = 12


def _na_kernel(q_ref, k_ref, v_ref, kc_ref, vc_ref, bias_ref, o_ref, *, heads, rows):
    qi = pl.program_id(1)
    r0 = qi * NA_QROWS
    w0 = jnp.clip(r0 - NA_ROWS // 2, 0, rows - NA_WIN_ROWS)
    k0 = pl.multiple_of(w0 * GRID_W, NA_QROWS * GRID_W)
    tq = NA_QROWS * GRID_W
    tk = NA_WIN_ROWS * GRID_W
    q = q_ref[...] * jnp.asarray(HEAD_DIM ** -0.5, BF16)
    k = k_ref[pl.ds(k0, tk), :]
    v = v_ref[pl.ds(k0, tk), :]
    kc = kc_ref[0, 0].astype(BF16)
    vc = vc_ref[0, 0].astype(BF16)
    qrow = r0 + lax.broadcasted_iota(jnp.int32, (tq, tk), 0) // GRID_W
    krow = w0 + lax.broadcasted_iota(jnp.int32, (tq, tk), 1) // GRID_W
    band0 = jnp.clip(qrow - NA_ROWS // 2, 0, rows - NA_ROWS)
    in_band = (krow >= band0) & (krow < band0 + NA_ROWS)
    low_q = lax.broadcasted_iota(jnp.int32, (tq, LANES), 1) < HEAD_DIM
    low_k = lax.broadcasted_iota(jnp.int32, (tk, LANES), 1) < HEAD_DIM
    low_c = lax.broadcasted_iota(jnp.int32, (kc.shape[0], LANES), 1) < HEAD_DIM

    def window_scores(h):
        def fn(s):
            bias = jnp.concatenate(
                [jnp.concatenate([bias_ref[h, w0 + 2 * p - (r0 + rq) + 2 * NA_ROWS - 1]
                                  for p in range(NA_WIN_ROWS // 2)], axis=1) for rq in range(NA_QROWS)], axis=0)
            return jnp.where(in_band, s + bias, NEG_INF)
        return fn

    head_specs = []
    for h in range(heads):
        sl = slice((h // 2) * LANES, (h // 2 + 1) * LANES)
        first = h % 2 == 0
        own = lambda low: low if first else jnp.logical_not(low)
        head_specs.append((jnp.where(own(low_q), q[:, sl], 0), [k[:, sl], kc[:, sl]],
                           [jnp.where(own(low_k), v[:, sl], 1), jnp.where(own(low_c), vc[:, sl], 1)],
                           [window_scores(h), None], None))
    res = _half_head_attention(head_specs, NA_HEAD_BATCH)
    o_ref[...] = jnp.concatenate([jnp.where(low_q, res[2 * p], res[2 * p + 1]) for p in range(heads // 2)],
                                 axis=-1).astype(BF16)


def _na_attn(u, kc, vc, bias, l, *, n_ctx, db, t_lat, w_a, heads):
    tq = NA_QROWS * GRID_W
    rows = t_lat // GRID_W
    nq = t_lat // tq
    p = kc.shape[2]
    assert rows >= NA_WIN_ROWS and NA_WIN_ROWS >= NA_QROWS + NA_ROWS - 1 and (rows - NA_WIN_ROWS) % NA_QROWS == 0
    kern = functools.partial(_na_kernel, heads=heads, rows=rows)
    return pl.pallas_call(
        kern,
        grid=(db, nq),
        in_specs=[
            pl.BlockSpec((tq, w_a), lambda b, i: (n_ctx // tq + b * nq + i, 0)),
            pl.BlockSpec((t_lat, w_a), lambda b, i: (n_ctx // t_lat + b, 1)),
            pl.BlockSpec((t_lat, w_a), lambda b, i: (n_ctx // t_lat + b, 2)),
            pl.BlockSpec((1, 1, p, w_a), lambda b, i: (b, l, 0, 0)),
            pl.BlockSpec((1, 1, p, w_a), lambda b, i: (b, l, 0, 0)),
            pl.BlockSpec((None,) + bias.shape[1:], lambda b, i: (l, 0, 0, 0, 0)),
        ],
        out_specs=pl.BlockSpec((tq, w_a), lambda b, i: (b * nq + i, 0)),
        out_shape=jax.ShapeDtypeStruct((db * t_lat, w_a), BF16),
        compiler_params=_cparams(2),
        name="na_attn",
    )(u, u, u, kc, vc, bias)


def _na_bias_tables(rpb):
    col = np.arange(GRID_W)
    col_start = np.clip(col - NA_COLS // 2, 0, GRID_W - NA_COLS)
    col_ok = (col[None, :] >= col_start[:, None]) & (col[None, :] < col_start[:, None] + NA_COLS)
    dc = np.clip(col[None, :] - col[:, None] + NA_COLS - 1, 0, 2 * NA_COLS - 2)
    pick = (np.arange(2 * NA_COLS - 1)[:, None, None] == dc[None]).astype(np.float32)
    t1 = jnp.einsum("lhdj,jck->lhdck", rpb.astype(F32), jnp.asarray(pick), precision=HIGHEST)
    t1 = jnp.where(jnp.asarray(col_ok), t1, NEG_INF)
    t1 = jnp.pad(t1, ((0, 0), (0, 0), (NA_ROWS, NA_ROWS), (0, 0), (0, 0)), constant_values=NEG_INF)
    return jnp.concatenate([t1[:, :, :-1], t1[:, :, 1:]], axis=-1)


def _rope(x, cos, sin_signed):
    w = x.shape[-1]
    q4 = HEAD_DIM // 4
    lane = lax.broadcasted_iota(jnp.int32, x.shape, 1)
    first = (lane % (2 * q4)) < q4
    swapped = jnp.where(first, pltpu.roll(x, w - q4, 1), pltpu.roll(x, q4, 1))
    return x * cos + swapped * sin_signed


def _swa_kernel(sink_ref, q_ref, k_ref, v_ref, kc_ref, vc_ref, cos_ref, sin_ref, o_ref,
                ks_scr, vs_scr, kcs_scr, vcs_scr, *, hkv, g, t_lat):
    n = pl.program_id(1)
    qb = SWA_QBLOCK
    wk = hkv * HEAD_DIM
    span = 3 * qb

    @pl.when(n == 0)
    def _():
        k_r = _rope(k_ref[...].astype(F32), cos_ref[...], sin_ref[...]).astype(BF16)
        sources = ((k_r, ks_scr, False), (v_ref[...], vs_scr, True),
                   (kc_ref[0, 0].astype(BF16), kcs_scr, False), (vc_ref[0, 0].astype(BF16), vcs_scr, True))
        for src, dst, is_value in sources:
            low = lax.broadcasted_iota(jnp.int32, (src.shape[0], LANES), 1) < HEAD_DIM
            for kh in range(hkv):
                slab = src[:, (kh // 2) * LANES:(kh // 2 + 1) * LANES]
                for half in range(2):
                    x = slab if kh % 2 == half else _swap_halves(slab)
                    if is_value:
                        x = jnp.where(low if half == 0 else jnp.logical_not(low), x, 1)
                    dst[kh * 2 + half] = x

    q0 = pl.multiple_of(n * qb, qb)
    start = pl.multiple_of(jnp.clip(n * qb - qb, 0, t_lat - span), qb)
    cos_q = cos_ref[pl.ds(q0, qb), :]
    sin_q = sin_ref[pl.ds(q0, qb), :]
    qpos = q0 + lax.broadcasted_iota(jnp.int32, (qb, span), 0)
    kpos = start + lax.broadcasted_iota(jnp.int32, (qb, span), 1)
    valid = jnp.abs(qpos - kpos) <= SWA_WIN
    in_window = lambda s: jnp.where(valid, s, NEG_INF)
    low = lax.broadcasted_iota(jnp.int32, (qb, LANES), 1) < HEAD_DIM
    scale = jnp.asarray(HEAD_DIM ** -0.5, BF16)
    heads = []
    for kh in range(hkv):
        q_r = _rope(q_ref[:, kh * wk:(kh + 1) * wk].astype(F32), cos_q, sin_q).astype(BF16) * scale
        for gi in range(g):
            half = gi % 2
            idx = kh * 2 + half
            qp = q_r[:, (gi // 2) * LANES:(gi // 2 + 1) * LANES]
            heads.append((jnp.where(low if half == 0 else jnp.logical_not(low), qp, 0),
                          [ks_scr[idx, pl.ds(start, span), :], kcs_scr[idx]],
                          [vs_scr[idx, pl.ds(start, span), :], vcs_scr[idx]],
                          [in_window, None], sink_ref[kh, gi]))
    res = _half_head_attention(heads, SWA_HEAD_BATCH)
    o_ref[...] = jnp.concatenate([jnp.where(low, res[2 * p], res[2 * p + 1]) for p in range(len(res) // 2)],
                                 axis=-1).astype(BF16)


def _swa_attn(u, sink, kc, vc, cos, sin, l, *, n_ctx, db, t_lat, qcol, kcol, vcol, hkv, g):
    qb = SWA_QBLOCK
    nq = t_lat // qb
    wq = hkv * g * HEAD_DIM
    wk = hkv * HEAD_DIM
    assert g * HEAD_DIM == wk, "rotary tables are shared between the q groups and k"
    p = kc.shape[2]
    kern = functools.partial(_swa_kernel, hkv=hkv, g=g, t_lat=t_lat)
    return pl.pallas_call(
        kern,
        grid=(db, nq),
        in_specs=[
            pl.BlockSpec(memory_space=pltpu.SMEM),
            pl.BlockSpec((qb, wq), lambda b, i: (n_ctx // qb + b * nq + i, qcol)),
            pl.BlockSpec((t_lat, wk), lambda b, i: (n_ctx // t_lat + b, kcol)),
            pl.BlockSpec((t_lat, wk), lambda b, i: (n_ctx // t_lat + b, vcol)),
            pl.BlockSpec((1, 1, p, wk), lambda b, i: (b, l, 0, 0)),
            pl.BlockSpec((1, 1, p, wk), lambda b, i: (b, l, 0, 0)),
            pl.BlockSpec((t_lat, wk), lambda b, i: (0, 0)),
            pl.BlockSpec((t_lat, wk), lambda b, i: (0, 0)),
        ],
        out_specs=pl.BlockSpec((qb, wq), lambda b, i: (b * nq + i, 0)),
        out_shape=jax.ShapeDtypeStruct((db * t_lat, wq), BF16),
        scratch_shapes=[pltpu.VMEM((2 * hkv, t_lat, LANES), BF16), pltpu.VMEM((2 * hkv, t_lat, LANES), BF16),
                        pltpu.VMEM((2 * hkv, p, LANES), BF16), pltpu.VMEM((2 * hkv, p, LANES), BF16)],
        compiler_params=_cparams(2),
        name="swa_attn",
    )(sink, u, u, u, kc, vc, cos, sin)


def _rope_tables(t_lat, width):
    quarter = HEAD_DIM // 4
    pos = np.arange(t_lat)
    freqs = ROPE_BASE ** (-np.arange(quarter, dtype=np.float32) / quarter)
    ang_r = (pos // GRID_W).astype(np.float32)[:, None] * freqs[None, :]
    ang_c = (pos % GRID_W).astype(np.float32)[:, None] * freqs[None, :]
    ang_r, ang_c = jnp.asarray(ang_r, F32), jnp.asarray(ang_c, F32)
    cos = jnp.concatenate([jnp.cos(ang_r), jnp.cos(ang_r), jnp.cos(ang_c), jnp.cos(ang_c)], axis=-1)
    sin = jnp.concatenate([-jnp.sin(ang_r), jnp.sin(ang_r), -jnp.sin(ang_c), jnp.sin(ang_c)], axis=-1)
    reps = width // HEAD_DIM
    return jnp.tile(cos, (1, reps)), jnp.tile(sin, (1, reps))


def _log_sigmoid(x):
    return jnp.minimum(x, 0.0) - jnp.log(1.0 + jnp.exp(-jnp.abs(x)))


def _gla_kernel(*refs, heads, dk, dv, seq_len, has_s0, out_state):
    refs = list(refs)
    ub_ref, z_ref, gw_ref, gb_ref, gn_ref = refs[:5]
    refs = refs[5:]
    s0_ref = refs.pop(0) if has_s0 else None
    if out_state:
        refs.pop(0)
    o_ref = refs.pop(0)
    st_ref = refs.pop(0) if out_state else None
    la_scr, o_scr, st_scr = refs
    c_len = GLA_CHUNK
    nc = seq_len // c_len
    wk = heads * dk
    wv = heads * dv

    z_hi, z_lo = _split_bf16(z_ref[...])
    for d in range(2):
        g_hi, g_lo = _split_bf16(gw_ref[d])
        pre = _dot(z_hi, g_hi) + (_dot(z_hi, g_lo) + _dot(z_lo, g_hi)) + gb_ref[d]
        la_scr[d] = _log_sigmoid(pre) / GLA_TAU

    eye_v = (lax.broadcasted_iota(jnp.int32, (dv, dv), 0) == lax.broadcasted_iota(jnp.int32, (dv, dv), 1)).astype(F32)
    for d in range(2):
        for h in range(heads):
            if has_s0:
                s0_rows = jnp.concatenate(
                    [s0_ref[0, 0, d, h] if hh == h else jnp.zeros((dk, dv), F32) for hh in range(heads)], axis=0)
                st_scr[d, h] = _dot_nt(eye_v, s0_rows, precision=HIGHEST)
            else:
                st_scr[d, h] = jnp.zeros((dv, wk), F32)

    ng = GLA_GROUP
    gr = ng * c_len
    ti = lax.broadcasted_iota(jnp.int32, (gr, gr), 0)
    si = lax.broadcasted_iota(jnp.int32, (gr, gr), 1)
    same_chunk = (ti // c_len) == (si // c_len)
    lane_head = lax.broadcasted_iota(jnp.int32, (gr, wk), 1) // dk
    row_chunk = lax.broadcasted_iota(jnp.int32, (gr, dv), 0) // c_len

    def group(gi, d):
        reverse = d == 1
        keep = same_chunk & ((si >= ti) if reverse else (si <= ti))
        sums = jnp.concatenate([keep.astype(BF16), same_chunk.astype(BF16)], axis=0)
        rows = pl.ds(pl.multiple_of(gi * gr, gr), gr)
        la_hi, la_lo = _split_bf16(la_scr[d, rows, :])
        res = _dot(sums, la_hi) + _dot(sums, la_lo)
        b, b_end = res[:gr], res[gr:]
        q = ub_ref[rows, 0:wk].astype(F32) * (dk ** -0.5)
        k = ub_ref[rows, wk:2 * wk].astype(F32)
        qt = q * jnp.exp(b)
        kt = (k * jnp.exp(-b)).astype(BF16)
        ke = (k * jnp.exp(b_end - b)).astype(BF16)
        dec = jnp.exp(b_end)
        order = range(ng - 1, -1, -1) if reverse else range(ng)
        for h in range(heads):
            vs = slice(h * dv, (h + 1) * dv)
            qm = jnp.where(lane_head == h, qt, 0.0).astype(BF16)
            a = jnp.where(keep, _dot_nt(qm, kt), 0.0).astype(BF16)
            v_h = ub_ref[rows, 2 * wk + h * dv:2 * wk + (h + 1) * dv]
            o = _dot(a, v_h)
            v_f = v_h.astype(F32)
            v_blk = jnp.concatenate([jnp.where(row_chunk == c, v_f, 0.0).astype(BF16) for c in range(ng)], axis=1)
            ut = _dot_tn(v_blk, ke)
            st = st_scr[d, h]
            inter = [None] * ng
            for c in order:
                inter[c] = _dot_nt(qm[c * c_len:(c + 1) * c_len], st.astype(BF16))
                st = st * dec[c * c_len:c * c_len + 1, :] + ut[c * dv:(c + 1) * dv]
            st_scr[d, h] = st
            o = o + jnp.concatenate(inter, axis=0)
            if reverse:
                o_scr[rows, vs] += o
            else:
                o_scr[rows, vs] = o

    n_groups = seq_len // gr
    lax.fori_loop(0, n_groups, lambda g, carry: (group(g, 0), carry)[1], 0)
    lax.fori_loop(0, n_groups, lambda g, carry: (group(n_groups - 1 - g, 1), carry)[1], 0)

    r = ub_ref[:, 2 * wk + wv:2 * wk + 2 * wv].astype(F32)
    gn = gn_ref[...]
    for h in range(heads):
        vs = slice(h * dv, (h + 1) * dv)
        o_ref[:, vs] = (_rmsnorm(o_scr[:, vs], gn[:, vs]) * _silu(r[:, vs])).astype(BF16)
    if out_state:
        for d in range(2):
            for h in range(heads):
                st_ref[0, 0, d, h] = st_scr[d, h].T[h * dk:(h + 1) * dk, :]


def _gla(u, z, gw, gb, gn, s0, st_buf, l, *, row0, nb, seq_len, bcol, heads, dk, dv):
    wk, wv = heads * dk, heads * dv
    wb = 2 * wk + 2 * wv
    has_s0 = s0 is not None
    out_state = st_buf is not None
    rb = row0 // seq_len
    kern = functools.partial(_gla_kernel, heads=heads, dk=dk, dv=dv, seq_len=seq_len, has_s0=has_s0,
                             out_state=out_state)
    in_specs = [
        pl.BlockSpec((seq_len, wb), lambda b: (rb + b, bcol)),
        pl.BlockSpec((seq_len, LANES), lambda b: (rb + b, 0)),
        pl.BlockSpec((2, LANES, wk), lambda b: (0, 0, 0)),
        pl.BlockSpec((2, 1, wk), lambda b: (0, 0, 0)),
        pl.BlockSpec((1, wv), lambda b: (0, 0)),
    ]
    args = [u, z, gw, gb, gn]
    if has_s0:
        in_specs.append(pl.BlockSpec((1, 1, 2, heads, dk, dv), lambda b: (b, l, 0, 0, 0, 0)))
        args.append(s0)
    out_specs = [pl.BlockSpec((seq_len, wv), lambda b: (b, 0))]
    out_shape = [jax.ShapeDtypeStruct((nb * seq_len, wv), BF16)]
    aliases = {}
    if out_state:
        aliases = {len(args): 1}
        in_specs.append(pl.BlockSpec(memory_space=pl.ANY))
        args.append(st_buf)
        out_specs.append(pl.BlockSpec((1, 1, 2, heads, dk, dv), lambda b: (b, l, 0, 0, 0, 0)))
        out_shape.append(jax.ShapeDtypeStruct(st_buf.shape, st_buf.dtype))
    res = pl.pallas_call(
        kern,
        grid=(nb,),
        in_specs=in_specs,
        out_specs=out_specs,
        out_shape=out_shape,
        scratch_shapes=[
            pltpu.VMEM((2, seq_len, wk), F32),
            pltpu.VMEM((seq_len, wv), F32),
            pltpu.VMEM((2, heads, dv, wk), F32),
        ],
        input_output_aliases=aliases,
        compiler_params=_cparams(1),
        name="gla_lat" if has_s0 else "gla_ctx",
    )(*args)
    return res


ROUTE_I1, ROUTE_I2, ROUTE_W1, ROUTE_W2, ROUTE_R1, ROUTE_R2 = range(6)


def _pack_bf16_pairs(h):
    c = h.shape[1] // 2
    lo = lax.bitcast_convert_type(h[:, :c].astype(BF16).astype(F32), jnp.uint32)
    hi = lax.bitcast_convert_type(h[:, c:].astype(BF16).astype(F32), jnp.uint32)
    return hi | (lo >> 16)


def _unpack_bf16_pairs(w):
    lo = lax.bitcast_convert_type(w << 16, F32).astype(BF16)
    hi = lax.bitcast_convert_type(w & jnp.uint32(0xFFFF0000), F32).astype(BF16)
    return jnp.concatenate([lo, hi], axis=-1)


def _outproj_kernel(*refs, n_experts, top_k, nct, split_x):
    refs = list(refs)
    x_ref = refs.pop(0)
    xl_ref = refs.pop(0) if split_x else None
    ctx_refs, lat_refs, refs = refs[0:3], refs[3:6], refs[6:]
    if n_experts:
        w_ref, mod_ref, g_ref, wr_ref, br_ref, xn_ref, h_ref, route_ref, counts_ref, run_scr = refs
    else:
        w_ref, mod_ref, g_ref, xn_ref, h_ref = refs
    is_ctx = pl.program_id(0) < nct
    mixed = jnp.concatenate([jnp.where(is_ctx, c[...], t[...]) for c, t in zip(ctx_refs, lat_refs)], axis=-1)
    x = jnp.where(is_ctx, x_ref[...], xl_ref[...]) if split_x else x_ref[...]
    xn = x + mod_ref[0, 2:3, :] * _dot(mixed, w_ref[...])
    xn_ref[...] = xn
    h = _rmsnorm(xn, g_ref[...]) * (1.0 + mod_ref[0, 4:5, :]) + mod_ref[0, 3:4, :]
    if not n_experts:
        h_ref[...] = h.astype(BF16)
        return
    assert top_k == 2
    h_ref[...] = _pack_bf16_pairs(h)
    h_hi = h.astype(BF16)
    h_lo = (h - h_hi.astype(F32)).astype(BF16)
    wr = wr_ref[...]
    w_hi = wr.astype(BF16)
    w_lo = (wr - w_hi.astype(F32)).astype(BF16)
    logits = _dot(h_hi, w_hi) + (_dot(h_hi, w_lo) + _dot(h_lo, w_hi)) + br_ref[...]
    tm = logits.shape[0]
    lane = lax.broadcasted_iota(jnp.int32, logits.shape, 1)
    l1 = jnp.max(logits, axis=-1, keepdims=True)
    i1 = jnp.min(jnp.where(logits == l1, lane, LANES), axis=-1, keepdims=True)
    rest = jnp.where(lane == i1, -jnp.inf, logits)
    l2 = jnp.max(rest, axis=-1, keepdims=True)
    i2 = jnp.min(jnp.where(rest == l2, lane, LANES), axis=-1, keepdims=True)
    e2 = jnp.exp(l2 - l1)
    w1 = 1.0 / (1.0 + e2)
    w2 = e2 * w1
    @pl.when(pl.program_id(0) == 0)
    def _():
        run_scr[...] = jnp.zeros(run_scr.shape, F32)

    oh1 = lane == i1
    oh2 = lane == i2
    earlier = (lax.broadcasted_iota(jnp.int32, (tm, tm), 1) < lax.broadcasted_iota(jnp.int32, (tm, tm), 0)).astype(BF16)
    c1 = _dot(earlier, oh1.astype(BF16))
    c2 = _dot(earlier, oh2.astype(BF16))
    n1 = jnp.sum(oh1.astype(F32), axis=0, keepdims=True)
    n2 = jnp.sum(oh2.astype(F32), axis=0, keepdims=True)
    run = run_scr[0:1, :]
    r1 = jnp.sum(jnp.where(oh1, run + c1, 0.0), axis=-1, keepdims=True)
    r2 = jnp.sum(jnp.where(oh2, run + n1 + c2, 0.0), axis=-1, keepdims=True)
    total = run + n1 + n2
    run_scr[...] = jnp.broadcast_to(total, run_scr.shape)
    counts_ref[...] = jnp.broadcast_to(total, counts_ref.shape)
    rec = jnp.zeros(logits.shape, F32)
    for slot, val in ((ROUTE_I1, i1.astype(F32)), (ROUTE_I2, i2.astype(F32)), (ROUTE_W1, w1), (ROUTE_W2, w2),
                      (ROUTE_R1, r1), (ROUTE_R2, r2)):
        rec = jnp.where(lane == slot, val, rec)
    route_ref[...] = rec


def _outproj(x, x_lat, mixed_ctx, mixed_lat, w_out, l, mod_l, g, router, *, n_ctx, t_lat, tm=512):
    split_x = x_lat is not None
    d = x.shape[1]
    n = x.shape[0] + (x_lat.shape[0] if split_x else 0)
    nct = n_ctx // tm
    nlt = n // tm - nct

    def cond_idx(i):
        return jnp.where(i < nct, 0, 1 + ((i - nct) * tm) // t_lat)

    row = lambda i: (i, 0)
    const = lambda i: (0, 0)
    ctx_row = lambda i: (jnp.minimum(i, nct - 1), 0)
    lat_row = lambda i: (jnp.clip(i - nct, 0, nlt - 1), 0)
    x_specs = [pl.BlockSpec((tm, d), ctx_row), pl.BlockSpec((tm, d), lat_row)] if split_x else [pl.BlockSpec((tm, d), row)]
    x_args = [x, x_lat] if split_x else [x]
    in_specs = (
        x_specs
        + [pl.BlockSpec((tm, o.shape[1]), ctx_row) for o in mixed_ctx]
        + [pl.BlockSpec((tm, o.shape[1]), lat_row) for o in mixed_lat]
        + [pl.BlockSpec((None, d, d), lambda i: (l, 0, 0)), pl.BlockSpec((1, 6, d), lambda i: (cond_idx(i), 0, 0)),
           pl.BlockSpec((1, d), const)]
    )
    args = [*x_args, *mixed_ctx, *mixed_lat, w_out, mod_l, g]
    n_experts = 0
    scratch = []
    if router is None:
        out_specs = [pl.BlockSpec((tm, d), row), pl.BlockSpec((tm, d), row)]
        out_shape = [jax.ShapeDtypeStruct((n, d), F32), jax.ShapeDtypeStruct((n, d), BF16)]
    else:
        wr, br, n_experts = router
        in_specs += [pl.BlockSpec((d, LANES), const), pl.BlockSpec((1, LANES), const)]
        args += [wr, br]
        out_specs = [pl.BlockSpec((tm, d), row), pl.BlockSpec((tm, d // 2), row), pl.BlockSpec((tm, LANES), row),
                     pl.BlockSpec((8, LANES), const)]
        out_shape = [jax.ShapeDtypeStruct((n, d), F32), jax.ShapeDtypeStruct((n, d // 2), jnp.uint32),
                     jax.ShapeDtypeStruct((n, LANES), F32), jax.ShapeDtypeStruct((8, LANES), F32)]
        scratch = [pltpu.VMEM((8, LANES), F32)]
    kern = functools.partial(_outproj_kernel, n_experts=n_experts, top_k=2, nct=nct, split_x=split_x)
    return pl.pallas_call(
        kern,
        grid=(n // tm,),
        in_specs=in_specs,
        out_specs=out_specs,
        out_shape=out_shape,
        scratch_shapes=scratch,
        compiler_params=_cparams(1),
        name="outproj_router" if n_experts else "outproj",
    )(*args)


def _row_copy(src_ref, src_row, dst_ref, dst_row, sem):
    return pltpu.make_async_copy(src_ref.at[pl.ds(src_row, 1), :], dst_ref.at[pl.ds(dst_row, 1), :], sem)


def _dispatch_kernel(pos_ref, hp_ref, hs_in_ref, hs_ref, sem, *, n, tm):
    del hs_in_ref
    base = pl.program_id(0) * tm

    def copies(k):
        return (_row_copy(hp_ref, k, hs_ref, pos_ref[base + k], sem),
                _row_copy(hp_ref, k, hs_ref, pos_ref[n + base + k], sem))

    def start(k, carry):
        for queue, cp in enumerate(copies(k)):
            cp.start(priority=queue)
        return carry

    def wait(k, carry):
        for cp in copies(k):
            cp.wait()
        return carry

    lax.fori_loop(0, tm, start, 0, unroll=8)
    lax.fori_loop(0, tm, wait, 0, unroll=8)


def _dispatch(pos, hp, n_rows_sorted, tm=512):
    n, c = hp.shape
    hs0 = jnp.zeros((n_rows_sorted, c), hp.dtype)
    return pl.pallas_call(
        functools.partial(_dispatch_kernel, n=n, tm=tm),
        grid_spec=pltpu.PrefetchScalarGridSpec(
            num_scalar_prefetch=1,
            grid=(n // tm,),
            in_specs=[pl.BlockSpec((tm, c), lambda i, pos: (i, 0)), pl.BlockSpec(memory_space=pl.ANY)],
            out_specs=pl.BlockSpec(memory_space=pl.ANY),
            scratch_shapes=[pltpu.SemaphoreType.DMA(())],
        ),
        out_shape=jax.ShapeDtypeStruct((n_rows_sorted, c), hp.dtype),
        input_output_aliases={2: 0},
        compiler_params=_cparams(1),
        name="moe_dispatch",
    )(pos, hp, hs0)


def _moe_ffn_kernel(texp_ref, trow_ref, nact_ref, hs_ref, wg_ref, wu_ref, wd_ref, ys_ref, h_scr):
    del texp_ref, trow_ref
    i = pl.program_id(0)
    f = pl.program_id(1)
    active = i < nact_ref[0]

    @pl.when(active & (f == 0))
    def _():
        h_scr[...] = _unpack_bf16_pairs(hs_ref[...])

    def contribution():
        h = h_scr[...]
        t = _silu(_dot(h, wg_ref[...])) * _dot(h, wu_ref[...])
        return _dot(t.astype(BF16), wd_ref[...])

    @pl.when(active & (f == 0))
    def _():
        ys_ref[...] = contribution()

    @pl.when(active & (f > 0))
    def _():
        ys_ref[...] += contribution()

    @pl.when(jnp.logical_not(active) & (f == 0))
    def _():
        ys_ref[...] = jnp.zeros(ys_ref.shape, F32)


def _moe_ffn(tile_exp, tile_row, n_active, hs, wg, wu, wd, li, *, tm, nf=2):
    r, c = hs.shape
    d, ff = wg.shape[2:]
    tf = ff // nf

    def f_eff(i, f, nact):
        return jnp.where(i < nact[0], f, nf - 1)

    return pl.pallas_call(
        _moe_ffn_kernel,
        grid_spec=pltpu.PrefetchScalarGridSpec(
            num_scalar_prefetch=3,
            grid=(r // tm, nf),
            in_specs=[
                pl.BlockSpec((tm, c), lambda i, f, te, tr, na: (tr[i], 0)),
                pl.BlockSpec((None, None, d, tf), lambda i, f, te, tr, na: (li, te[i], 0, f_eff(i, f, na))),
                pl.BlockSpec((None, None, d, tf), lambda i, f, te, tr, na: (li, te[i], 0, f_eff(i, f, na))),
                pl.BlockSpec((None, None, tf, d), lambda i, f, te, tr, na: (li, te[i], f_eff(i, f, na), 0)),
            ],
            out_specs=pl.BlockSpec((tm, d), lambda i, f, te, tr, na: (i, 0)),
            scratch_shapes=[pltpu.VMEM((tm, d), BF16)],
        ),
        out_shape=jax.ShapeDtypeStruct((r, d), F32),
        compiler_params=_cparams(2),
        name="moe_ffn",
    )(tile_exp, tile_row, n_active, hs, wg, wu, wd)


def _combine_kernel(*refs, n, tm, final_norm, nct):
    refs = list(refs)
    pos_ref, ys_ref, x_ref, route_ref, mod_ref = refs[:5]
    refs = refs[5:]
    fg_ref = refs.pop(0) if final_norm else None
    o_ref = refs.pop(0)
    ol_ref = refs.pop(0) if final_norm else None
    buf, sem = refs
    base = pl.program_id(0) * tm

    def copies(k):
        return (_row_copy(ys_ref, pos_ref[base + k], buf.at[0], k, sem),
                _row_copy(ys_ref, pos_ref[n + base + k], buf.at[1], k, sem))

    def start(k, carry):
        for queue, cp in enumerate(copies(k)):
            cp.start(priority=queue)
        return carry

    def wait(k, carry):
        for cp in copies(k):
            cp.wait()
        return carry

    lax.fori_loop(0, tm, start, 0, unroll=8)
    lax.fori_loop(0, tm, wait, 0, unroll=8)
    route = route_ref[...]
    lane = lax.broadcasted_iota(jnp.int32, route.shape, 1)
    w1 = jnp.sum(jnp.where(lane == ROUTE_W1, route, 0.0), axis=-1, keepdims=True)
    w2 = jnp.sum(jnp.where(lane == ROUTE_W2, route, 0.0), axis=-1, keepdims=True)
    y = x_ref[...] + mod_ref[0, 5:6, :] * (w1 * buf[0] + w2 * buf[1])
    if not final_norm:
        o_ref[...] = y
        return
    y = _rmsnorm(y, fg_ref[...])
    is_ctx = pl.program_id(0) < nct

    @pl.when(is_ctx)
    def _():
        o_ref[...] = y

    @pl.when(jnp.logical_not(is_ctx))
    def _():
        ol_ref[...] = y


def _combine(pos, ys, x, route, mod_l, final_g, *, n_ctx, t_lat, tm=512):
    n, d = x.shape
    nct = n_ctx // tm
    nlt = n // tm - nct

    def cond_idx(i):
        return jnp.where(i < nct, 0, 1 + ((i - nct) * tm) // t_lat)

    in_specs = [
        pl.BlockSpec(memory_space=pl.ANY),
        pl.BlockSpec((tm, d), lambda i, pos: (i, 0)),
        pl.BlockSpec((tm, LANES), lambda i, pos: (i, 0)),
        pl.BlockSpec((1, 6, d), lambda i, pos: (cond_idx(i), 0, 0)),
    ]
    args = [ys, x, route, mod_l]
    if final_g is None:
        out_specs = pl.BlockSpec((tm, d), lambda i, pos: (i, 0))
        out_shape = jax.ShapeDtypeStruct((n, d), F32)
    else:
        in_specs.append(pl.BlockSpec((1, d), lambda i, pos: (0, 0)))
        args.append(final_g)
        out_specs = [pl.BlockSpec((tm, d), lambda i, pos: (jnp.minimum(i, nct - 1), 0)),
                     pl.BlockSpec((tm, d), lambda i, pos: (jnp.clip(i - nct, 0, nlt - 1), 0))]
        out_shape = [jax.ShapeDtypeStruct((n_ctx, d), F32), jax.ShapeDtypeStruct((n - n_ctx, d), F32)]
    return pl.pallas_call(
        functools.partial(_combine_kernel, n=n, tm=tm, final_norm=final_g is not None, nct=nct),
        grid_spec=pltpu.PrefetchScalarGridSpec(
            num_scalar_prefetch=1,
            grid=(n // tm,),
            in_specs=in_specs,
            out_specs=out_specs,
            scratch_shapes=[pltpu.VMEM((2, tm, d), F32), pltpu.SemaphoreType.DMA(())],
        ),
        out_shape=out_shape,
        compiler_params=_cparams(1),
        name="moe_combine",
    )(pos, *args)


def _routing_tables(route, counts, n_experts, tm):
    n = route.shape[0]
    n_tiles = 2 * n // tm + n_experts
    cnt = counts[0, :n_experts].astype(jnp.int32)
    tiles_e = (cnt + tm - 1) // tm
    tile_end = jnp.cumsum(tiles_e)
    offs = (tile_end - tiles_e) * tm
    i1 = route[:, ROUTE_I1].astype(jnp.int32)
    i2 = route[:, ROUTE_I2].astype(jnp.int32)
    eidx = jnp.arange(n_experts, dtype=jnp.int32)[None, :]
    off1 = jnp.sum(jnp.where(i1[:, None] == eidx, offs[None, :], 0), axis=1)
    off2 = jnp.sum(jnp.where(i2[:, None] == eidx, offs[None, :], 0), axis=1)
    pos = jnp.concatenate([off1 + route[:, ROUTE_R1].astype(jnp.int32), off2 + route[:, ROUTE_R2].astype(jnp.int32)])
    n_active = tile_end[-1]
    t = jnp.arange(n_tiles, dtype=jnp.int32)
    t_eff = jnp.minimum(t, n_active - 1)
    tile_exp = jnp.sum((t_eff[:, None] >= tile_end[None, :]).astype(jnp.int32), axis=1)
    return pos, tile_exp.astype(jnp.int32), t_eff.astype(jnp.int32), n_active.reshape(1).astype(jnp.int32), n_tiles


def _ffn_kernel(*refs, final_norm):
    refs = list(refs)
    h_ref, x_ref, wg_ref, wu_ref, wd_ref, mod_ref = refs[:6]
    refs = refs[6:]
    fg_ref = refs.pop(0) if final_norm else None
    (o_ref,) = refs
    f = pl.program_id(1)

    @pl.when(f == 0)
    def _():
        o_ref[...] = jnp.zeros(o_ref.shape, F32)

    h = h_ref[...]
    t = _silu(_dot(h, wg_ref[...])) * _dot(h, wu_ref[...])
    o_ref[...] += _dot(t.astype(BF16), wd_ref[...])

    @pl.when(f == pl.num_programs(1) - 1)
    def _():
        y = x_ref[...] + mod_ref[0, 5:6, :] * o_ref[...]
        if final_norm:
            y = _rmsnorm(y, fg_ref[...])
        o_ref[...] = y


def _ffn(h, x, wg, wu, wd, li, mod_l, final_g, *, n_ctx, t_lat, tm=512):
    n, d = x.shape
    tf = FFN_TF
    nf = wg.shape[2] // tf
    nct = n_ctx // tm

    def cond_idx(i):
        return jnp.where(i < nct, 0, 1 + ((i - nct) * tm) // t_lat)

    row = lambda i, f: (i, 0)
    in_specs = [
        pl.BlockSpec((tm, d), row),
        pl.BlockSpec((tm, d), row),
        pl.BlockSpec((None, d, tf), lambda i, f: (li, 0, f)),
        pl.BlockSpec((None, d, tf), lambda i, f: (li, 0, f)),
        pl.BlockSpec((None, tf, d), lambda i, f: (li, f, 0)),
        pl.BlockSpec((1, 6, d), lambda i, f: (cond_idx(i), 0, 0)),
    ]
    args = [h, x, wg, wu, wd, mod_l]
    if final_g is not None:
        in_specs.append(pl.BlockSpec((1, d), lambda i, f: (0, 0)))
        args.append(final_g)
    return pl.pallas_call(
        functools.partial(_ffn_kernel, final_norm=final_g is not None),
        grid=(n // tm, nf),
        in_specs=in_specs,
        out_specs=pl.BlockSpec((tm, d), row),
        out_shape=jax.ShapeDtypeStruct((n, d), F32),
        compiler_params=_cparams(2),
        name="ffn_dense",
    )(*args)


def kernel(x_prompt, x_sample, cache_nat_k, cache_nat_v, state_gla, cache_swa_k, cache_swa_v, c, c_ctx, w_ada, b_ada,
           norm_mix_g, norm_ffn_g, w_in, na_rpb, gla_w_gate, gla_b_gate, gla_norm_g, swa_sink, w_out, ffn_w_gate,
           ffn_w_up, ffn_w_down, moe_w_router, moe_b_router, moe_w_gate, moe_w_up, moe_w_down, final_norm_g):
    nb, seq, d = x_prompt.shape
    db, t_lat, _ = x_sample.shape
    depth = w_in.shape[0]
    p_len, h_a = cache_nat_k.shape[2], cache_nat_k.shape[3]
    h_b, dk_b, dv_b = state_gla.shape[3:6]
    hkv_c, g_c = swa_sink.shape[1], swa_sink.shape[2]
    rank = gla_w_gate.shape[2]
    n_exp = moe_w_router.shape[-1]
    w_a = h_a * HEAD_DIM
    w_bk = h_b * dk_b
    w_bv = h_b * dv_b
    w_cq = hkv_c * g_c * HEAD_DIM
    w_ck = hkv_c * HEAD_DIM
    n_ctx = nb * seq
    n_lat = db * t_lat
    n = n_ctx + n_lat
    z0 = 3 * w_a + 2 * w_bk + 2 * w_bv
    z1 = z0 + 2 * rank
    a_w = 3 * w_a
    b_w = 2 * w_bk + 2 * w_bv
    assert a_w == b_w and (a_w + b_w) % w_cq == 0 and (a_w + b_w + w_cq) % w_ck == 0 and 2 * rank <= LANES
    assert db + 1 <= 8

    x, x_lat = x_prompt.reshape(n_ctx, d), x_sample.reshape(n_lat, d)
    cond8 = jnp.concatenate([c_ctx[None, :], c, jnp.zeros((7 - db, d), F32)], axis=0)
    mod = _adaln(cond8, w_ada, b_ada).reshape(depth, 8, 6, d)

    cos, sin = _rope_tables(t_lat, w_ck)
    na_tab = _na_bias_tables(na_rpb)
    kc_a = cache_nat_k.reshape(db, depth, p_len, w_a)
    vc_a = cache_nat_v.reshape(db, depth, p_len, w_a)
    kc_c = cache_swa_k.reshape(db, depth, p_len, w_ck)
    vc_c = cache_swa_v.reshape(db, depth, p_len, w_ck)
    sizes = dict(n_ctx=n_ctx, t_lat=t_lat)

    kv_bufs = [jnp.zeros((nb, depth, seq, w), F32) for w in (w_a, w_a, w_ck, w_ck)]
    gla_st = jnp.zeros((nb, depth, 2, h_b, dk_b, dv_b), F32)
    w_main = jnp.concatenate([w_in[:, :, :z0], w_in[:, :, z1:]], axis=2).astype(BF16)
    wz = jnp.pad(w_in[:, :, z0:z1], ((0, 0), (0, 0), (0, LANES - 2 * rank))).astype(BF16)
    w_out_b = w_out.astype(BF16)
    ffn_wg, ffn_wu, ffn_wd = ffn_w_gate.astype(BF16), ffn_w_up.astype(BF16), ffn_w_down.astype(BF16)
    moe_wg, moe_wu, moe_wd = moe_w_gate.astype(BF16), moe_w_up.astype(BF16), moe_w_down.astype(BF16)
    for l in range(depth):
        u, z, *kv_bufs = _inproj(x, x_lat, mod[l], norm_mix_g[l][None, :], w_main, wz, kv_bufs, l, seq=seq, w_a=w_a,
                                 w_cq=w_cq, w_kvc=w_ck, **sizes)

        oa_c = _ctx_attn(u, None, nb=nb, seq=seq, qcol=0, kcol=1, vcol=2, wq=w_a, wk=w_a, hkv=h_a, g=1)
        oa_l = _na_attn(u, kc_a, vc_a, na_tab, l, n_ctx=n_ctx, db=db, t_lat=t_lat, w_a=w_a, heads=h_a)
        gw = jnp.zeros((2, LANES, w_bk), F32)
        gw = gw.at[0, 0:rank].set(gla_w_gate[l, 0]).at[1, rank:2 * rank].set(gla_w_gate[l, 1])
        gb = gla_b_gate[l][:, None, :]
        gn = gla_norm_g[l].reshape(1, w_bv)
        gla_args = dict(bcol=a_w // b_w, heads=h_b, dk=dk_b, dv=dv_b)
        ob_c, gla_st = _gla(u, z, gw, gb, gn, None, gla_st, l, row0=0, nb=nb, seq_len=seq, **gla_args)
        (ob_l,) = _gla(u, z, gw, gb, gn, state_gla, None, l, row0=n_ctx, nb=db, seq_len=t_lat, **gla_args)
        c0 = a_w + b_w
        cols = dict(qcol=c0 // w_cq, kcol=(c0 + w_cq) // w_ck, vcol=(c0 + w_cq) // w_ck + 1, hkv=hkv_c, g=g_c)
        oc_c = _ctx_attn(u, swa_sink[l], nb=nb, seq=seq, wq=w_cq, wk=w_ck, **cols)
        oc_l = _swa_attn(u, swa_sink[l], kc_c, vc_c, cos, sin, l, n_ctx=n_ctx, db=db, t_lat=t_lat, **cols)

        moe = l % 2 == 1
        i = l // 2
        router = None
        if moe:
            wr = jnp.pad(moe_w_router[i], ((0, 0), (0, LANES - n_exp)))
            br = jnp.concatenate([moe_b_router[i], jnp.full((LANES - n_exp,), NEG_INF, F32)])[None, :]
            router = (wr, br, n_exp)
        res = _outproj(x, x_lat, (oa_c, ob_c, oc_c), (oa_l, ob_l, oc_l), w_out_b, l, mod[l],
                       norm_ffn_g[l][None, :], router, **sizes)
        x_lat = None
        fg = final_norm_g[None, :] if l == depth - 1 else None
        if moe:
            xn, hp, route, counts = res
            tm_e = 512
            pos, tile_exp, tile_row, n_active, n_tiles = _routing_tables(route, counts, n_exp, tm_e)
            hs = _dispatch(pos, hp, n_tiles * tm_e)
            ys = _moe_ffn(tile_exp, tile_row, n_active, hs, moe_wg, moe_wu, moe_wd, i, tm=tm_e)
            x = _combine(pos, ys, xn, route, mod[l], fg, **sizes)
        else:
            xn, h2 = res
            x = _ffn(h2, xn, ffn_wg, ffn_wu, ffn_wd, i, mod[l], fg, **sizes)

    y_ctx, y_lat = x if isinstance(x, (list, tuple)) else (x[:n_ctx], x[n_ctx:])
    nat_k, nat_v, swa_k, swa_v = kv_bufs
    return (y_ctx.reshape(nb, seq, d), y_lat.reshape(db, t_lat, d),
            nat_k.reshape(nb, depth, seq, h_a, HEAD_DIM), nat_v.reshape(nb, depth, seq, h_a, HEAD_DIM), gla_st,
            swa_k.reshape(nb, depth, seq, hkv_c, HEAD_DIM), swa_v.reshape(nb, depth, seq, hkv_c, HEAD_DIM))
```

```python
import functools

import numpy as np
import jax
import jax.numpy as jnp
from jax import lax
from jax.experimental import pallas as pl
from jax.experimental.pallas import tpu as pltpu

F32 = jnp.float32
BF16 = jnp.bfloat16
HIGHEST = lax.Precision.HIGHEST

EPS = 1e-6
NEG_INF = -1e30
GRID_W = 64
HEAD_DIM = 64
NA_ROWS = 8
NA_COLS = 16
SWA_WIN = 128
SWA_QBLOCK = 128
GLA_CHUNK = 64
GLA_GROUP = 4
GLA_TAU = 16.0
ROPE_BASE = 10000.0
LANES = 128
FFN_TF = 512
CTX_HEAD_BATCH = 16
NA_HEAD_BATCH = 4
SWA_HEAD_BATCH = 16
VMEM_LIMIT = 56 * 1024 * 1024


def _cparams(n_axes):
    return pltpu.CompilerParams(dimension_semantics=("arbitrary",) * n_axes, vmem_limit_bytes=VMEM_LIMIT)


def _dot(a, b, **kw):
    return jnp.dot(a, b, preferred_element_type=F32, **kw)


def _dot_nt(a, b, **kw):
    return lax.dot_general(a, b, (((1,), (1,)), ((), ())), preferred_element_type=F32, **kw)


def _dot_tn(a, b, **kw):
    return lax.dot_general(a, b, (((0,), (0,)), ((), ())), preferred_element_type=F32, **kw)


def _silu(x):
    return x / (1.0 + jnp.exp(-x))


def _split_bf16(x):
    hi = x.astype(BF16)
    return hi, (x - hi.astype(F32)).astype(BF16)


def _rmsnorm(x, g):
    return x * lax.rsqrt(jnp.mean(x * x, axis=-1, keepdims=True) + EPS) * g


def _adaln_kernel(cond_ref, w_ref, b_ref, o_ref):
    s = _silu(cond_ref[...]).astype(BF16)
    o_ref[0] = _dot(s, w_ref[0].astype(BF16)) + b_ref[0]


def _adaln(cond8, w_ada, b_ada, tn=1024):
    depth, d, n6 = w_ada.shape
    return pl.pallas_call(
        _adaln_kernel,
        grid=(depth, n6 // tn),
        in_specs=[
            pl.BlockSpec((8, d), lambda l, j: (0, 0)),
            pl.BlockSpec((1, d, tn), lambda l, j: (l, 0, j)),
            pl.BlockSpec((1, 1, tn), lambda l, j: (l, 0, j)),
        ],
        out_specs=pl.BlockSpec((1, 8, tn), lambda l, j: (l, 0, j)),
        out_shape=jax.ShapeDtypeStruct((depth, 8, n6), F32),
        compiler_params=_cparams(2),
        name="adaln",
    )(cond8, w_ada, b_ada.reshape(depth, 1, n6))


def _inproj_kernel(*refs, nct, seq, w_a, w_cq, w_kvc, split_x):
    refs = list(refs)
    x_ref = refs.pop(0)
    xl_ref = refs.pop(0) if split_x else None
    mod_ref, g_ref, w_ref, wz_ref = refs[:4]
    u_ref, z_ref, nk_ref, nv_ref, sk_ref, sv_ref, h_scr = refs[8:]
    i = pl.program_id(0)
    j = pl.program_id(1)
    is_ctx = i < nct

    @pl.when(j == 0)
    def _():
        x = jnp.where(is_ctx, x_ref[...], xl_ref[...]) if split_x else x_ref[...]
        h = _rmsnorm(x, g_ref[...]) * (1.0 + mod_ref[0, 1:2, :]) + mod_ref[0, 0:1, :]
        hb = h.astype(BF16)
        h_scr[...] = hb
        z_ref[...] = _dot(hb, wz_ref[...])

    acc = _dot(h_scr[...], w_ref[...])
    u_ref[...] = acc.astype(BF16)
    nseq = nk_ref.shape[0]

    @pl.when(is_ctx & (j == 0))
    def _():
        for bb in range(nseq):
            nk_ref[bb, 0] = acc[bb * seq:(bb + 1) * seq, w_a:2 * w_a]
            nv_ref[bb, 0] = acc[bb * seq:(bb + 1) * seq, 2 * w_a:3 * w_a]

    @pl.when(is_ctx & (j == 2))
    def _():
        for bb in range(nseq):
            sk_ref[bb, 0] = acc[bb * seq:(bb + 1) * seq, w_cq:w_cq + w_kvc]
            sv_ref[bb, 0] = acc[bb * seq:(bb + 1) * seq, w_cq + w_kvc:w_cq + 2 * w_kvc]


def _inproj(x, x_lat, mod_l, g, w_main, wz, kv_bufs, l, *, n_ctx, seq, t_lat, w_a, w_cq, w_kvc, tm=512):
    split_x = x_lat is not None
    d = x.shape[1]
    n = x.shape[0] + (x_lat.shape[0] if split_x else 0)
    n_main = w_main.shape[2]
    tn = 3 * w_a
    nct = n_ctx // tm
    nlt = n // tm - nct
    nseq = tm // seq
    assert n_ctx % tm == 0 and tm % seq == 0 and t_lat % tm == 0 and n_main == 3 * tn and w_cq + 2 * w_kvc == tn

    def cond_idx(i):
        return jnp.where(i < nct, 0, 1 + ((i - nct) * tm) // t_lat)

    kv_idx = lambda i, j: (jnp.minimum(i, nct - 1), l, 0, 0)
    if split_x:
        x_specs = [pl.BlockSpec((tm, d), lambda i, j: (jnp.minimum(i, nct - 1), 0)),
                   pl.BlockSpec((tm, d), lambda i, j: (jnp.clip(i - nct, 0, nlt - 1), 0))]
        x_args = [x, x_lat]
    else:
        x_specs = [pl.BlockSpec((tm, d), lambda i, j: (i, 0))]
        x_args = [x]
    n_in = len(x_args) + 4
    kern = functools.partial(_inproj_kernel, nct=nct, seq=seq, w_a=w_a, w_cq=w_cq, w_kvc=w_kvc, split_x=split_x)
    return pl.pallas_call(
        kern,
        grid=(n // tm, n_main // tn),
        in_specs=x_specs + [
            pl.BlockSpec((1, 6, d), lambda i, j: (cond_idx(i), 0, 0)),
            pl.BlockSpec((1, d), lambda i, j: (0, 0)),
            pl.BlockSpec((None, d, tn), lambda i, j: (l, 0, j)),
            pl.BlockSpec((None, d, LANES), lambda i, j: (l, 0, 0)),
        ] + [pl.BlockSpec(memory_space=pl.ANY)] * 4,
        out_specs=[
            pl.BlockSpec((tm, tn), lambda i, j: (i, j)),
            pl.BlockSpec((tm, LANES), lambda i, j: (i, 0)),
        ] + [pl.BlockSpec((nseq, 1, seq, b.shape[-1]), kv_idx) for b in kv_bufs],
        out_shape=[
            jax.ShapeDtypeStruct((n, n_main), BF16),
            jax.ShapeDtypeStruct((n, LANES), F32),
        ] + [jax.ShapeDtypeStruct(b.shape, b.dtype) for b in kv_bufs],
        scratch_shapes=[pltpu.VMEM((tm, d), BF16)],
        input_output_aliases={n_in + k: 2 + k for k in range(4)},
        compiler_params=_cparams(2),
        name="inproj",
    )(*x_args, mod_l, g, w_main, wz, *kv_bufs)


def _swap_halves(x):
    return pltpu.roll(x.astype(F32), LANES // 2, 1).astype(x.dtype)


def _half_head_attention(heads, batch):
    if len(heads) > batch:
        return _half_head_attention(heads[:batch], batch) + _half_head_attention(heads[batch:], batch)
    scores =[[_dot_nt(qm, kp) if fn is None else fn(_dot_nt(qm, kp)) for kp, fn in zip(keys, fns)]
              for qm, keys, _, fns, _ in heads]
    maxes = []
    for ss, (_, _, _, _, sink) in zip(scores, heads):
        m = jnp.max(ss[0], axis=-1, keepdims=True)
        for s in ss[1:]:
            m = jnp.maximum(m, jnp.max(s, axis=-1, keepdims=True))
        maxes.append(m if sink is None else jnp.maximum(m, sink))
    probs = [[jnp.exp(s - m).astype(BF16) for s in ss] for ss, m in zip(scores, maxes)]
    outs = []
    for ps, m, (_, _, values_aug, _, sink) in zip(probs, maxes, heads):
        acc = _dot(ps[0], values_aug[0])
        for p, va in zip(ps[1:], values_aug[1:]):
            acc = acc + _dot(p, va)
        den = pltpu.roll(acc, LANES // 2, 1)
        if sink is not None:
            den = den + jnp.exp(sink - m)
        outs.append(acc / den)
    return outs


def _ctx_attn_kernel(*refs, hkv, g, has_sink):
    if has_sink:
        sink_ref, q_ref, k_ref, v_ref, o_ref = refs
    else:
        q_ref, k_ref, v_ref, o_ref = refs
    q = q_ref[...] * jnp.asarray(HEAD_DIM ** -0.5, BF16)
    k = k_ref[...]
    v = v_ref[...]
    rows = q.shape[0]
    low = lax.broadcasted_iota(jnp.int32, (rows, LANES), 1) < HEAD_DIM
    kv_pairs = {}

    def kv_pair(kh, half):
        if (kh, half) not in kv_pairs:
            sl = slice((kh // 2) * LANES, (kh // 2 + 1) * LANES)
            kp, vp = k[:, sl], v[:, sl]
            if kh % 2 != half:
                kp, vp = _swap_halves(kp), _swap_halves(vp)
            kv_pairs[(kh, half)] = (kp, vp)
        return kv_pairs[(kh, half)]

    heads = []
    for hq in range(hkv * g):
        pair, half = hq // 2, hq % 2
        kh, gi = hq // g, hq % g
        qp = q[:, pair * LANES:(pair + 1) * LANES]
        own = low if half == 0 else jnp.logical_not(low)
        kp, vp = kv_pair(kh, half)
        sk = sink_ref[kh, gi] if has_sink else None
        heads.append((jnp.where(own, qp, 0), [kp], [jnp.where(own, vp, 1)], [None], sk))
    res = _half_head_attention(heads, CTX_HEAD_BATCH)
    o_ref[...] = jnp.concatenate([jnp.where(low, res[2 * p], res[2 * p + 1]) for p in range(len(res) // 2)],
                                 axis=-1).astype(BF16)


def _ctx_attn(u, sink, *, nb, seq, qcol, kcol, vcol, wq, wk, hkv, g):
    has_sink = sink is not None
    kern = functools.partial(_ctx_attn_kernel, hkv=hkv, g=g, has_sink=has_sink)
    in_specs = [
        pl.BlockSpec((seq, wq), lambda b: (b, qcol)),
        pl.BlockSpec((seq, wk), lambda b: (b, kcol)),
        pl.BlockSpec((seq, wk), lambda b: (b, vcol)),
    ]
    args = [u, u, u]
    if has_sink:
        in_specs = [pl.BlockSpec(memory_space=pltpu.SMEM)] + in_specs
        args = [sink] + args
    return pl.pallas_call(
        kern,
        grid=(nb,),
        in_specs=in_specs,
        out_specs=pl.BlockSpec((seq, wq), lambda b: (b, 0)),
        out_shape=jax.ShapeDtypeStruct((nb * seq, wq), BF16),
        compiler_params=_cparams(1),
        name="ctx_attn_sink" if has_sink else "ctx_attn",
    )(*args)


NA_QROWS = 4
NA_WIN_ROWS = 12


def _na_kernel(q_ref, k_ref, v_ref, kc_ref, vc_ref, bias_ref, o_ref, *, heads, rows):
    qi = pl.program_id(1)
    r0 = qi * NA_QROWS
    w0 = jnp.clip(r0 - NA_ROWS // 2, 0, rows - NA_WIN_ROWS)
    k0 = pl.multiple_of(w0 * GRID_W, NA_QROWS * GRID_W)
    tq = NA_QROWS * GRID_W
    tk = NA_WIN_ROWS * GRID_W
    q = q_ref[...] * jnp.asarray(HEAD_DIM ** -0.5, BF16)
    k = k_ref[pl.ds(k0, tk), :]
    v = v_ref[pl.ds(k0, tk), :]
    kc = kc_ref[0, 0].astype(BF16)
    vc = vc_ref[0, 0].astype(BF16)
    qrow = r0 + lax.broadcasted_iota(jnp.int32, (tq, tk), 0) // GRID_W
    krow = w0 + lax.broadcasted_iota(jnp.int32, (tq, tk), 1) // GRID_W
    band0 = jnp.clip(qrow - NA_ROWS // 2, 0, rows - NA_ROWS)
    in_band = (krow >= band0) & (krow < band0 + NA_ROWS)
    low_q = lax.broadcasted_iota(jnp.int32, (tq, LANES), 1) < HEAD_DIM
    low_k = lax.broadcasted_iota(jnp.int32, (tk, LANES), 1) < HEAD_DIM
    low_c = lax.broadcasted_iota(jnp.int32, (kc.shape[0], LANES), 1) < HEAD_DIM

    def window_scores(h):
        def fn(s):
            bias = jnp.concatenate(
                [jnp.concatenate([bias_ref[h, w0 + 2 * p - (r0 + rq) + 2 * NA_ROWS - 1]
                                  for p in range(NA_WIN_ROWS // 2)], axis=1) for rq in range(NA_QROWS)], axis=0)
            return jnp.where(in_band, s + bias, NEG_INF)
        return fn

    head_specs = []
    for h in range(heads):
        sl = slice((h // 2) * LANES, (h // 2 + 1) * LANES)
        first = h % 2 == 0
        own = lambda low: low if first else jnp.logical_not(low)
        head_specs.append((jnp.where(own(low_q), q[:, sl], 0), [k[:, sl], kc[:, sl]],
                           [jnp.where(own(low_k), v[:, sl], 1), jnp.where(own(low_c), vc[:, sl], 1)],
                           [window_scores(h), None], None))
    res = _half_head_attention(head_specs, NA_HEAD_BATCH)
    o_ref[...] = jnp.concatenate([jnp.where(low_q, res[2 * p], res[2 * p + 1]) for p in range(heads // 2)],
                                 axis=-1).astype(BF16)


def _na_attn(u, kc, vc, bias, l, *, n_ctx, db, t_lat, w_a, heads):
    tq = NA_QROWS * GRID_W
    rows = t_lat // GRID_W
    nq = t_lat // tq
    p = kc.shape[2]
    assert rows >= NA_WIN_ROWS and NA_WIN_ROWS >= NA_QROWS + NA_ROWS - 1 and (rows - NA_WIN_ROWS) % NA_QROWS == 0
    kern = functools.partial(_na_kernel, heads=heads, rows=rows)
    return pl.pallas_call(
        kern,
        grid=(db, nq),
        in_specs=[
            pl.BlockSpec((tq, w_a), lambda b, i: (n_ctx // tq + b * nq + i, 0)),
            pl.BlockSpec((t_lat, w_a), lambda b, i: (n_ctx // t_lat + b, 1)),
            pl.BlockSpec((t_lat, w_a), lambda b, i: (n_ctx // t_lat + b, 2)),
            pl.BlockSpec((1, 1, p, w_a), lambda b, i: (b, l, 0, 0)),
            pl.BlockSpec((1, 1, p, w_a), lambda b, i: (b, l, 0, 0)),
            pl.BlockSpec((None,) + bias.shape[1:], lambda b, i: (l, 0, 0, 0, 0)),
        ],
        out_specs=pl.BlockSpec((tq, w_a), lambda b, i: (b * nq + i, 0)),
        out_shape=jax.ShapeDtypeStruct((db * t_lat, w_a), BF16),
        compiler_params=_cparams(2),
        name="na_attn",
    )(u, u, u, kc, vc, bias)


def _na_bias_tables(rpb):
    col = np.arange(GRID_W)
    col_start = np.clip(col - NA_COLS // 2, 0, GRID_W - NA_COLS)
    col_ok = (col[None, :] >= col_start[:, None]) & (col[None, :] < col_start[:, None] + NA_COLS)
    dc = np.clip(col[None, :] - col[:, None] + NA_COLS - 1, 0, 2 * NA_COLS - 2)
    pick = (np.arange(2 * NA_COLS - 1)[:, None, None] == dc[None]).astype(np.float32)
    t1 = jnp.einsum("lhdj,jck->lhdck", rpb.astype(F32), jnp.asarray(pick), precision=HIGHEST)
    t1 = jnp.where(jnp.asarray(col_ok), t1, NEG_INF)
    t1 = jnp.pad(t1, ((0, 0), (0, 0), (NA_ROWS, NA_ROWS), (0, 0), (0, 0)), constant_values=NEG_INF)
    return jnp.concatenate([t1[:, :, :-1], t1[:, :, 1:]], axis=-1)


def _rope(x, cos, sin_signed):
    w = x.shape[-1]
    q4 = HEAD_DIM // 4
    lane = lax.broadcasted_iota(jnp.int32, x.shape, 1)
    first = (lane % (2 * q4)) < q4
    swapped = jnp.where(first, pltpu.roll(x, w - q4, 1), pltpu.roll(x, q4, 1))
    return x * cos + swapped * sin_signed


def _swa_kernel(sink_ref, q_ref, k_ref, v_ref, kc_ref, vc_ref, cos_ref, sin_ref, o_ref,
                ks_scr, vs_scr, kcs_scr, vcs_scr, *, hkv, g, t_lat):
    n = pl.program_id(1)
    qb = SWA_QBLOCK
    wk = hkv * HEAD_DIM
    span = 3 * qb

    @pl.when(n == 0)
    def _():
        k_r = _rope(k_ref[...].astype(F32), cos_ref[...], sin_ref[...]).astype(BF16)
        sources = ((k_r, ks_scr, False), (v_ref[...], vs_scr, True),
                   (kc_ref[0, 0].astype(BF16), kcs_scr, False), (vc_ref[0, 0].astype(BF16), vcs_scr, True))
        for src, dst, is_value in sources:
            low = lax.broadcasted_iota(jnp.int32, (src.shape[0], LANES), 1) < HEAD_DIM
            for kh in range(hkv):
                slab = src[:, (kh // 2) * LANES:(kh // 2 + 1) * LANES]
                for half in range(2):
                    x = slab if kh % 2 == half else _swap_halves(slab)
                    if is_value:
                        x = jnp.where(low if half == 0 else jnp.logical_not(low), x, 1)
                    dst[kh * 2 + half] = x

    q0 = pl.multiple_of(n * qb, qb)
    start = pl.multiple_of(jnp.clip(n * qb - qb, 0, t_lat - span), qb)
    cos_q = cos_ref[pl.ds(q0, qb), :]
    sin_q = sin_ref[pl.ds(q0, qb), :]
    qpos = q0 + lax.broadcasted_iota(jnp.int32, (qb, span), 0)
    kpos = start + lax.broadcasted_iota(jnp.int32, (qb, span), 1)
    valid = jnp.abs(qpos - kpos) <= SWA_WIN
    in_window = lambda s: jnp.where(valid, s, NEG_INF)
    low = lax.broadcasted_iota(jnp.int32, (qb, LANES), 1) < HEAD_DIM
    scale = jnp.asarray(HEAD_DIM ** -0.5, BF16)
    heads = []
    for kh in range(hkv):
        q_r = _rope(q_ref[:, kh * wk:(kh + 1) * wk].astype(F32), cos_q, sin_q).astype(BF16) * scale
        for gi in range(g):
            half = gi % 2
            idx = kh * 2 + half
            qp = q_r[:, (gi // 2) * LANES:(gi // 2 + 1) * LANES]
            heads.append((jnp.where(low if half == 0 else jnp.logical_not(low), qp, 0),
                          [ks_scr[idx, pl.ds(start, span), :], kcs_scr[idx]],
                          [vs_scr[idx, pl.ds(start, span), :], vcs_scr[idx]],
                          [in_window, None], sink_ref[kh, gi]))
    res = _half_head_attention(heads, SWA_HEAD_BATCH)
    o_ref[...] = jnp.concatenate([jnp.where(low, res[2 * p], res[2 * p + 1]) for p in range(len(res) // 2)],
                                 axis=-1).astype(BF16)


def _swa_attn(u, sink, kc, vc, cos, sin, l, *, n_ctx, db, t_lat, qcol, kcol, vcol, hkv, g):
    qb = SWA_QBLOCK
    nq = t_lat // qb
    wq = hkv * g * HEAD_DIM
    wk = hkv * HEAD_DIM
    assert g * HEAD_DIM == wk, "rotary tables are shared between the q groups and k"
    p = kc.shape[2]
    kern = functools.partial(_swa_kernel, hkv=hkv, g=g, t_lat=t_lat)
    return pl.pallas_call(
        kern,
        grid=(db, nq),
        in_specs=[
            pl.BlockSpec(memory_space=pltpu.SMEM),
            pl.BlockSpec((qb, wq), lambda b, i: (n_ctx // qb + b * nq + i, qcol)),
            pl.BlockSpec((t_lat, wk), lambda b, i: (n_ctx // t_lat + b, kcol)),
            pl.BlockSpec((t_lat, wk), lambda b, i: (n_ctx // t_lat + b, vcol)),
            pl.BlockSpec((1, 1, p, wk), lambda b, i: (b, l, 0, 0)),
            pl.BlockSpec((1, 1, p, wk), lambda b, i: (b, l, 0, 0)),
            pl.BlockSpec((t_lat, wk), lambda b, i: (0, 0)),
            pl.BlockSpec((t_lat, wk), lambda b, i: (0, 0)),
        ],
        out_specs=pl.BlockSpec((qb, wq), lambda b, i: (b * nq + i, 0)),
        out_shape=jax.ShapeDtypeStruct((db * t_lat, wq), BF16),
        scratch_shapes=[pltpu.VMEM((2 * hkv, t_lat, LANES), BF16), pltpu.VMEM((2 * hkv, t_lat, LANES), BF16),
                        pltpu.VMEM((2 * hkv, p, LANES), BF16), pltpu.VMEM((2 * hkv, p, LANES), BF16)],
        compiler_params=_cparams(2),
        name="swa_attn",
    )(sink, u, u, u, kc, vc, cos, sin)


def _rope_tables(t_lat, width):
    quarter = HEAD_DIM // 4
    pos = np.arange(t_lat)
    freqs = ROPE_BASE ** (-np.arange(quarter, dtype=np.float32) / quarter)
    ang_r = (pos // GRID_W).astype(np.float32)[:, None] * freqs[None, :]
    ang_c = (pos % GRID_W).astype(np.float32)[:, None] * freqs[None, :]
    ang_r, ang_c = jnp.asarray(ang_r, F32), jnp.asarray(ang_c, F32)
    cos = jnp.concatenate([jnp.cos(ang_r), jnp.cos(ang_r), jnp.cos(ang_c), jnp.cos(ang_c)], axis=-1)
    sin = jnp.concatenate([-jnp.sin(ang_r), jnp.sin(ang_r), -jnp.sin(ang_c), jnp.sin(ang_c)], axis=-1)
    reps = width // HEAD_DIM
    return jnp.tile(cos, (1, reps)), jnp.tile(sin, (1, reps))


def _log_sigmoid(x):
    return jnp.minimum(x, 0.0) - jnp.log(1.0 + jnp.exp(-jnp.abs(x)))


def _gla_kernel(*refs, heads, dk, dv, seq_len, has_s0, out_state):
    refs = list(refs)
    ub_ref, z_ref, gw_ref, gb_ref, gn_ref = refs[:5]
    refs = refs[5:]
    s0_ref = refs.pop(0) if has_s0 else None
    if out_state:
        refs.pop(0)
    o_ref = refs.pop(0)
    st_ref = refs.pop(0) if out_state else None
    la_scr, o_scr, st_scr = refs
    c_len = GLA_CHUNK
    nc = seq_len // c_len
    wk = heads * dk
    wv = heads * dv

    z_hi, z_lo = _split_bf16(z_ref[...])
    for d in range(2):
        g_hi, g_lo = _split_bf16(gw_ref[d])
        pre = _dot(z_hi, g_hi) + (_dot(z_hi, g_lo) + _dot(z_lo, g_hi)) + gb_ref[d]
        la_scr[d] = _log_sigmoid(pre) / GLA_TAU

    eye_v = (lax.broadcasted_iota(jnp.int32, (dv, dv), 0) == lax.broadcasted_iota(jnp.int32, (dv, dv), 1)).astype(F32)
    for d in range(2):
        for h in range(heads):
            if has_s0:
                s0_rows = jnp.concatenate(
                    [s0_ref[0, 0, d, h] if hh == h else jnp.zeros((dk, dv), F32) for hh in range(heads)], axis=0)
                st_scr[d, h] = _dot_nt(eye_v, s0_rows, precision=HIGHEST)
            else:
                st_scr[d, h] = jnp.zeros((dv, wk), F32)

    ng = GLA_GROUP
    gr = ng * c_len
    ti = lax.broadcasted_iota(jnp.int32, (gr, gr), 0)
    si = lax.broadcasted_iota(jnp.int32, (gr, gr), 1)
    same_chunk = (ti // c_len) == (si // c_len)
    lane_head = lax.broadcasted_iota(jnp.int32, (gr, wk), 1) // dk
    row_chunk = lax.broadcasted_iota(jnp.int32, (gr, dv), 0) // c_len

    def group(gi, d):
        reverse = d == 1
        keep = same_chunk & ((si >= ti) if reverse else (si <= ti))
        sums = jnp.concatenate([keep.astype(BF16), same_chunk.astype(BF16)], axis=0)
        rows = pl.ds(pl.multiple_of(gi * gr, gr), gr)
        la_hi, la_lo = _split_bf16(la_scr[d, rows, :])
        res = _dot(sums, la_hi) + _dot(sums, la_lo)
        b, b_end = res[:gr], res[gr:]
        q = ub_ref[rows, 0:wk].astype(F32) * (dk ** -0.5)
        k = ub_ref[rows, wk:2 * wk].astype(F32)
        qt = q * jnp.exp(b)
        kt = (k * jnp.exp(-b)).astype(BF16)
        ke = (k * jnp.exp(b_end - b)).astype(BF16)
        dec = jnp.exp(b_end)
        order = range(ng - 1, -1, -1) if reverse else range(ng)
        for h in range(heads):
            vs = slice(h * dv, (h + 1) * dv)
            qm = jnp.where(lane_head == h, qt, 0.0).astype(BF16)
            a = jnp.where(keep, _dot_nt(qm, kt), 0.0).astype(BF16)
            v_h = ub_ref[rows, 2 * wk + h * dv:2 * wk + (h + 1) * dv]
            o = _dot(a, v_h)
            v_f = v_h.astype(F32)
            v_blk = jnp.concatenate([jnp.where(row_chunk == c, v_f, 0.0).astype(BF16) for c in range(ng)], axis=1)
            ut = _dot_tn(v_blk, ke)
            st = st_scr[d, h]
            inter = [None] * ng
            for c in order:
                inter[c] = _dot_nt(qm[c * c_len:(c + 1) * c_len], st.astype(BF16))
                st = st * dec[c * c_len:c * c_len + 1, :] + ut[c * dv:(c + 1) * dv]
            st_scr[d, h] = st
            o = o + jnp.concatenate(inter, axis=0)
            if reverse:
                o_scr[rows, vs] += o
            else:
                o_scr[rows, vs] = o

    n_groups = seq_len // gr
    lax.fori_loop(0, n_groups, lambda g, carry: (group(g, 0), carry)[1], 0)
    lax.fori_loop(0, n_groups, lambda g, carry: (group(n_groups - 1 - g, 1), carry)[1], 0)

    r = ub_ref[:, 2 * wk + wv:2 * wk + 2 * wv].astype(F32)
    gn = gn_ref[...]
    for h in range(heads):
        vs = slice(h * dv, (h + 1) * dv)
        o_ref[:, vs] = (_rmsnorm(o_scr[:, vs], gn[:, vs]) * _silu(r[:, vs])).astype(BF16)
    if out_state:
        for d in range(2):
            for h in range(heads):
                st_ref[0, 0, d, h] = st_scr[d, h].T[h * dk:(h + 1) * dk, :]


def _gla(u, z, gw, gb, gn, s0, st_buf, l, *, row0, nb, seq_len, bcol, heads, dk, dv):
    wk, wv = heads * dk, heads * dv
    wb = 2 * wk + 2 * wv
    has_s0 = s0 is not None
    out_state = st_buf is not None
    rb = row0 // seq_len
    kern = functools.partial(_gla_kernel, heads=heads, dk=dk, dv=dv, seq_len=seq_len, has_s0=has_s0,
                             out_state=out_state)
    in_specs = [
        pl.BlockSpec((seq_len, wb), lambda b: (rb + b, bcol)),
        pl.BlockSpec((seq_len, LANES), lambda b: (rb + b, 0)),
        pl.BlockSpec((2, LANES, wk), lambda b: (0, 0, 0)),
        pl.BlockSpec((2, 1, wk), lambda b: (0, 0, 0)),
        pl.BlockSpec((1, wv), lambda b: (0, 0)),
    ]
    args = [u, z, gw, gb, gn]
    if has_s0:
        in_specs.append(pl.BlockSpec((1, 1, 2, heads, dk, dv), lambda b: (b, l, 0, 0, 0, 0)))
        args.append(s0)
    out_specs = [pl.BlockSpec((seq_len, wv), lambda b: (b, 0))]
    out_shape = [jax.ShapeDtypeStruct((nb * seq_len, wv), BF16)]
    aliases = {}
    if out_state:
        aliases = {len(args): 1}
        in_specs.append(pl.BlockSpec(memory_space=pl.ANY))
        args.append(st_buf)
        out_specs.append(pl.BlockSpec((1, 1, 2, heads, dk, dv), lambda b: (b, l, 0, 0, 0, 0)))
        out_shape.append(jax.ShapeDtypeStruct(st_buf.shape, st_buf.dtype))
    res = pl.pallas_call(
        kern,
        grid=(nb,),
        in_specs=in_specs,
        out_specs=out_specs,
        out_shape=out_shape,
        scratch_shapes=[
            pltpu.VMEM((2, seq_len, wk), F32),
            pltpu.VMEM((seq_len, wv), F32),
            pltpu.VMEM((2, heads, dv, wk), F32),
        ],
        input_output_aliases=aliases,
        compiler_params=_cparams(1),
        name="gla_lat" if has_s0 else "gla_ctx",
    )(*args)
    return res


ROUTE_I1, ROUTE_I2, ROUTE_W1, ROUTE_W2, ROUTE_R1, ROUTE_R2 = range(6)


def _pack_bf16_pairs(h):
    c = h.shape[1] // 2
    lo = lax.bitcast_convert_type(h[:, :c].astype(BF16).astype(F32), jnp.uint32)
    hi = lax.bitcast_convert_type(h[:, c:].astype(BF16).astype(F32), jnp.uint32)
    return hi | (lo >> 16)


def _unpack_bf16_pairs(w):
    lo = lax.bitcast_convert_type(w << 16, F32).astype(BF16)
    hi = lax.bitcast_convert_type(w & jnp.uint32(0xFFFF0000), F32).astype(BF16)
    return jnp.concatenate([lo, hi], axis=-1)


def _outproj_kernel(*refs, n_experts, top_k, nct, split_x):
    refs = list(refs)
    x_ref = refs.pop(0)
    xl_ref = refs.pop(0) if split_x else None
    ctx_refs, lat_refs, refs = refs[0:3], refs[3:6], refs[6:]
    if n_experts:
        w_ref, mod_ref, g_ref, wr_ref, br_ref, xn_ref, h_ref, route_ref, counts_ref, run_scr = refs
    else:
        w_ref, mod_ref, g_ref, xn_ref, h_ref = refs
    is_ctx = pl.program_id(0) < nct
    mixed = jnp.concatenate([jnp.where(is_ctx, c[...], t[...]) for c, t in zip(ctx_refs, lat_refs)], axis=-1)
    x = jnp.where(is_ctx, x_ref[...], xl_ref[...]) if split_x else x_ref[...]
    xn = x + mod_ref[0, 2:3, :] * _dot(mixed, w_ref[...])
    xn_ref[...] = xn
    h = _rmsnorm(xn, g_ref[...]) * (1.0 + mod_ref[0, 4:5, :]) + mod_ref[0, 3:4, :]
    if not n_experts:
        h_ref[...] = h.astype(BF16)
        return
    assert top_k == 2
    h_ref[...] = _pack_bf16_pairs(h)
    h_hi = h.astype(BF16)
    h_lo = (h - h_hi.astype(F32)).astype(BF16)
    wr = wr_ref[...]
    w_hi = wr.astype(BF16)
    w_lo = (wr - w_hi.astype(F32)).astype(BF16)
    logits = _dot(h_hi, w_hi) + (_dot(h_hi, w_lo) + _dot(h_lo, w_hi)) + br_ref[...]
    tm = logits.shape[0]
    lane = lax.broadcasted_iota(jnp.int32, logits.shape, 1)
    l1 = jnp.max(logits, axis=-1, keepdims=True)
    i1 = jnp.min(jnp.where(logits == l1, lane, LANES), axis=-1, keepdims=True)
    rest = jnp.where(lane == i1, -jnp.inf, logits)
    l2 = jnp.max(rest, axis=-1, keepdims=True)
    i2 = jnp.min(jnp.where(rest == l2, lane, LANES), axis=-1, keepdims=True)
    e2 = jnp.exp(l2 - l1)
    w1 = 1.0 / (1.0 + e2)
    w2 = e2 * w1
    @pl.when(pl.program_id(0) == 0)
    def _():
        run_scr[...] = jnp.zeros(run_scr.shape, F32)

    oh1 = lane == i1
    oh2 = lane == i2
    earlier = (lax.broadcasted_iota(jnp.int32, (tm, tm), 1) < lax.broadcasted_iota(jnp.int32, (tm, tm), 0)).astype(BF16)
    c1 = _dot(earlier, oh1.astype(BF16))
    c2 = _dot(earlier, oh2.astype(BF16))
    n1 = jnp.sum(oh1.astype(F32), axis=0, keepdims=True)
    n2 = jnp.sum(oh2.astype(F32), axis=0, keepdims=True)
    run = run_scr[0:1, :]
    r1 = jnp.sum(jnp.where(oh1, run + c1, 0.0), axis=-1, keepdims=True)
    r2 = jnp.sum(jnp.where(oh2, run + n1 + c2, 0.0), axis=-1, keepdims=True)
    total = run + n1 + n2
    run_scr[...] = jnp.broadcast_to(total, run_scr.shape)
    counts_ref[...] = jnp.broadcast_to(total, counts_ref.shape)
    rec = jnp.zeros(logits.shape, F32)
    for slot, val in ((ROUTE_I1, i1.astype(F32)), (ROUTE_I2, i2.astype(F32)), (ROUTE_W1, w1), (ROUTE_W2, w2),
                      (ROUTE_R1, r1), (ROUTE_R2, r2)):
        rec = jnp.where(lane == slot, val, rec)
    route_ref[...] = rec


def _outproj(x, x_lat, mixed_ctx, mixed_lat, w_out, l, mod_l, g, router, *, n_ctx, t_lat, tm=512):
    split_x = x_lat is not None
    d = x.shape[1]
    n = x.shape[0] + (x_lat.shape[0] if split_x else 0)
    nct = n_ctx // tm
    nlt = n // tm - nct

    def cond_idx(i):
        return jnp.where(i < nct, 0, 1 + ((i - nct) * tm) // t_lat)

    row = lambda i: (i, 0)
    const = lambda i: (0, 0)
    ctx_row = lambda i: (jnp.minimum(i, nct - 1), 0)
    lat_row = lambda i: (jnp.clip(i - nct, 0, nlt - 1), 0)
    x_specs = [pl.BlockSpec((tm, d), ctx_row), pl.BlockSpec((tm, d), lat_row)] if split_x else [pl.BlockSpec((tm, d), row)]
    x_args = [x, x_lat] if split_x else [x]
    in_specs = (
        x_specs
        + [pl.BlockSpec((tm, o.shape[1]), ctx_row) for o in mixed_ctx]
        + [pl.BlockSpec((tm, o.shape[1]), lat_row) for o in mixed_lat]
        + [pl.BlockSpec((None, d, d), lambda i: (l, 0, 0)), pl.BlockSpec((1, 6, d), lambda i: (cond_idx(i), 0, 0)),
           pl.BlockSpec((1, d), const)]
    )
    args = [*x_args, *mixed_ctx, *mixed_lat, w_out, mod_l, g]
    n_experts = 0
    scratch = []
    if router is None:
        out_specs = [pl.BlockSpec((tm, d), row), pl.BlockSpec((tm, d), row)]
        out_shape = [jax.ShapeDtypeStruct((n, d), F32), jax.ShapeDtypeStruct((n, d), BF16)]
    else:
        wr, br, n_experts = router
        in_specs += [pl.BlockSpec((d, LANES), const), pl.BlockSpec((1, LANES), const)]
        args += [wr, br]
        out_specs = [pl.BlockSpec((tm, d), row), pl.BlockSpec((tm, d // 2), row), pl.BlockSpec((tm, LANES), row),
                     pl.BlockSpec((8, LANES), const)]
        out_shape = [jax.ShapeDtypeStruct((n, d), F32), jax.ShapeDtypeStruct((n, d // 2), jnp.uint32),
                     jax.ShapeDtypeStruct((n, LANES), F32), jax.ShapeDtypeStruct((8, LANES), F32)]
        scratch = [pltpu.VMEM((8, LANES), F32)]
    kern = functools.partial(_outproj_kernel, n_experts=n_experts, top_k=2, nct=nct, split_x=split_x)
    return pl.pallas_call(
        kern,
        grid=(n // tm,),
        in_specs=in_specs,
        out_specs=out_specs,
        out_shape=out_shape,
        scratch_shapes=scratch,
        compiler_params=_cparams(1),
        name="outproj_router" if n_experts else "outproj",
    )(*args)


def _row_copy(src_ref, src_row, dst_ref, dst_row, sem):
    return pltpu.make_async_copy(src_ref.at[pl.ds(src_row, 1), :], dst_ref.at[pl.ds(dst_row, 1), :], sem)


def _dispatch_kernel(pos_ref, ztile_ref, zon_ref, hp_ref, hs_ref, zbuf, sem, zsem, *, n, tm):
    base = pl.program_id(0) * tm

    @pl.when(pl.program_id(0) == 0)
    def _():
        zbuf[...] = jnp.zeros(zbuf.shape, zbuf.dtype)
        clears = [pltpu.make_async_copy(zbuf, hs_ref.at[pl.ds(pl.multiple_of(ztile_ref[t] * tm, tm), tm), :], zsem)
                  for t in range(ztile_ref.shape[0])]
        for t, cp in enumerate(clears):
            pl.when(zon_ref[t] == 1)(cp.start)
        for t, cp in enumerate(clears):
            pl.when(zon_ref[t] == 1)(cp.wait)

    def copies(k):
        return (_row_copy(hp_ref, k, hs_ref, pos_ref[base + k], sem),
                _row_copy(hp_ref, k, hs_ref, pos_ref[n + base + k], sem))

    def start(k, carry):
        for queue, cp in enumerate(copies(k)):
            cp.start(priority=queue)
        return carry

    def wait(k, carry):
        for cp in copies(k):
            cp.wait()
        return carry

    lax.fori_loop(0, tm, start, 0, unroll=8)
    lax.fori_loop(0, tm, wait, 0, unroll=8)


def _dispatch(pos, clear_tiles, clear_on, hp, n_rows_sorted, tm=512):
    n, c = hp.shape
    return pl.pallas_call(
        functools.partial(_dispatch_kernel, n=n, tm=tm),
        grid_spec=pltpu.PrefetchScalarGridSpec(
            num_scalar_prefetch=3,
            grid=(n // tm,),
            in_specs=[pl.BlockSpec((tm, c), lambda i, pos, zt, zo: (i, 0))],
            out_specs=pl.BlockSpec(memory_space=pl.ANY),
            scratch_shapes=[pltpu.VMEM((tm, c), hp.dtype), pltpu.SemaphoreType.DMA(()), pltpu.SemaphoreType.DMA(())],
        ),
        out_shape=jax.ShapeDtypeStruct((n_rows_sorted, c), hp.dtype),
        compiler_params=_cparams(1),
        name="moe_dispatch",
    )(pos, clear_tiles, clear_on, hp)


def _moe_ffn_kernel(texp_ref, trow_ref, nact_ref, hs_ref, wg_ref, wu_ref, wd_ref, ys_ref, h_scr, acc_scr):
    del texp_ref, trow_ref
    i = pl.program_id(0)
    f = pl.program_id(1)
    last = pl.num_programs(1) - 1
    active = i < nact_ref[0]

    @pl.when(active & (f == 0))
    def _():
        h_scr[...] = _unpack_bf16_pairs(hs_ref[...])

    def contribution():
        h = h_scr[...]
        t = _silu(_dot(h, wg_ref[...])) * _dot(h, wu_ref[...])
        return _dot(t.astype(BF16), wd_ref[...])

    @pl.when(active & (f == 0))
    def _():
        acc_scr[...] = contribution()

    @pl.when(active & (f > 0) & (f < last))
    def _():
        acc_scr[...] += contribution()

    @pl.when(active & (f == last))
    def _():
        ys_ref[...] = _pack_bf16_pairs(acc_scr[...] + contribution())

    @pl.when(jnp.logical_not(active) & (f == 0))
    def _():
        ys_ref[...] = jnp.zeros(ys_ref.shape, ys_ref.dtype)


def _moe_ffn(tile_exp, tile_row, n_active, hs, wg, wu, wd, li, *, tm, nf=2):
    r, c = hs.shape
    d, ff = wg.shape[2:]
    tf = ff // nf
    assert nf >= 2, "the first and the last hidden tile are handled by different branches"

    def f_eff(i, f, nact):
        return jnp.where(i < nact[0], f, nf - 1)

    return pl.pallas_call(
        _moe_ffn_kernel,
        grid_spec=pltpu.PrefetchScalarGridSpec(
            num_scalar_prefetch=3,
            grid=(r // tm, nf),
            in_specs=[
                pl.BlockSpec((tm, c), lambda i, f, te, tr, na: (tr[i], 0)),
                pl.BlockSpec((None, None, d, tf), lambda i, f, te, tr, na: (li, te[i], 0, f_eff(i, f, na))),
                pl.BlockSpec((None, None, d, tf), lambda i, f, te, tr, na: (li, te[i], 0, f_eff(i, f, na))),
                pl.BlockSpec((None, None, tf, d), lambda i, f, te, tr, na: (li, te[i], f_eff(i, f, na), 0)),
            ],
            out_specs=pl.BlockSpec((tm, d // 2), lambda i, f, te, tr, na: (i, 0)),
            scratch_shapes=[pltpu.VMEM((tm, d), BF16), pltpu.VMEM((tm, d), F32)],
        ),
        out_shape=jax.ShapeDtypeStruct((r, d // 2), jnp.uint32),
        compiler_params=_cparams(2),
        name="moe_ffn",
    )(tile_exp, tile_row, n_active, hs, wg, wu, wd)


def _combine_kernel(*refs, n, tm, final_norm, nct):
    refs = list(refs)
    pos_ref, ys_ref, x_ref, route_ref, mod_ref = refs[:5]
    refs = refs[5:]
    fg_ref = refs.pop(0) if final_norm else None
    o_ref = refs.pop(0)
    ol_ref = refs.pop(0) if final_norm else None
    buf, sem = refs
    base = pl.program_id(0) * tm

    def copies(k):
        return (_row_copy(ys_ref, pos_ref[base + k], buf.at[0], k, sem),
                _row_copy(ys_ref, pos_ref[n + base + k], buf.at[1], k, sem))

    def start(k, carry):
        for queue, cp in enumerate(copies(k)):
            cp.start(priority=queue)
        return carry

    def wait(k, carry):
        for cp in copies(k):
            cp.wait()
        return carry

    lax.fori_loop(0, tm, start, 0, unroll=8)
    lax.fori_loop(0, tm, wait, 0, unroll=8)
    route = route_ref[...]
    lane = lax.broadcasted_iota(jnp.int32, route.shape, 1)
    w1 = jnp.sum(jnp.where(lane == ROUTE_W1, route, 0.0), axis=-1, keepdims=True)
    w2 = jnp.sum(jnp.where(lane == ROUTE_W2, route, 0.0), axis=-1, keepdims=True)
    y1 = _unpack_bf16_pairs(buf[0]).astype(F32)
    y2 = _unpack_bf16_pairs(buf[1]).astype(F32)
    y = x_ref[...] + mod_ref[0, 5:6, :] * (w1 * y1 + w2 * y2)
    if not final_norm:
        o_ref[...] = y
        return
    y = _rmsnorm(y, fg_ref[...])
    is_ctx = pl.program_id(0) < nct

    @pl.when(is_ctx)
    def _():
        o_ref[...] = y

    @pl.when(jnp.logical_not(is_ctx))
    def _():
        ol_ref[...] = y


def _combine(pos, ys, x, route, mod_l, final_g, *, n_ctx, t_lat, tm=512):
    n, d = x.shape
    nct = n_ctx // tm
    nlt = n // tm - nct

    def cond_idx(i):
        return jnp.where(i < nct, 0, 1 + ((i - nct) * tm) // t_lat)

    in_specs = [
        pl.BlockSpec(memory_space=pl.ANY),
        pl.BlockSpec((tm, d), lambda i, pos: (i, 0)),
        pl.BlockSpec((tm, LANES), lambda i, pos: (i, 0)),
        pl.BlockSpec((1, 6, d), lambda i, pos: (cond_idx(i), 0, 0)),
    ]
    args = [ys, x, route, mod_l]
    if final_g is None:
        out_specs = pl.BlockSpec((tm, d), lambda i, pos: (i, 0))
        out_shape = jax.ShapeDtypeStruct((n, d), F32)
    else:
        in_specs.append(pl.BlockSpec((1, d), lambda i, pos: (0, 0)))
        args.append(final_g)
        out_specs = [pl.BlockSpec((tm, d), lambda i, pos: (jnp.minimum(i, nct - 1), 0)),
                     pl.BlockSpec((tm, d), lambda i, pos: (jnp.clip(i - nct, 0, nlt - 1), 0))]
        out_shape = [jax.ShapeDtypeStruct((n_ctx, d), F32), jax.ShapeDtypeStruct((n - n_ctx, d), F32)]
    return pl.pallas_call(
        functools.partial(_combine_kernel, n=n, tm=tm, final_norm=final_g is not None, nct=nct),
        grid_spec=pltpu.PrefetchScalarGridSpec(
            num_scalar_prefetch=1,
            grid=(n // tm,),
            in_specs=in_specs,
            out_specs=out_specs,
            scratch_shapes=[pltpu.VMEM((2, tm, d // 2), jnp.uint32), pltpu.SemaphoreType.DMA(())],
        ),
        out_shape=out_shape,
        compiler_params=_cparams(1),
        name="moe_combine",
    )(pos, *args)


def _routing_tables(route, counts, n_experts, tm):
    n = route.shape[0]
    n_tiles = 2 * n // tm + n_experts
    cnt = counts[0, :n_experts].astype(jnp.int32)
    tiles_e = (cnt + tm - 1) // tm
    tile_end = jnp.cumsum(tiles_e)
    offs = (tile_end - tiles_e) * tm
    i1 = route[:, ROUTE_I1].astype(jnp.int32)
    i2 = route[:, ROUTE_I2].astype(jnp.int32)
    eidx = jnp.arange(n_experts, dtype=jnp.int32)[None, :]
    off1 = jnp.sum(jnp.where(i1[:, None] == eidx, offs[None, :], 0), axis=1)
    off2 = jnp.sum(jnp.where(i2[:, None] == eidx, offs[None, :], 0), axis=1)
    pos = jnp.concatenate([off1 + route[:, ROUTE_R1].astype(jnp.int32), off2 + route[:, ROUTE_R2].astype(jnp.int32)])
    n_active = tile_end[-1]
    t = jnp.arange(n_tiles, dtype=jnp.int32)
    t_eff = jnp.minimum(t, n_active - 1)
    tile_exp = jnp.sum((t_eff[:, None] >= tile_end[None, :]).astype(jnp.int32), axis=1)
    tails = jnp.maximum(tile_end - 1, 0)
    trailing = n_tiles - 1 - jnp.arange(n_experts, dtype=jnp.int32)
    clear_tiles = jnp.concatenate([tails, trailing]).astype(jnp.int32)
    clear_on = jnp.concatenate([tiles_e > 0, trailing >= n_active]).astype(jnp.int32)
    return (pos, clear_tiles, clear_on, tile_exp.astype(jnp.int32), t_eff.astype(jnp.int32),
            n_active.reshape(1).astype(jnp.int32), n_tiles)


def _ffn_kernel(*refs, final_norm):
    refs = list(refs)
    h_ref, x_ref, wg_ref, wu_ref, wd_ref, mod_ref = refs[:6]
    refs = refs[6:]
    fg_ref = refs.pop(0) if final_norm else None
    (o_ref,) = refs
    f = pl.program_id(1)

    @pl.when(f == 0)
    def _():
        o_ref[...] = jnp.zeros(o_ref.shape, F32)

    h = h_ref[...]
    t = _silu(_dot(h, wg_ref[...])) * _dot(h, wu_ref[...])
    o_ref[...] += _dot(t.astype(BF16), wd_ref[...])

    @pl.when(f == pl.num_programs(1) - 1)
    def _():
        y = x_ref[...] + mod_ref[0, 5:6, :] * o_ref[...]
        if final_norm:
            y = _rmsnorm(y, fg_ref[...])
        o_ref[...] = y


def _ffn(h, x, wg, wu, wd, li, mod_l, final_g, *, n_ctx, t_lat, tm=512):
    n, d = x.shape
    tf = FFN_TF
    nf = wg.shape[2] // tf
    nct = n_ctx // tm

    def cond_idx(i):
        return jnp.where(i < nct, 0, 1 + ((i - nct) * tm) // t_lat)

    row = lambda i, f: (i, 0)
    in_specs = [
        pl.BlockSpec((tm, d), row),
        pl.BlockSpec((tm, d), row),
        pl.BlockSpec((None, d, tf), lambda i, f: (li, 0, f)),
        pl.BlockSpec((None, d, tf), lambda i, f: (li, 0, f)),
        pl.BlockSpec((None, tf, d), lambda i, f: (li, f, 0)),
        pl.BlockSpec((1, 6, d), lambda i, f: (cond_idx(i), 0, 0)),
    ]
    args = [h, x, wg, wu, wd, mod_l]
    if final_g is not None:
        in_specs.append(pl.BlockSpec((1, d), lambda i, f: (0, 0)))
        args.append(final_g)
    return pl.pallas_call(
        functools.partial(_ffn_kernel, final_norm=final_g is not None),
        grid=(n // tm, nf),
        in_specs=in_specs,
        out_specs=pl.BlockSpec((tm, d), row),
        out_shape=jax.ShapeDtypeStruct((n, d), F32),
        compiler_params=_cparams(2),
        name="ffn_dense",
    )(*args)


def kernel(x_prompt, x_sample, cache_nat_k, cache_nat_v, state_gla, cache_swa_k, cache_swa_v, c, c_ctx, w_ada, b_ada,
           norm_mix_g, norm_ffn_g, w_in, na_rpb, gla_w_gate, gla_b_gate, gla_norm_g, swa_sink, w_out, ffn_w_gate,
           ffn_w_up, ffn_w_down, moe_w_router, moe_b_router, moe_w_gate, moe_w_up, moe_w_down, final_norm_g):
    nb, seq, d = x_prompt.shape
    db, t_lat, _ = x_sample.shape
    depth = w_in.shape[0]
    p_len, h_a = cache_nat_k.shape[2], cache_nat_k.shape[3]
    h_b, dk_b, dv_b = state_gla.shape[3:6]
    hkv_c, g_c = swa_sink.shape[1], swa_sink.shape[2]
    rank = gla_w_gate.shape[2]
    n_exp = moe_w_router.shape[-1]
    w_a = h_a * HEAD_DIM
    w_bk = h_b * dk_b
    w_bv = h_b * dv_b
    w_cq = hkv_c * g_c * HEAD_DIM
    w_ck = hkv_c * HEAD_DIM
    n_ctx = nb * seq
    n_lat = db * t_lat
    n = n_ctx + n_lat
    z0 = 3 * w_a + 2 * w_bk + 2 * w_bv
    z1 = z0 + 2 * rank
    a_w = 3 * w_a
    b_w = 2 * w_bk + 2 * w_bv
    assert a_w == b_w and (a_w + b_w) % w_cq == 0 and (a_w + b_w + w_cq) % w_ck == 0 and 2 * rank <= LANES
    assert db + 1 <= 8

    x, x_lat = x_prompt.reshape(n_ctx, d), x_sample.reshape(n_lat, d)
    cond8 = jnp.concatenate([c_ctx[None, :], c, jnp.zeros((7 - db, d), F32)], axis=0)
    mod = _adaln(cond8, w_ada, b_ada).reshape(depth, 8, 6, d)

    cos, sin = _rope_tables(t_lat, w_ck)
    na_tab = _na_bias_tables(na_rpb)
    kc_a = cache_nat_k.reshape(db, depth, p_len, w_a)
    vc_a = cache_nat_v.reshape(db, depth, p_len, w_a)
    kc_c = cache_swa_k.reshape(db, depth, p_len, w_ck)
    vc_c = cache_swa_v.reshape(db, depth, p_len, w_ck)
    sizes = dict(n_ctx=n_ctx, t_lat=t_lat)

    kv_bufs = [jnp.zeros((nb, depth, seq, w), F32) for w in (w_a, w_a, w_ck, w_ck)]
    gla_st = jnp.zeros((nb, depth, 2, h_b, dk_b, dv_b), F32)
    n_main = w_in.shape[2] - (z1 - z0)
    before_z = lax.broadcasted_iota(jnp.int32, (1, 1, n_main), 2) < z0
    w_main = jnp.where(before_z, w_in[:, :, :n_main], w_in[:, :, z1 - z0:]).astype(BF16)
    wz = jnp.pad(w_in[:, :, z0:z1], ((0, 0), (0, 0), (0, LANES - 2 * rank))).astype(BF16)
    w_out_b = w_out.astype(BF16)
    ffn_wg, ffn_wu, ffn_wd = ffn_w_gate.astype(BF16), ffn_w_up.astype(BF16), ffn_w_down.astype(BF16)
    moe_wg, moe_wu, moe_wd = moe_w_gate.astype(BF16), moe_w_up.astype(BF16), moe_w_down.astype(BF16)
    for l in range(depth):
        u, z, *kv_bufs = _inproj(x, x_lat, mod[l], norm_mix_g[l][None, :], w_main, wz, kv_bufs, l, seq=seq, w_a=w_a,
                                 w_cq=w_cq, w_kvc=w_ck, **sizes)

        oa_c = _ctx_attn(u, None, nb=nb, seq=seq, qcol=0, kcol=1, vcol=2, wq=w_a, wk=w_a, hkv=h_a, g=1)
        oa_l = _na_attn(u, kc_a, vc_a, na_tab, l, n_ctx=n_ctx, db=db, t_lat=t_lat, w_a=w_a, heads=h_a)
        gw = jnp.zeros((2, LANES, w_bk), F32)
        gw = gw.at[0, 0:rank].set(gla_w_gate[l, 0]).at[1, rank:2 * rank].set(gla_w_gate[l, 1])
        gb = gla_b_gate[l][:, None, :]
        gn = gla_norm_g[l].reshape(1, w_bv)
        gla_args = dict(bcol=a_w // b_w, heads=h_b, dk=dk_b, dv=dv_b)
        ob_c, gla_st = _gla(u, z, gw, gb, gn, None, gla_st, l, row0=0, nb=nb, seq_len=seq, **gla_args)
        (ob_l,) = _gla(u, z, gw, gb, gn, state_gla, None, l, row0=n_ctx, nb=db, seq_len=t_lat, **gla_args)
        c0 = a_w + b_w
        cols = dict(qcol=c0 // w_cq, kcol=(c0 + w_cq) // w_ck, vcol=(c0 + w_cq) // w_ck + 1, hkv=hkv_c, g=g_c)
        oc_c = _ctx_attn(u, swa_sink[l], nb=nb, seq=seq, wq=w_cq, wk=w_ck, **cols)
        oc_l = _swa_attn(u, swa_sink[l], kc_c, vc_c, cos, sin, l, n_ctx=n_ctx, db=db, t_lat=t_lat, **cols)

        moe = l % 2 == 1
        i = l // 2
        router = None
        if moe:
            wr = jnp.pad(moe_w_router[i], ((0, 0), (0, LANES - n_exp)))
            br = jnp.concatenate([moe_b_router[i], jnp.full((LANES - n_exp,), NEG_INF, F32)])[None, :]
            router = (wr, br, n_exp)
        res = _outproj(x, x_lat, (oa_c, ob_c, oc_c), (oa_l, ob_l, oc_l), w_out_b, l, mod[l],
                       norm_ffn_g[l][None, :], router, **sizes)
        x_lat = None
        fg = final_norm_g[None, :] if l == depth - 1 else None
        if moe:
            xn, hp, route, counts = res
            tm_e = 512
            pos, clear_tiles, clear_on, tile_exp, tile_row, n_active, n_tiles = _routing_tables(route, counts, n_exp,
                                                                                                tm_e)
            hs = _dispatch(pos, clear_tiles, clear_on, hp, n_tiles * tm_e)
            ys = _moe_ffn(tile_exp, tile_row, n_active, hs, moe_wg, moe_wu, moe_wd, i, tm=tm_e)
            x = _combine(pos, ys, xn, route, mod[l], fg, **sizes)
        else:
            xn, h2 = res
            x = _ffn(h2, xn, ffn_wg, ffn_wu, ffn_wd, i, mod[l], fg, **sizes)

    y_ctx, y_lat = x if isinstance(x, (list, tuple)) else (x[:n_ctx], x[n_ctx:])
    nat_k, nat_v, swa_k, swa_v = kv_bufs
    return (y_ctx.reshape(nb, seq, d), y_lat.reshape(db, t_lat, d),
            nat_k.reshape(nb, depth, seq, h_a, HEAD_DIM), nat_v.reshape(nb, depth, seq, h_a, HEAD_DIM), gla_st,
            swa_k.reshape(nb, depth, seq, hkv_c, HEAD_DIM), swa_v.reshape(nb, depth, seq, hkv_c, HEAD_DIM))
```

```python
import functools

import numpy as np
import jax
import jax.numpy as jnp
from jax import lax
from jax.experimental import pallas as pl
from jax.experimental.pallas import tpu as pltpu

F32 = jnp.float32
BF16 = jnp.bfloat16
HIGHEST = lax.Precision.HIGHEST

EPS = 1e-6
NEG_INF = -1e30
GRID_W = 64
HEAD_DIM = 64
NA_ROWS = 8
NA_COLS = 16
SWA_WIN = 128
SWA_QBLOCK = 128
GLA_CHUNK = 64
GLA_GROUP = 4
GLA_TAU = 16.0
ROPE_BASE = 10000.0
LANES = 128
FFN_TF = 512
FFN_TM = 1024
INPROJ_TM = 1024
INPROJ_TM_SPLIT = 512
CTX_HEAD_BATCH = 16
NA_HEAD_BATCH = 4
SWA_HEAD_BATCH = 16
VMEM_LIMIT = 60 * 1024 * 1024


def _cparams(n_axes):
    return pltpu.CompilerParams(dimension_semantics=("arbitrary",) * n_axes, vmem_limit_bytes=VMEM_LIMIT)


def _dot(a, b, **kw):
    return jnp.dot(a, b, preferred_element_type=F32, **kw)


def _dot_nt(a, b, **kw):
    return lax.dot_general(a, b, (((1,), (1,)), ((), ())), preferred_element_type=F32, **kw)


def _dot_tn(a, b, **kw):
    return lax.dot_general(a, b, (((0,), (0,)), ((), ())), preferred_element_type=F32, **kw)


def _silu(x):
    return x / (1.0 + jnp.exp(-x))


def _split_bf16(x):
    hi = x.astype(BF16)
    return hi, (x - hi.astype(F32)).astype(BF16)


def _rmsnorm(x, g):
    return x * lax.rsqrt(jnp.mean(x * x, axis=-1, keepdims=True) + EPS) * g


def _adaln_kernel(cond_ref, w_ref, b_ref, o_ref):
    s = _silu(cond_ref[...]).astype(BF16)
    o_ref[0] = _dot(s, w_ref[0].astype(BF16)) + b_ref[0]


def _adaln(cond8, w_ada, b_ada, tn=1024):
    depth, d, n6 = w_ada.shape
    return pl.pallas_call(
        _adaln_kernel,
        grid=(depth, n6 // tn),
        in_specs=[
            pl.BlockSpec((8, d), lambda l, j: (0, 0)),
            pl.BlockSpec((1, d, tn), lambda l, j: (l, 0, j)),
            pl.BlockSpec((1, 1, tn), lambda l, j: (l, 0, j)),
        ],
        out_specs=pl.BlockSpec((1, 8, tn), lambda l, j: (l, 0, j)),
        out_shape=jax.ShapeDtypeStruct((depth, 8, n6), F32),
        compiler_params=_cparams(2),
        name="adaln",
    )(cond8, w_ada, b_ada.reshape(depth, 1, n6))


def _inproj_kernel(*refs, nct, seq, w_a, w_cq, w_kvc, split_x):
    refs = list(refs)
    x_ref = refs.pop(0)
    xl_ref = refs.pop(0) if split_x else None
    mod_ref, g_ref, w_ref, wz_ref = refs[:4]
    u_ref, z_ref, nk_ref, nv_ref, sk_ref, sv_ref, h_scr = refs[8:]
    i = pl.program_id(0)
    j = pl.program_id(1)
    is_ctx = i < nct

    @pl.when(j == 0)
    def _():
        x = jnp.where(is_ctx, x_ref[...], xl_ref[...]) if split_x else x_ref[...]
        h = _rmsnorm(x, g_ref[...]) * (1.0 + mod_ref[0, 1:2, :]) + mod_ref[0, 0:1, :]
        hb = h.astype(BF16)
        h_scr[...] = hb
        z_ref[...] = _dot(hb, wz_ref[...])

    acc = _dot(h_scr[...], w_ref[...])
    u_ref[...] = acc.astype(BF16)
    nseq = nk_ref.shape[0]

    @pl.when(is_ctx & (j == 0))
    def _():
        for bb in range(nseq):
            nk_ref[bb, 0] = acc[bb * seq:(bb + 1) * seq, w_a:2 * w_a]
            nv_ref[bb, 0] = acc[bb * seq:(bb + 1) * seq, 2 * w_a:3 * w_a]

    @pl.when(is_ctx & (j == 2))
    def _():
        for bb in range(nseq):
            sk_ref[bb, 0] = acc[bb * seq:(bb + 1) * seq, w_cq:w_cq + w_kvc]
            sv_ref[bb, 0] = acc[bb * seq:(bb + 1) * seq, w_cq + w_kvc:w_cq + 2 * w_kvc]


def _inproj(x, x_lat, mod_l, g, w_main, wz, kv_bufs, l, *, n_ctx, seq, t_lat, w_a, w_cq, w_kvc, tm=512):
    split_x = x_lat is not None
    d = x.shape[1]
    n = x.shape[0] + (x_lat.shape[0] if split_x else 0)
    n_main = w_main.shape[2]
    tn = 3 * w_a
    nct = n_ctx // tm
    nlt = n // tm - nct
    nseq = tm // seq
    assert n_ctx % tm == 0 and tm % seq == 0 and t_lat % tm == 0 and n_main == 3 * tn and w_cq + 2 * w_kvc == tn

    def cond_idx(i):
        return jnp.where(i < nct, 0, 1 + ((i - nct) * tm) // t_lat)

    kv_idx = lambda i, j: (jnp.minimum(i, nct - 1), l, 0, 0)
    if split_x:
        x_specs = [pl.BlockSpec((tm, d), lambda i, j: (jnp.minimum(i, nct - 1), 0)),
                   pl.BlockSpec((tm, d), lambda i, j: (jnp.clip(i - nct, 0, nlt - 1), 0))]
        x_args = [x, x_lat]
    else:
        x_specs = [pl.BlockSpec((tm, d), lambda i, j: (i, 0))]
        x_args = [x]
    n_in = len(x_args) + 4
    kern = functools.partial(_inproj_kernel, nct=nct, seq=seq, w_a=w_a, w_cq=w_cq, w_kvc=w_kvc, split_x=split_x)
    return pl.pallas_call(
        kern,
        grid=(n // tm, n_main // tn),
        in_specs=x_specs + [
            pl.BlockSpec((1, 6, d), lambda i, j: (cond_idx(i), 0, 0)),
            pl.BlockSpec((1, d), lambda i, j: (0, 0)),
            pl.BlockSpec((None, d, tn), lambda i, j: (l, 0, j)),
            pl.BlockSpec((None, d, LANES), lambda i, j: (l, 0, 0)),
        ] + [pl.BlockSpec(memory_space=pl.ANY)] * 4,
        out_specs=[
            pl.BlockSpec((tm, tn), lambda i, j: (i, j)),
            pl.BlockSpec((tm, LANES), lambda i, j: (i, 0)),
        ] + [pl.BlockSpec((nseq, 1, seq, b.shape[-1]), kv_idx) for b in kv_bufs],
        out_shape=[
            jax.ShapeDtypeStruct((n, n_main), BF16),
            jax.ShapeDtypeStruct((n, LANES), F32),
        ] + [jax.ShapeDtypeStruct(b.shape, b.dtype) for b in kv_bufs],
        scratch_shapes=[pltpu.VMEM((tm, d), BF16)],
        input_output_aliases={n_in + k: 2 + k for k in range(4)},
        compiler_params=_cparams(2),
        name="inproj",
    )(*x_args, mod_l, g, w_main, wz, *kv_bufs)


def _swap_halves(x):
    return pltpu.roll(x.astype(F32), LANES // 2, 1).astype(x.dtype)


def _half_head_attention(heads, batch):
    if len(heads) > batch:
        return _half_head_attention(heads[:batch], batch) + _half_head_attention(heads[batch:], batch)
    scores =[[_dot_nt(qm, kp) if fn is None else fn(_dot_nt(qm, kp)) for kp, fn in zip(keys, fns)]
              for qm, keys, _, fns, _ in heads]
    maxes = []
    for ss, (_, _, _, _, sink) in zip(scores, heads):
        m = jnp.max(ss[0], axis=-1, keepdims=True)
        for s in ss[1:]:
            m = jnp.maximum(m, jnp.max(s, axis=-1, keepdims=True))
        maxes.append(m if sink is None else jnp.maximum(m, sink))
    probs = [[jnp.exp(s - m).astype(BF16) for s in ss] for ss, m in zip(scores, maxes)]
    outs = []
    for ps, m, (_, _, values_aug, _, sink) in zip(probs, maxes, heads):
        acc = _dot(ps[0], values_aug[0])
        for p, va in zip(ps[1:], values_aug[1:]):
            acc = acc + _dot(p, va)
        den = pltpu.roll(acc, LANES // 2, 1)
        if sink is not None:
            den = den + jnp.exp(sink - m)
        outs.append(acc / den)
    return outs


def _ctx_attn_kernel(*refs, hkv, g, has_sink):
    if has_sink:
        sink_ref, q_ref, k_ref, v_ref, o_ref = refs
    else:
        q_ref, k_ref, v_ref, o_ref = refs
    q = q_ref[...] * jnp.asarray(HEAD_DIM ** -0.5, BF16)
    k = k_ref[...]
    v = v_ref[...]
    rows = q.shape[0]
    low = lax.broadcasted_iota(jnp.int32, (rows, LANES), 1) < HEAD_DIM
    kv_pairs = {}

    def kv_pair(kh, half):
        if (kh, half) not in kv_pairs:
            sl = slice((kh // 2) * LANES, (kh // 2 + 1) * LANES)
            kp, vp = k[:, sl], v[:, sl]
            if kh % 2 != half:
                kp, vp = _swap_halves(kp), _swap_halves(vp)
            kv_pairs[(kh, half)] = (kp, vp)
        return kv_pairs[(kh, half)]

    heads = []
    for hq in range(hkv * g):
        pair, half = hq // 2, hq % 2
        kh, gi = hq // g, hq % g
        qp = q[:, pair * LANES:(pair + 1) * LANES]
        own = low if half == 0 else jnp.logical_not(low)
        kp, vp = kv_pair(kh, half)
        sk = sink_ref[kh, gi] if has_sink else None
        heads.append((jnp.where(own, qp, 0), [kp], [jnp.where(own, vp, 1)], [None], sk))
    res = _half_head_attention(heads, CTX_HEAD_BATCH)
    o_ref[...] = jnp.concatenate([jnp.where(low, res[2 * p], res[2 * p + 1]) for p in range(len(res) // 2)],
                                 axis=-1).astype(BF16)


def _ctx_attn(u, sink, *, nb, seq, qcol, kcol, vcol, wq, wk, hkv, g):
    has_sink = sink is not None
    kern = functools.partial(_ctx_attn_kernel, hkv=hkv, g=g, has_sink=has_sink)
    in_specs = [
        pl.BlockSpec((seq, wq), lambda b: (b, qcol)),
        pl.BlockSpec((seq, wk), lambda b: (b, kcol)),
        pl.BlockSpec((seq, wk), lambda b: (b, vcol)),
    ]
    args = [u, u, u]
    if has_sink:
        in_specs = [pl.BlockSpec(memory_space=pltpu.SMEM)] + in_specs
        args = [sink] + args
    return pl.pallas_call(
        kern,
        grid=(nb,),
        in_specs=in_specs,
        out_specs=pl.BlockSpec((seq, wq), lambda b: (b, 0)),
        out_shape=jax.ShapeDtypeStruct((nb * seq, wq), BF16),
        compiler_params=_cparams(1),
        name="ctx_attn_sink" if has_sink else "ctx_attn",
    )(*args)


NA_QROWS = 4
NA_WIN_ROWS = 12


def _na_kernel(q_ref, k_ref, v_ref, kc_ref, vc_ref, bias_ref, o_ref, *, heads, rows):
    qi = pl.program_id(1)
    r0 = qi * NA_QROWS
    w0 = jnp.clip(r0 - NA_ROWS // 2, 0, rows - NA_WIN_ROWS)
    k0 = pl.multiple_of(w0 * GRID_W, NA_QROWS * GRID_W)
    tq = NA_QROWS * GRID_W
    tk = NA_WIN_ROWS * GRID_W
    q = q_ref[...] * jnp.asarray(HEAD_DIM ** -0.5, BF16)
    k = k_ref[pl.ds(k0, tk), :]
    v = v_ref[pl.ds(k0, tk), :]
    kc = kc_ref[0, 0].astype(BF16)
    vc = vc_ref[0, 0].astype(BF16)
    qrow = r0 + lax.broadcasted_iota(jnp.int32, (tq, tk), 0) // GRID_W
    krow = w0 + lax.broadcasted_iota(jnp.int32, (tq, tk), 1) // GRID_W
    band0 = jnp.clip(qrow - NA_ROWS // 2, 0, rows - NA_ROWS)
    in_band = (krow >= band0) & (krow < band0 + NA_ROWS)
    low_q = lax.broadcasted_iota(jnp.int32, (tq, LANES), 1) < HEAD_DIM
    low_k = lax.broadcasted_iota(jnp.int32, (tk, LANES), 1) < HEAD_DIM
    low_c = lax.broadcasted_iota(jnp.int32, (kc.shape[0], LANES), 1) < HEAD_DIM

    def window_scores(h):
        def fn(s):
            bias = jnp.concatenate(
                [jnp.concatenate([bias_ref[h, w0 + 2 * p - (r0 + rq) + 2 * NA_ROWS - 1]
                                  for p in range(NA_WIN_ROWS // 2)], axis=1) for rq in range(NA_QROWS)], axis=0)
            return jnp.where(in_band, s + bias, NEG_INF)
        return fn

    head_specs = []
    for h in range(heads):
        sl = slice((h // 2) * LANES, (h // 2 + 1) * LANES)
        first = h % 2 == 0
        own = lambda low: low if first else jnp.logical_not(low)
        head_specs.append((jnp.where(own(low_q), q[:, sl], 0), [k[:, sl], kc[:, sl]],
                           [jnp.where(own(low_k), v[:, sl], 1), jnp.where(own(low_c), vc[:, sl], 1)],
                           [window_scores(h), None], None))
    res = _half_head_attention(head_specs, NA_HEAD_BATCH)
    o_ref[...] = jnp.concatenate([jnp.where(low_q, res[2 * p], res[2 * p + 1]) for p in range(heads // 2)],
                                 axis=-1).astype(BF16)


def _na_attn(u, kc, vc, bias, l, *, n_ctx, db, t_lat, w_a, heads):
    tq = NA_QROWS * GRID_W
    rows = t_lat // GRID_W
    nq = t_lat // tq
    p = kc.shape[2]
    assert rows >= NA_WIN_ROWS and NA_WIN_ROWS >= NA_QROWS + NA_ROWS - 1 and (rows - NA_WIN_ROWS) % NA_QROWS == 0
    kern = functools.partial(_na_kernel, heads=heads, rows=rows)
    return pl.pallas_call(
        kern,
        grid=(db, nq),
        in_specs=[
            pl.BlockSpec((tq, w_a), lambda b, i: (n_ctx // tq + b * nq + i, 0)),
            pl.BlockSpec((t_lat, w_a), lambda b, i: (n_ctx // t_lat + b, 1)),
            pl.BlockSpec((t_lat, w_a), lambda b, i: (n_ctx // t_lat + b, 2)),
            pl.BlockSpec((1, 1, p, w_a), lambda b, i: (b, l, 0, 0)),
            pl.BlockSpec((1, 1, p, w_a), lambda b, i: (b, l, 0, 0)),
            pl.BlockSpec((None,) + bias.shape[1:], lambda b, i: (l, 0, 0, 0, 0)),
        ],
        out_specs=pl.BlockSpec((tq, w_a), lambda b, i: (b * nq + i, 0)),
        out_shape=jax.ShapeDtypeStruct((db * t_lat, w_a), BF16),
        compiler_params=_cparams(2),
        name="na_attn",
    )(u, u, u, kc, vc, bias)


def _na_bias_tables(rpb):
    col = np.arange(GRID_W)
    col_start = np.clip(col - NA_COLS // 2, 0, GRID_W - NA_COLS)
    col_ok = (col[None, :] >= col_start[:, None]) & (col[None, :] < col_start[:, None] + NA_COLS)
    dc = np.clip(col[None, :] - col[:, None] + NA_COLS - 1, 0, 2 * NA_COLS - 2)
    pick = (np.arange(2 * NA_COLS - 1)[:, None, None] == dc[None]).astype(np.float32)
    t1 = jnp.einsum("lhdj,jck->lhdck", rpb.astype(F32), jnp.asarray(pick), precision=HIGHEST)
    t1 = jnp.where(jnp.asarray(col_ok), t1, NEG_INF)
    t1 = jnp.pad(t1, ((0, 0), (0, 0), (NA_ROWS, NA_ROWS), (0, 0), (0, 0)), constant_values=NEG_INF)
    return jnp.concatenate([t1[:, :, :-1], t1[:, :, 1:]], axis=-1)


def _rope(x, cos, sin_signed):
    w = x.shape[-1]
    q4 = HEAD_DIM // 4
    lane = lax.broadcasted_iota(jnp.int32, x.shape, 1)
    first = (lane % (2 * q4)) < q4
    swapped = jnp.where(first, pltpu.roll(x, w - q4, 1), pltpu.roll(x, q4, 1))
    return x * cos + swapped * sin_signed


def _swa_kernel(sink_ref, q_ref, k_ref, v_ref, kc_ref, vc_ref, cos_ref, sin_ref, o_ref,
                ks_scr, vs_scr, kcs_scr, vcs_scr, *, hkv, g, t_lat):
    n = pl.program_id(1)
    qb = SWA_QBLOCK
    wk = hkv * HEAD_DIM
    span = 3 * qb

    @pl.when(n == 0)
    def _():
        k_r = _rope(k_ref[...].astype(F32), cos_ref[...], sin_ref[...]).astype(BF16)
        sources = ((k_r, ks_scr, False), (v_ref[...], vs_scr, True),
                   (kc_ref[0, 0].astype(BF16), kcs_scr, False), (vc_ref[0, 0].astype(BF16), vcs_scr, True))
        for src, dst, is_value in sources:
            low = lax.broadcasted_iota(jnp.int32, (src.shape[0], LANES), 1) < HEAD_DIM
            for kh in range(hkv):
                slab = src[:, (kh // 2) * LANES:(kh // 2 + 1) * LANES]
                for half in range(2):
                    x = slab if kh % 2 == half else _swap_halves(slab)
                    if is_value:
                        x = jnp.where(low if half == 0 else jnp.logical_not(low), x, 1)
                    dst[kh * 2 + half] = x

    q0 = pl.multiple_of(n * qb, qb)
    start = pl.multiple_of(jnp.clip(n * qb - qb, 0, t_lat - span), qb)
    cos_q = cos_ref[pl.ds(q0, qb), :]
    sin_q = sin_ref[pl.ds(q0, qb), :]
    qpos = q0 + lax.broadcasted_iota(jnp.int32, (qb, span), 0)
    kpos = start + lax.broadcasted_iota(jnp.int32, (qb, span), 1)
    valid = jnp.abs(qpos - kpos) <= SWA_WIN
    in_window = lambda s: jnp.where(valid, s, NEG_INF)
    low = lax.broadcasted_iota(jnp.int32, (qb, LANES), 1) < HEAD_DIM
    scale = jnp.asarray(HEAD_DIM ** -0.5, BF16)
    heads = []
    for kh in range(hkv):
        q_r = _rope(q_ref[:, kh * wk:(kh + 1) * wk].astype(F32), cos_q, sin_q).astype(BF16) * scale
        for gi in range(g):
            half = gi % 2
            idx = kh * 2 + half
            qp = q_r[:, (gi // 2) * LANES:(gi // 2 + 1) * LANES]
            heads.append((jnp.where(low if half == 0 else jnp.logical_not(low), qp, 0),
                          [ks_scr[idx, pl.ds(start, span), :], kcs_scr[idx]],
                          [vs_scr[idx, pl.ds(start, span), :], vcs_scr[idx]],
                          [in_window, None], sink_ref[kh, gi]))
    res = _half_head_attention(heads, SWA_HEAD_BATCH)
    o_ref[...] = jnp.concatenate([jnp.where(low, res[2 * p], res[2 * p + 1]) for p in range(len(res) // 2)],
                                 axis=-1).astype(BF16)


def _swa_attn(u, sink, kc, vc, cos, sin, l, *, n_ctx, db, t_lat, qcol, kcol, vcol, hkv, g):
    qb = SWA_QBLOCK
    nq = t_lat // qb
    wq = hkv * g * HEAD_DIM
    wk = hkv * HEAD_DIM
    assert g * HEAD_DIM == wk, "rotary tables are shared between the q groups and k"
    p = kc.shape[2]
    kern = functools.partial(_swa_kernel, hkv=hkv, g=g, t_lat=t_lat)
    return pl.pallas_call(
        kern,
        grid=(db, nq),
        in_specs=[
            pl.BlockSpec(memory_space=pltpu.SMEM),
            pl.BlockSpec((qb, wq), lambda b, i: (n_ctx // qb + b * nq + i, qcol)),
            pl.BlockSpec((t_lat, wk), lambda b, i: (n_ctx // t_lat + b, kcol)),
            pl.BlockSpec((t_lat, wk), lambda b, i: (n_ctx // t_lat + b, vcol)),
            pl.BlockSpec((1, 1, p, wk), lambda b, i: (b, l, 0, 0)),
            pl.BlockSpec((1, 1, p, wk), lambda b, i: (b, l, 0, 0)),
            pl.BlockSpec((t_lat, wk), lambda b, i: (0, 0)),
            pl.BlockSpec((t_lat, wk), lambda b, i: (0, 0)),
        ],
        out_specs=pl.BlockSpec((qb, wq), lambda b, i: (b * nq + i, 0)),
        out_shape=jax.ShapeDtypeStruct((db * t_lat, wq), BF16),
        scratch_shapes=[pltpu.VMEM((2 * hkv, t_lat, LANES), BF16), pltpu.VMEM((2 * hkv, t_lat, LANES), BF16),
                        pltpu.VMEM((2 * hkv, p, LANES), BF16), pltpu.VMEM((2 * hkv, p, LANES), BF16)],
        compiler_params=_cparams(2),
        name="swa_attn",
    )(sink, u, u, u, kc, vc, cos, sin)


def _rope_tables(t_lat, width):
    quarter = HEAD_DIM // 4
    pos = np.arange(t_lat)
    freqs = ROPE_BASE ** (-np.arange(quarter, dtype=np.float32) / quarter)
    ang_r = (pos // GRID_W).astype(np.float32)[:, None] * freqs[None, :]
    ang_c = (pos % GRID_W).astype(np.float32)[:, None] * freqs[None, :]
    ang_r, ang_c = jnp.asarray(ang_r, F32), jnp.asarray(ang_c, F32)
    cos = jnp.concatenate([jnp.cos(ang_r), jnp.cos(ang_r), jnp.cos(ang_c), jnp.cos(ang_c)], axis=-1)
    sin = jnp.concatenate([-jnp.sin(ang_r), jnp.sin(ang_r), -jnp.sin(ang_c), jnp.sin(ang_c)], axis=-1)
    reps = width // HEAD_DIM
    return jnp.tile(cos, (1, reps)), jnp.tile(sin, (1, reps))


def _log_sigmoid(x):
    return jnp.minimum(x, 0.0) - jnp.log(1.0 + jnp.exp(-jnp.abs(x)))


def _gla_kernel(*refs, heads, dk, dv, seq_len, has_s0, out_state):
    refs = list(refs)
    ub_ref, z_ref, gw_ref, gb_ref, gn_ref = refs[:5]
    refs = refs[5:]
    s0_ref = refs.pop(0) if has_s0 else None
    if out_state:
        refs.pop(0)
    o_ref = refs.pop(0)
    st_ref = refs.pop(0) if out_state else None
    la_scr, o_scr, st_scr = refs
    c_len = GLA_CHUNK
    nc = seq_len // c_len
    wk = heads * dk
    wv = heads * dv

    z_hi, z_lo = _split_bf16(z_ref[...])
    for d in range(2):
        g_hi, g_lo = _split_bf16(gw_ref[d])
        pre = _dot(z_hi, g_hi) + (_dot(z_hi, g_lo) + _dot(z_lo, g_hi)) + gb_ref[d]
        la_scr[d] = _log_sigmoid(pre) / GLA_TAU

    eye_v = (lax.broadcasted_iota(jnp.int32, (dv, dv), 0) == lax.broadcasted_iota(jnp.int32, (dv, dv), 1)).astype(F32)
    for d in range(2):
        for h in range(heads):
            if has_s0:
                s0_rows = jnp.concatenate(
                    [s0_ref[0, 0, d, h] if hh == h else jnp.zeros((dk, dv), F32) for hh in range(heads)], axis=0)
                st_scr[d, h] = _dot_nt(eye_v, s0_rows, precision=HIGHEST)
            else:
                st_scr[d, h] = jnp.zeros((dv, wk), F32)

    ng = GLA_GROUP
    gr = ng * c_len
    ti = lax.broadcasted_iota(jnp.int32, (gr, gr), 0)
    si = lax.broadcasted_iota(jnp.int32, (gr, gr), 1)
    same_chunk = (ti // c_len) == (si // c_len)
    lane_head = lax.broadcasted_iota(jnp.int32, (gr, wk), 1) // dk
    row_chunk = lax.broadcasted_iota(jnp.int32, (gr, dv), 0) // c_len

    def group(gi, d):
        reverse = d == 1
        keep = same_chunk & ((si >= ti) if reverse else (si <= ti))
        sums = jnp.concatenate([keep.astype(BF16), same_chunk.astype(BF16)], axis=0)
        rows = pl.ds(pl.multiple_of(gi * gr, gr), gr)
        la_hi, la_lo = _split_bf16(la_scr[d, rows, :])
        res = _dot(sums, la_hi) + _dot(sums, la_lo)
        b, b_end = res[:gr], res[gr:]
        q = ub_ref[rows, 0:wk].astype(F32) * (dk ** -0.5)
        k = ub_ref[rows, wk:2 * wk].astype(F32)
        qt = q * jnp.exp(b)
        kt = (k * jnp.exp(-b)).astype(BF16)
        ke = (k * jnp.exp(b_end - b)).astype(BF16)
        dec = jnp.exp(b_end)
        order = range(ng - 1, -1, -1) if reverse else range(ng)
        for h in range(heads):
            vs = slice(h * dv, (h + 1) * dv)
            qm = jnp.where(lane_head == h, qt, 0.0).astype(BF16)
            a = jnp.where(keep, _dot_nt(qm, kt), 0.0).astype(BF16)
            v_h = ub_ref[rows, 2 * wk + h * dv:2 * wk + (h + 1) * dv]
            o = _dot(a, v_h)
            v_f = v_h.astype(F32)
            v_blk = jnp.concatenate([jnp.where(row_chunk == c, v_f, 0.0).astype(BF16) for c in range(ng)], axis=1)
            ut = _dot_tn(v_blk, ke)
            st = st_scr[d, h]
            inter = [None] * ng
            for c in order:
                inter[c] = _dot_nt(qm[c * c_len:(c + 1) * c_len], st.astype(BF16))
                st = st * dec[c * c_len:c * c_len + 1, :] + ut[c * dv:(c + 1) * dv]
            st_scr[d, h] = st
            o = o + jnp.concatenate(inter, axis=0)
            if reverse:
                o_scr[rows, vs] += o
            else:
                o_scr[rows, vs] = o

    n_groups = seq_len // gr
    lax.fori_loop(0, n_groups, lambda g, carry: (group(g, 0), carry)[1], 0)
    lax.fori_loop(0, n_groups, lambda g, carry: (group(n_groups - 1 - g, 1), carry)[1], 0)

    r = ub_ref[:, 2 * wk + wv:2 * wk + 2 * wv].astype(F32)
    gn = gn_ref[...]
    for h in range(heads):
        vs = slice(h * dv, (h + 1) * dv)
        o_ref[:, vs] = (_rmsnorm(o_scr[:, vs], gn[:, vs]) * _silu(r[:, vs])).astype(BF16)
    if out_state:
        for d in range(2):
            for h in range(heads):
                st_ref[0, 0, d, h] = st_scr[d, h].T[h * dk:(h + 1) * dk, :]


def _gla(u, z, gw, gb, gn, s0, st_buf, l, *, row0, nb, seq_len, bcol, heads, dk, dv):
    wk, wv = heads * dk, heads * dv
    wb = 2 * wk + 2 * wv
    has_s0 = s0 is not None
    out_state = st_buf is not None
    rb = row0 // seq_len
    kern = functools.partial(_gla_kernel, heads=heads, dk=dk, dv=dv, seq_len=seq_len, has_s0=has_s0,
                             out_state=out_state)
    in_specs = [
        pl.BlockSpec((seq_len, wb), lambda b: (rb + b, bcol)),
        pl.BlockSpec((seq_len, LANES), lambda b: (rb + b, 0)),
        pl.BlockSpec((2, LANES, wk), lambda b: (0, 0, 0)),
        pl.BlockSpec((2, 1, wk), lambda b: (0, 0, 0)),
        pl.BlockSpec((1, wv), lambda b: (0, 0)),
    ]
    args = [u, z, gw, gb, gn]
    if has_s0:
        in_specs.append(pl.BlockSpec((1, 1, 2, heads, dk, dv), lambda b: (b, l, 0, 0, 0, 0)))
        args.append(s0)
    out_specs = [pl.BlockSpec((seq_len, wv), lambda b: (b, 0))]
    out_shape = [jax.ShapeDtypeStruct((nb * seq_len, wv), BF16)]
    aliases = {}
    if out_state:
        aliases = {len(args): 1}
        in_specs.append(pl.BlockSpec(memory_space=pl.ANY))
        args.append(st_buf)
        out_specs.append(pl.BlockSpec((1, 1, 2, heads, dk, dv), lambda b: (b, l, 0, 0, 0, 0)))
        out_shape.append(jax.ShapeDtypeStruct(st_buf.shape, st_buf.dtype))
    res = pl.pallas_call(
        kern,
        grid=(nb,),
        in_specs=in_specs,
        out_specs=out_specs,
        out_shape=out_shape,
        scratch_shapes=[
            pltpu.VMEM((2, seq_len, wk), F32),
            pltpu.VMEM((seq_len, wv), F32),
            pltpu.VMEM((2, heads, dv, wk), F32),
        ],
        input_output_aliases=aliases,
        compiler_params=_cparams(1),
        name="gla_lat" if has_s0 else "gla_ctx",
    )(*args)
    return res


ROUTE_I1, ROUTE_I2, ROUTE_W1, ROUTE_W2, ROUTE_R1, ROUTE_R2 = range(6)


def _pack_bf16_pairs(h):
    c = h.shape[1] // 2
    lo = lax.bitcast_convert_type(h[:, :c].astype(BF16).astype(F32), jnp.uint32)
    hi = lax.bitcast_convert_type(h[:, c:].astype(BF16).astype(F32), jnp.uint32)
    return hi | (lo >> 16)


def _unpack_bf16_pairs(w):
    lo = lax.bitcast_convert_type(w << 16, F32).astype(BF16)
    hi = lax.bitcast_convert_type(w & jnp.uint32(0xFFFF0000), F32).astype(BF16)
    return jnp.concatenate([lo, hi], axis=-1)


def _outproj_kernel(*refs, n_experts, top_k, nct, split_x):
    refs = list(refs)
    x_ref = refs.pop(0)
    xl_ref = refs.pop(0) if split_x else None
    ctx_refs, lat_refs, refs = refs[0:3], refs[3:6], refs[6:]
    if n_experts:
        w_ref, mod_ref, g_ref, wr_ref, br_ref, xn_ref, h_ref, route_ref, counts_ref, run_scr = refs
    else:
        w_ref, mod_ref, g_ref, xn_ref, h_ref = refs
    is_ctx = pl.program_id(0) < nct
    mixed = jnp.concatenate([jnp.where(is_ctx, c[...], t[...]) for c, t in zip(ctx_refs, lat_refs)], axis=-1)
    x = jnp.where(is_ctx, x_ref[...], xl_ref[...]) if split_x else x_ref[...]
    xn = x + mod_ref[0, 2:3, :] * _dot(mixed, w_ref[...])
    xn_ref[...] = xn
    h = _rmsnorm(xn, g_ref[...]) * (1.0 + mod_ref[0, 4:5, :]) + mod_ref[0, 3:4, :]
    if not n_experts:
        h_ref[...] = h.astype(BF16)
        return
    assert top_k == 2
    h_ref[...] = _pack_bf16_pairs(h)
    h_hi = h.astype(BF16)
    h_lo = (h - h_hi.astype(F32)).astype(BF16)
    wr = wr_ref[...]
    w_hi = wr.astype(BF16)
    w_lo = (wr - w_hi.astype(F32)).astype(BF16)
    logits = _dot(h_hi, w_hi) + (_dot(h_hi, w_lo) + _dot(h_lo, w_hi)) + br_ref[...]
    tm = logits.shape[0]
    lane = lax.broadcasted_iota(jnp.int32, logits.shape, 1)
    l1 = jnp.max(logits, axis=-1, keepdims=True)
    i1 = jnp.min(jnp.where(logits == l1, lane, LANES), axis=-1, keepdims=True)
    rest = jnp.where(lane == i1, -jnp.inf, logits)
    l2 = jnp.max(rest, axis=-1, keepdims=True)
    i2 = jnp.min(jnp.where(rest == l2, lane, LANES), axis=-1, keepdims=True)
    e2 = jnp.exp(l2 - l1)
    w1 = 1.0 / (1.0 + e2)
    w2 = e2 * w1
    @pl.when(pl.program_id(0) == 0)
    def _():
        run_scr[...] = jnp.zeros(run_scr.shape, F32)

    oh1 = lane == i1
    oh2 = lane == i2
    earlier = (lax.broadcasted_iota(jnp.int32, (tm, tm), 1) < lax.broadcasted_iota(jnp.int32, (tm, tm), 0)).astype(BF16)
    c1 = _dot(earlier, oh1.astype(BF16))
    c2 = _dot(earlier, oh2.astype(BF16))
    n1 = jnp.sum(oh1.astype(F32), axis=0, keepdims=True)
    n2 = jnp.sum(oh2.astype(F32), axis=0, keepdims=True)
    run = run_scr[0:1, :]
    r1 = jnp.sum(jnp.where(oh1, run + c1, 0.0), axis=-1, keepdims=True)
    r2 = jnp.sum(jnp.where(oh2, run + n1 + c2, 0.0), axis=-1, keepdims=True)
    total = run + n1 + n2
    run_scr[...] = jnp.broadcast_to(total, run_scr.shape)
    counts_ref[...] = jnp.broadcast_to(total, counts_ref.shape)
    rec = jnp.zeros(logits.shape, F32)
    for slot, val in ((ROUTE_I1, i1.astype(F32)), (ROUTE_I2, i2.astype(F32)), (ROUTE_W1, w1), (ROUTE_W2, w2),
                      (ROUTE_R1, r1), (ROUTE_R2, r2)):
        rec = jnp.where(lane == slot, val, rec)
    route_ref[...] = rec


def _outproj(x, x_lat, mixed_ctx, mixed_lat, w_out, l, mod_l, g, router, *, n_ctx, t_lat, tm=512):
    split_x = x_lat is not None
    d = x.shape[1]
    n = x.shape[0] + (x_lat.shape[0] if split_x else 0)
    nct = n_ctx // tm
    nlt = n // tm - nct

    def cond_idx(i):
        return jnp.where(i < nct, 0, 1 + ((i - nct) * tm) // t_lat)

    row = lambda i: (i, 0)
    const = lambda i: (0, 0)
    ctx_row = lambda i: (jnp.minimum(i, nct - 1), 0)
    lat_row = lambda i: (jnp.clip(i - nct, 0, nlt - 1), 0)
    x_specs = [pl.BlockSpec((tm, d), ctx_row), pl.BlockSpec((tm, d), lat_row)] if split_x else [pl.BlockSpec((tm, d), row)]
    x_args = [x, x_lat] if split_x else [x]
    in_specs = (
        x_specs
        + [pl.BlockSpec((tm, o.shape[1]), ctx_row) for o in mixed_ctx]
        + [pl.BlockSpec((tm, o.shape[1]), lat_row) for o in mixed_lat]
        + [pl.BlockSpec((None, d, d), lambda i: (l, 0, 0)), pl.BlockSpec((1, 6, d), lambda i: (cond_idx(i), 0, 0)),
           pl.BlockSpec((1, d), const)]
    )
    args = [*x_args, *mixed_ctx, *mixed_lat, w_out, mod_l, g]
    n_experts = 0
    scratch = []
    if router is None:
        out_specs = [pl.BlockSpec((tm, d), row), pl.BlockSpec((tm, d), row)]
        out_shape = [jax.ShapeDtypeStruct((n, d), F32), jax.ShapeDtypeStruct((n, d), BF16)]
    else:
        wr, br, n_experts = router
        in_specs += [pl.BlockSpec((d, LANES), const), pl.BlockSpec((1, LANES), const)]
        args += [wr, br]
        out_specs = [pl.BlockSpec((tm, d), row), pl.BlockSpec((tm, d // 2), row), pl.BlockSpec((tm, LANES), row),
                     pl.BlockSpec((8, LANES), const)]
        out_shape = [jax.ShapeDtypeStruct((n, d), F32), jax.ShapeDtypeStruct((n, d // 2), jnp.uint32),
                     jax.ShapeDtypeStruct((n, LANES), F32), jax.ShapeDtypeStruct((8, LANES), F32)]
        scratch = [pltpu.VMEM((8, LANES), F32)]
    kern = functools.partial(_outproj_kernel, n_experts=n_experts, top_k=2, nct=nct, split_x=split_x)
    return pl.pallas_call(
        kern,
        grid=(n // tm,),
        in_specs=in_specs,
        out_specs=out_specs,
        out_shape=out_shape,
        scratch_shapes=scratch,
        compiler_params=_cparams(1),
        name="outproj_router" if n_experts else "outproj",
    )(*args)


def _row_copy(src_ref, src_row, dst_ref, dst_row, sem):
    return pltpu.make_async_copy(src_ref.at[pl.ds(src_row, 1), :], dst_ref.at[pl.ds(dst_row, 1), :], sem)


def _dispatch_kernel(pos_ref, ztile_ref, zon_ref, hp_ref, hs_ref, zbuf, sem, zsem, *, n, tm):
    base = pl.program_id(0) * tm

    @pl.when(pl.program_id(0) == 0)
    def _():
        zbuf[...] = jnp.zeros(zbuf.shape, zbuf.dtype)
        clears = [pltpu.make_async_copy(zbuf, hs_ref.at[pl.ds(pl.multiple_of(ztile_ref[t] * tm, tm), tm), :], zsem)
                  for t in range(ztile_ref.shape[0])]
        for t, cp in enumerate(clears):
            pl.when(zon_ref[t] == 1)(cp.start)
        for t, cp in enumerate(clears):
            pl.when(zon_ref[t] == 1)(cp.wait)

    def copies(k):
        return (_row_copy(hp_ref, k, hs_ref, pos_ref[base + k], sem),
                _row_copy(hp_ref, k, hs_ref, pos_ref[n + base + k], sem))

    def start(k, carry):
        for queue, cp in enumerate(copies(k)):
            cp.start(priority=queue)
        return carry

    def wait(k, carry):
        for cp in copies(k):
            cp.wait()
        return carry

    lax.fori_loop(0, tm, start, 0, unroll=8)
    lax.fori_loop(0, tm, wait, 0, unroll=8)


def _dispatch(pos, clear_tiles, clear_on, hp, n_rows_sorted, tm=512):
    n, c = hp.shape
    return pl.pallas_call(
        functools.partial(_dispatch_kernel, n=n, tm=tm),
        grid_spec=pltpu.PrefetchScalarGridSpec(
            num_scalar_prefetch=3,
            grid=(n // tm,),
            in_specs=[pl.BlockSpec((tm, c), lambda i, pos, zt, zo: (i, 0))],
            out_specs=pl.BlockSpec(memory_space=pl.ANY),
            scratch_shapes=[pltpu.VMEM((tm, c), hp.dtype), pltpu.SemaphoreType.DMA(()), pltpu.SemaphoreType.DMA(())],
        ),
        out_shape=jax.ShapeDtypeStruct((n_rows_sorted, c), hp.dtype),
        compiler_params=_cparams(1),
        name="moe_dispatch",
    )(pos, clear_tiles, clear_on, hp)


def _moe_ffn_kernel(texp_ref, trow_ref, nact_ref, hs_ref, wg_ref, wu_ref, wd_ref, ys_ref, h_scr, acc_scr):
    del texp_ref, trow_ref
    i = pl.program_id(0)
    f = pl.program_id(1)
    last = pl.num_programs(1) - 1
    active = i < nact_ref[0]

    @pl.when(active & (f == 0))
    def _():
        h_scr[...] = _unpack_bf16_pairs(hs_ref[...])

    def contribution():
        h = h_scr[...]
        t = _silu(_dot(h, wg_ref[...])) * _dot(h, wu_ref[...])
        return _dot(t.astype(BF16), wd_ref[...])

    @pl.when(active & (f == 0))
    def _():
        acc_scr[...] = contribution()

    @pl.when(active & (f > 0) & (f < last))
    def _():
        acc_scr[...] += contribution()

    @pl.when(active & (f == last))
    def _():
        ys_ref[...] = _pack_bf16_pairs(acc_scr[...] + contribution())

    @pl.when(jnp.logical_not(active) & (f == 0))
    def _():
        ys_ref[...] = jnp.zeros(ys_ref.shape, ys_ref.dtype)


def _moe_ffn(tile_exp, tile_row, n_active, hs, wg, wu, wd, li, *, tm, nf=2):
    r, c = hs.shape
    d, ff = wg.shape[2:]
    tf = ff // nf
    assert nf >= 2, "the first and the last hidden tile are handled by different branches"

    def f_eff(i, f, nact):
        return jnp.where(i < nact[0], f, nf - 1)

    return pl.pallas_call(
        _moe_ffn_kernel,
        grid_spec=pltpu.PrefetchScalarGridSpec(
            num_scalar_prefetch=3,
            grid=(r // tm, nf),
            in_specs=[
                pl.BlockSpec((tm, c), lambda i, f, te, tr, na: (tr[i], 0)),
                pl.BlockSpec((None, None, d, tf), lambda i, f, te, tr, na: (li, te[i], 0, f_eff(i, f, na))),
                pl.BlockSpec((None, None, d, tf), lambda i, f, te, tr, na: (li, te[i], 0, f_eff(i, f, na))),
                pl.BlockSpec((None, None, tf, d), lambda i, f, te, tr, na: (li, te[i], f_eff(i, f, na), 0)),
            ],
            out_specs=pl.BlockSpec((tm, d // 2), lambda i, f, te, tr, na: (i, 0)),
            scratch_shapes=[pltpu.VMEM((tm, d), BF16), pltpu.VMEM((tm, d), F32)],
        ),
        out_shape=jax.ShapeDtypeStruct((r, d // 2), jnp.uint32),
        compiler_params=_cparams(2),
        name="moe_ffn",
    )(tile_exp, tile_row, n_active, hs, wg, wu, wd)


def _combine_kernel(*refs, n, tm, final_norm, nct):
    refs = list(refs)
    pos_ref, ys_ref, x_ref, route_ref, mod_ref = refs[:5]
    refs = refs[5:]
    fg_ref = refs.pop(0) if final_norm else None
    o_ref = refs.pop(0)
    ol_ref = refs.pop(0) if final_norm else None
    buf, sem = refs
    base = pl.program_id(0) * tm

    def copies(k):
        return (_row_copy(ys_ref, pos_ref[base + k], buf.at[0], k, sem),
                _row_copy(ys_ref, pos_ref[n + base + k], buf.at[1], k, sem))

    def start(k, carry):
        for queue, cp in enumerate(copies(k)):
            cp.start(priority=queue)
        return carry

    def wait(k, carry):
        for cp in copies(k):
            cp.wait()
        return carry

    lax.fori_loop(0, tm, start, 0, unroll=8)
    lax.fori_loop(0, tm, wait, 0, unroll=8)
    route = route_ref[...]
    lane = lax.broadcasted_iota(jnp.int32, route.shape, 1)
    w1 = jnp.sum(jnp.where(lane == ROUTE_W1, route, 0.0), axis=-1, keepdims=True)
    w2 = jnp.sum(jnp.where(lane == ROUTE_W2, route, 0.0), axis=-1, keepdims=True)
    y1 = _unpack_bf16_pairs(buf[0]).astype(F32)
    y2 = _unpack_bf16_pairs(buf[1]).astype(F32)
    y = x_ref[...] + mod_ref[0, 5:6, :] * (w1 * y1 + w2 * y2)
    if not final_norm:
        o_ref[...] = y
        return
    y = _rmsnorm(y, fg_ref[...])
    is_ctx = pl.program_id(0) < nct

    @pl.when(is_ctx)
    def _():
        o_ref[...] = y

    @pl.when(jnp.logical_not(is_ctx))
    def _():
        ol_ref[...] = y


def _combine(pos, ys, x, route, mod_l, final_g, *, n_ctx, t_lat, tm=512):
    n, d = x.shape
    nct = n_ctx // tm
    nlt = n // tm - nct

    def cond_idx(i):
        return jnp.where(i < nct, 0, 1 + ((i - nct) * tm) // t_lat)

    in_specs = [
        pl.BlockSpec(memory_space=pl.ANY),
        pl.BlockSpec((tm, d), lambda i, pos: (i, 0)),
        pl.BlockSpec((tm, LANES), lambda i, pos: (i, 0)),
        pl.BlockSpec((1, 6, d), lambda i, pos: (cond_idx(i), 0, 0)),
    ]
    args = [ys, x, route, mod_l]
    if final_g is None:
        out_specs = pl.BlockSpec((tm, d), lambda i, pos: (i, 0))
        out_shape = jax.ShapeDtypeStruct((n, d), F32)
    else:
        in_specs.append(pl.BlockSpec((1, d), lambda i, pos: (0, 0)))
        args.append(final_g)
        out_specs = [pl.BlockSpec((tm, d), lambda i, pos: (jnp.minimum(i, nct - 1), 0)),
                     pl.BlockSpec((tm, d), lambda i, pos: (jnp.clip(i - nct, 0, nlt - 1), 0))]
        out_shape = [jax.ShapeDtypeStruct((n_ctx, d), F32), jax.ShapeDtypeStruct((n - n_ctx, d), F32)]
    return pl.pallas_call(
        functools.partial(_combine_kernel, n=n, tm=tm, final_norm=final_g is not None, nct=nct),
        grid_spec=pltpu.PrefetchScalarGridSpec(
            num_scalar_prefetch=1,
            grid=(n // tm,),
            in_specs=in_specs,
            out_specs=out_specs,
            scratch_shapes=[pltpu.VMEM((2, tm, d // 2), jnp.uint32), pltpu.SemaphoreType.DMA(())],
        ),
        out_shape=out_shape,
        compiler_params=_cparams(1),
        name="moe_combine",
    )(pos, *args)


def _routing_tables(route, counts, n_experts, tm):
    n = route.shape[0]
    n_tiles = 2 * n // tm + n_experts
    cnt = counts[0, :n_experts].astype(jnp.int32)
    tiles_e = (cnt + tm - 1) // tm
    tile_end = jnp.cumsum(tiles_e)
    offs = (tile_end - tiles_e) * tm
    i1 = route[:, ROUTE_I1].astype(jnp.int32)
    i2 = route[:, ROUTE_I2].astype(jnp.int32)
    eidx = jnp.arange(n_experts, dtype=jnp.int32)[None, :]
    off1 = jnp.sum(jnp.where(i1[:, None] == eidx, offs[None, :], 0), axis=1)
    off2 = jnp.sum(jnp.where(i2[:, None] == eidx, offs[None, :], 0), axis=1)
    pos = jnp.concatenate([off1 + route[:, ROUTE_R1].astype(jnp.int32), off2 + route[:, ROUTE_R2].astype(jnp.int32)])
    n_active = tile_end[-1]
    t = jnp.arange(n_tiles, dtype=jnp.int32)
    t_eff = jnp.minimum(t, n_active - 1)
    tile_exp = jnp.sum((t_eff[:, None] >= tile_end[None, :]).astype(jnp.int32), axis=1)
    tails = jnp.maximum(tile_end - 1, 0)
    trailing = n_tiles - 1 - jnp.arange(n_experts, dtype=jnp.int32)
    clear_tiles = jnp.concatenate([tails, trailing]).astype(jnp.int32)
    clear_on = jnp.concatenate([tiles_e > 0, trailing >= n_active]).astype(jnp.int32)
    return (pos, clear_tiles, clear_on, tile_exp.astype(jnp.int32), t_eff.astype(jnp.int32),
            n_active.reshape(1).astype(jnp.int32), n_tiles)


def _ffn_kernel(*refs, final_norm):
    refs = list(refs)
    h_ref, x_ref, wg_ref, wu_ref, wd_ref, mod_ref = refs[:6]
    refs = refs[6:]
    fg_ref = refs.pop(0) if final_norm else None
    (o_ref,) = refs
    f = pl.program_id(1)

    @pl.when(f == 0)
    def _():
        o_ref[...] = jnp.zeros(o_ref.shape, F32)

    h = h_ref[...]
    t = _silu(_dot(h, wg_ref[...])) * _dot(h, wu_ref[...])
    o_ref[...] += _dot(t.astype(BF16), wd_ref[...])

    @pl.when(f == pl.num_programs(1) - 1)
    def _():
        y = x_ref[...] + mod_ref[0, 5:6, :] * o_ref[...]
        if final_norm:
            y = _rmsnorm(y, fg_ref[...])
        o_ref[...] = y


def _ffn(h, x, wg, wu, wd, li, mod_l, final_g, *, n_ctx, t_lat, tm=512):
    n, d = x.shape
    tf = FFN_TF
    nf = wg.shape[2] // tf
    nct = n_ctx // tm

    def cond_idx(i):
        return jnp.where(i < nct, 0, 1 + ((i - nct) * tm) // t_lat)

    row = lambda i, f: (i, 0)
    in_specs = [
        pl.BlockSpec((tm, d), row),
        pl.BlockSpec((tm, d), row),
        pl.BlockSpec((None, d, tf), lambda i, f: (li, 0, f)),
        pl.BlockSpec((None, d, tf), lambda i, f: (li, 0, f)),
        pl.BlockSpec((None, tf, d), lambda i, f: (li, f, 0)),
        pl.BlockSpec((1, 6, d), lambda i, f: (cond_idx(i), 0, 0)),
    ]
    args = [h, x, wg, wu, wd, mod_l]
    if final_g is not None:
        in_specs.append(pl.BlockSpec((1, d), lambda i, f: (0, 0)))
        args.append(final_g)
    return pl.pallas_call(
        functools.partial(_ffn_kernel, final_norm=final_g is not None),
        grid=(n // tm, nf),
        in_specs=in_specs,
        out_specs=pl.BlockSpec((tm, d), row),
        out_shape=jax.ShapeDtypeStruct((n, d), F32),
        compiler_params=_cparams(2),
        name="ffn_dense",
    )(*args)


def kernel(x_prompt, x_sample, cache_nat_k, cache_nat_v, state_gla, cache_swa_k, cache_swa_v, c, c_ctx, w_ada, b_ada,
           norm_mix_g, norm_ffn_g, w_in, na_rpb, gla_w_gate, gla_b_gate, gla_norm_g, swa_sink, w_out, ffn_w_gate,
           ffn_w_up, ffn_w_down, moe_w_router, moe_b_router, moe_w_gate, moe_w_up, moe_w_down, final_norm_g):
    nb, seq, d = x_prompt.shape
    db, t_lat, _ = x_sample.shape
    depth = w_in.shape[0]
    p_len, h_a = cache_nat_k.shape[2], cache_nat_k.shape[3]
    h_b, dk_b, dv_b = state_gla.shape[3:6]
    hkv_c, g_c = swa_sink.shape[1], swa_sink.shape[2]
    rank = gla_w_gate.shape[2]
    n_exp = moe_w_router.shape[-1]
    w_a = h_a * HEAD_DIM
    w_bk = h_b * dk_b
    w_bv = h_b * dv_b
    w_cq = hkv_c * g_c * HEAD_DIM
    w_ck = hkv_c * HEAD_DIM
    n_ctx = nb * seq
    n_lat = db * t_lat
    n = n_ctx + n_lat
    z0 = 3 * w_a + 2 * w_bk + 2 * w_bv
    z1 = z0 + 2 * rank
    a_w = 3 * w_a
    b_w = 2 * w_bk + 2 * w_bv
    assert a_w == b_w and (a_w + b_w) % w_cq == 0 and (a_w + b_w + w_cq) % w_ck == 0 and 2 * rank <= LANES
    assert db + 1 <= 8

    x, x_lat = x_prompt.reshape(n_ctx, d), x_sample.reshape(n_lat, d)
    cond8 = jnp.concatenate([c_ctx[None, :], c, jnp.zeros((7 - db, d), F32)], axis=0)
    mod = _adaln(cond8, w_ada, b_ada).reshape(depth, 8, 6, d)

    cos, sin = _rope_tables(t_lat, w_ck)
    na_tab = _na_bias_tables(na_rpb)
    kc_a = cache_nat_k.reshape(db, depth, p_len, w_a)
    vc_a = cache_nat_v.reshape(db, depth, p_len, w_a)
    kc_c = cache_swa_k.reshape(db, depth, p_len, w_ck)
    vc_c = cache_swa_v.reshape(db, depth, p_len, w_ck)
    sizes = dict(n_ctx=n_ctx, t_lat=t_lat)

    kv_bufs = [jnp.zeros((nb, depth, seq, w), F32) for w in (w_a, w_a, w_ck, w_ck)]
    gla_st = jnp.zeros((nb, depth, 2, h_b, dk_b, dv_b), F32)
    n_main = w_in.shape[2] - (z1 - z0)
    before_z = lax.broadcasted_iota(jnp.int32, (1, 1, n_main), 2) < z0
    w_main = jnp.where(before_z, w_in[:, :, :n_main], w_in[:, :, z1 - z0:]).astype(BF16)
    wz = jnp.pad(w_in[:, :, z0:z1], ((0, 0), (0, 0), (0, LANES - 2 * rank))).astype(BF16)
    w_out_b = w_out.astype(BF16)
    ffn_wg, ffn_wu, ffn_wd = ffn_w_gate.astype(BF16), ffn_w_up.astype(BF16), ffn_w_down.astype(BF16)
    moe_wg, moe_wu, moe_wd = moe_w_gate.astype(BF16), moe_w_up.astype(BF16), moe_w_down.astype(BF16)
    for l in range(depth):
        u, z, *kv_bufs = _inproj(x, x_lat, mod[l], norm_mix_g[l][None, :], w_main, wz, kv_bufs, l, seq=seq, w_a=w_a,
                                 w_cq=w_cq, w_kvc=w_ck, tm=INPROJ_TM_SPLIT if x_lat is not None else INPROJ_TM,
                                 **sizes)

        oa_c = _ctx_attn(u, None, nb=nb, seq=seq, qcol=0, kcol=1, vcol=2, wq=w_a, wk=w_a, hkv=h_a, g=1)
        oa_l = _na_attn(u, kc_a, vc_a, na_tab, l, n_ctx=n_ctx, db=db, t_lat=t_lat, w_a=w_a, heads=h_a)
        gw = jnp.zeros((2, LANES, w_bk), F32)
        gw = gw.at[0, 0:rank].set(gla_w_gate[l, 0]).at[1, rank:2 * rank].set(gla_w_gate[l, 1])
        gb = gla_b_gate[l][:, None, :]
        gn = gla_norm_g[l].reshape(1, w_bv)
        gla_args = dict(bcol=a_w // b_w, heads=h_b, dk=dk_b, dv=dv_b)
        ob_c, gla_st = _gla(u, z, gw, gb, gn, None, gla_st, l, row0=0, nb=nb, seq_len=seq, **gla_args)
        (ob_l,) = _gla(u, z, gw, gb, gn, state_gla, None, l, row0=n_ctx, nb=db, seq_len=t_lat, **gla_args)
        c0 = a_w + b_w
        cols = dict(qcol=c0 // w_cq, kcol=(c0 + w_cq) // w_ck, vcol=(c0 + w_cq) // w_ck + 1, hkv=hkv_c, g=g_c)
        oc_c = _ctx_attn(u, swa_sink[l], nb=nb, seq=seq, wq=w_cq, wk=w_ck, **cols)
        oc_l = _swa_attn(u, swa_sink[l], kc_c, vc_c, cos, sin, l, n_ctx=n_ctx, db=db, t_lat=t_lat, **cols)

        moe = l % 2 == 1
        i = l // 2
        router = None
        if moe:
            wr = jnp.pad(moe_w_router[i], ((0, 0), (0, LANES - n_exp)))
            br = jnp.concatenate([moe_b_router[i], jnp.full((LANES - n_exp,), NEG_INF, F32)])[None, :]
            router = (wr, br, n_exp)
        res = _outproj(x, x_lat, (oa_c, ob_c, oc_c), (oa_l, ob_l, oc_l), w_out_b, l, mod[l],
                       norm_ffn_g[l][None, :], router, **sizes)
        x_lat = None
        fg = final_norm_g[None, :] if l == depth - 1 else None
        if moe:
            xn, hp, route, counts = res
            tm_e = 512
            pos, clear_tiles, clear_on, tile_exp, tile_row, n_active, n_tiles = _routing_tables(route, counts, n_exp,
                                                                                                tm_e)
            hs = _dispatch(pos, clear_tiles, clear_on, hp, n_tiles * tm_e)
            ys = _moe_ffn(tile_exp, tile_row, n_active, hs, moe_wg, moe_wu, moe_wd, i, tm=tm_e)
            x = _combine(pos, ys, xn, route, mod[l], fg, **sizes)
        else:
            xn, h2 = res
            x = _ffn(h2, xn, ffn_wg, ffn_wu, ffn_wd, i, mod[l], fg, tm=FFN_TM, **sizes)

    y_ctx, y_lat = x if isinstance(x, (list, tuple)) else (x[:n_ctx], x[n_ctx:])
    nat_k, nat_v, swa_k, swa_v = kv_bufs
    return (y_ctx.reshape(nb, seq, d), y_lat.reshape(db, t_lat, d),
            nat_k.reshape(nb, depth, seq, h_a, HEAD_DIM), nat_v.reshape(nb, depth, seq, h_a, HEAD_DIM), gla_st,
            swa_k.reshape(nb, depth, seq, hkv_c, HEAD_DIM), swa_v.reshape(nb, depth, seq, hkv_c, HEAD_DIM))
```

```python
import functools

import numpy as np
import jax
import jax.numpy as jnp
from jax import lax
from jax.experimental import pallas as pl
from jax.experimental.pallas import tpu as pltpu

F32 = jnp.float32
BF16 = jnp.bfloat16
HIGHEST = lax.Precision.HIGHEST

EPS = 1e-6
NEG_INF = -1e30
GRID_W = 64
HEAD_DIM = 64
NA_ROWS = 8
NA_COLS = 16
SWA_WIN = 128
SWA_QBLOCK = 128
GLA_CHUNK = 64
GLA_GROUP = 4
GLA_TAU = 16.0
ROPE_BASE = 10000.0
LANES = 128
FFN_TF = 512
FFN_TM = 1024
MOE_TILE_ROWS = 384
INPROJ_TM = 1024
INPROJ_TM_SPLIT = 512
CTX_HEAD_BATCH = 16
NA_HEAD_BATCH = 4
SWA_HEAD_BATCH = 16
VMEM_LIMIT = 60 * 1024 * 1024


def _cparams(n_axes):
    return pltpu.CompilerParams(dimension_semantics=("arbitrary",) * n_axes, vmem_limit_bytes=VMEM_LIMIT)


def _dot(a, b, **kw):
    return jnp.dot(a, b, preferred_element_type=F32, **kw)


def _dot_nt(a, b, **kw):
    return lax.dot_general(a, b, (((1,), (1,)), ((), ())), preferred_element_type=F32, **kw)


def _dot_tn(a, b, **kw):
    return lax.dot_general(a, b, (((0,), (0,)), ((), ())), preferred_element_type=F32, **kw)


def _silu(x):
    return x / (1.0 + jnp.exp(-x))


def _split_bf16(x):
    hi = x.astype(BF16)
    return hi, (x - hi.astype(F32)).astype(BF16)


def _rmsnorm(x, g):
    return x * lax.rsqrt(jnp.mean(x * x, axis=-1, keepdims=True) + EPS) * g


def _adaln_kernel(cond_ref, w_ref, b_ref, o_ref):
    s = _silu(cond_ref[...]).astype(BF16)
    o_ref[0] = _dot(s, w_ref[0].astype(BF16)) + b_ref[0]


def _adaln(cond8, w_ada, b_ada, tn=1024):
    depth, d, n6 = w_ada.shape
    return pl.pallas_call(
        _adaln_kernel,
        grid=(depth, n6 // tn),
        in_specs=[
            pl.BlockSpec((8, d), lambda l, j: (0, 0)),
            pl.BlockSpec((1, d, tn), lambda l, j: (l, 0, j)),
            pl.BlockSpec((1, 1, tn), lambda l, j: (l, 0, j)),
        ],
        out_specs=pl.BlockSpec((1, 8, tn), lambda l, j: (l, 0, j)),
        out_shape=jax.ShapeDtypeStruct((depth, 8, n6), F32),
        compiler_params=_cparams(2),
        name="adaln",
    )(cond8, w_ada, b_ada.reshape(depth, 1, n6))


def _inproj_kernel(*refs, nct, seq, w_a, w_cq, w_kvc, split_x):
    refs = list(refs)
    x_ref = refs.pop(0)
    xl_ref = refs.pop(0) if split_x else None
    mod_ref, g_ref, w_ref, wz_ref = refs[:4]
    u_ref, z_ref, nk_ref, nv_ref, sk_ref, sv_ref, h_scr = refs[8:]
    i = pl.program_id(0)
    j = pl.program_id(1)
    is_ctx = i < nct

    @pl.when(j == 0)
    def _():
        x = jnp.where(is_ctx, x_ref[...], xl_ref[...]) if split_x else x_ref[...]
        h = _rmsnorm(x, g_ref[...]) * (1.0 + mod_ref[0, 1:2, :]) + mod_ref[0, 0:1, :]
        hb = h.astype(BF16)
        h_scr[...] = hb
        z_ref[...] = _dot(hb, wz_ref[...])

    acc = _dot(h_scr[...], w_ref[...])
    u_ref[...] = acc.astype(BF16)
    nseq = nk_ref.shape[0]

    @pl.when(is_ctx & (j == 0))
    def _():
        for bb in range(nseq):
            nk_ref[bb, 0] = acc[bb * seq:(bb + 1) * seq, w_a:2 * w_a]
            nv_ref[bb, 0] = acc[bb * seq:(bb + 1) * seq, 2 * w_a:3 * w_a]

    @pl.when(is_ctx & (j == 2))
    def _():
        for bb in range(nseq):
            sk_ref[bb, 0] = acc[bb * seq:(bb + 1) * seq, w_cq:w_cq + w_kvc]
            sv_ref[bb, 0] = acc[bb * seq:(bb + 1) * seq, w_cq + w_kvc:w_cq + 2 * w_kvc]


def _inproj(x, x_lat, mod_l, g, w_main, wz, kv_bufs, l, *, n_ctx, seq, t_lat, w_a, w_cq, w_kvc, tm=512):
    split_x = x_lat is not None
    d = x.shape[1]
    n = x.shape[0] + (x_lat.shape[0] if split_x else 0)
    n_main = w_main.shape[2]
    tn = 3 * w_a
    nct = n_ctx // tm
    nlt = n // tm - nct
    nseq = tm // seq
    assert n_ctx % tm == 0 and tm % seq == 0 and t_lat % tm == 0 and n_main == 3 * tn and w_cq + 2 * w_kvc == tn

    def cond_idx(i):
        return jnp.where(i < nct, 0, 1 + ((i - nct) * tm) // t_lat)

    kv_idx = lambda i, j: (jnp.minimum(i, nct - 1), l, 0, 0)
    if split_x:
        x_specs = [pl.BlockSpec((tm, d), lambda i, j: (jnp.minimum(i, nct - 1), 0)),
                   pl.BlockSpec((tm, d), lambda i, j: (jnp.clip(i - nct, 0, nlt - 1), 0))]
        x_args = [x, x_lat]
    else:
        x_specs = [pl.BlockSpec((tm, d), lambda i, j: (i, 0))]
        x_args = [x]
    n_in = len(x_args) + 4
    kern = functools.partial(_inproj_kernel, nct=nct, seq=seq, w_a=w_a, w_cq=w_cq, w_kvc=w_kvc, split_x=split_x)
    return pl.pallas_call(
        kern,
        grid=(n // tm, n_main // tn),
        in_specs=x_specs + [
            pl.BlockSpec((1, 6, d), lambda i, j: (cond_idx(i), 0, 0)),
            pl.BlockSpec((1, d), lambda i, j: (0, 0)),
            pl.BlockSpec((None, d, tn), lambda i, j: (l, 0, j)),
            pl.BlockSpec((None, d, LANES), lambda i, j: (l, 0, 0)),
        ] + [pl.BlockSpec(memory_space=pl.ANY)] * 4,
        out_specs=[
            pl.BlockSpec((tm, tn), lambda i, j: (i, j)),
            pl.BlockSpec((tm, LANES), lambda i, j: (i, 0)),
        ] + [pl.BlockSpec((nseq, 1, seq, b.shape[-1]), kv_idx) for b in kv_bufs],
        out_shape=[
            jax.ShapeDtypeStruct((n, n_main), BF16),
            jax.ShapeDtypeStruct((n, LANES), F32),
        ] + [jax.ShapeDtypeStruct(b.shape, b.dtype) for b in kv_bufs],
        scratch_shapes=[pltpu.VMEM((tm, d), BF16)],
        input_output_aliases={n_in + k: 2 + k for k in range(4)},
        compiler_params=_cparams(2),
        name="inproj",
    )(*x_args, mod_l, g, w_main, wz, *kv_bufs)


def _swap_halves(x):
    return pltpu.roll(x.astype(F32), LANES // 2, 1).astype(x.dtype)


def _half_head_attention(heads, batch):
    if len(heads) > batch:
        return _half_head_attention(heads[:batch], batch) + _half_head_attention(heads[batch:], batch)
    scores =[[_dot_nt(qm, kp) if fn is None else fn(_dot_nt(qm, kp)) for kp, fn in zip(keys, fns)]
              for qm, keys, _, fns, _ in heads]
    maxes = []
    for ss, (_, _, _, _, sink) in zip(scores, heads):
        m = jnp.max(ss[0], axis=-1, keepdims=True)
        for s in ss[1:]:
            m = jnp.maximum(m, jnp.max(s, axis=-1, keepdims=True))
        maxes.append(m if sink is None else jnp.maximum(m, sink))
    probs = [[jnp.exp(s - m).astype(BF16) for s in ss] for ss, m in zip(scores, maxes)]
    outs = []
    for ps, m, (_, _, values_aug, _, sink) in zip(probs, maxes, heads):
        acc = _dot(ps[0], values_aug[0])
        for p, va in zip(ps[1:], values_aug[1:]):
            acc = acc + _dot(p, va)
        den = pltpu.roll(acc, LANES // 2, 1)
        if sink is not None:
            den = den + jnp.exp(sink - m)
        outs.append(acc / den)
    return outs


def _ctx_attn_kernel(*refs, hkv, g, has_sink):
    if has_sink:
        sink_ref, q_ref, k_ref, v_ref, o_ref = refs
    else:
        q_ref, k_ref, v_ref, o_ref = refs
    q = q_ref[...] * jnp.asarray(HEAD_DIM ** -0.5, BF16)
    k = k_ref[...]
    v = v_ref[...]
    rows = q.shape[0]
    low = lax.broadcasted_iota(jnp.int32, (rows, LANES), 1) < HEAD_DIM
    kv_pairs = {}

    def kv_pair(kh, half):
        if (kh, half) not in kv_pairs:
            sl = slice((kh // 2) * LANES, (kh // 2 + 1) * LANES)
            kp, vp = k[:, sl], v[:, sl]
            if kh % 2 != half:
                kp, vp = _swap_halves(kp), _swap_halves(vp)
            kv_pairs[(kh, half)] = (kp, vp)
        return kv_pairs[(kh, half)]

    heads = []
    for hq in range(hkv * g):
        pair, half = hq // 2, hq % 2
        kh, gi = hq // g, hq % g
        qp = q[:, pair * LANES:(pair + 1) * LANES]
        own = low if half == 0 else jnp.logical_not(low)
        kp, vp = kv_pair(kh, half)
        sk = sink_ref[kh, gi] if has_sink else None
        heads.append((jnp.where(own, qp, 0), [kp], [jnp.where(own, vp, 1)], [None], sk))
    res = _half_head_attention(heads, CTX_HEAD_BATCH)
    o_ref[...] = jnp.concatenate([jnp.where(low, res[2 * p], res[2 * p + 1]) for p in range(len(res) // 2)],
                                 axis=-1).astype(BF16)


def _ctx_attn(u, sink, *, nb, seq, qcol, kcol, vcol, wq, wk, hkv, g):
    has_sink = sink is not None
    kern = functools.partial(_ctx_attn_kernel, hkv=hkv, g=g, has_sink=has_sink)
    in_specs = [
        pl.BlockSpec((seq, wq), lambda b: (b, qcol)),
        pl.BlockSpec((seq, wk), lambda b: (b, kcol)),
        pl.BlockSpec((seq, wk), lambda b: (b, vcol)),
    ]
    args = [u, u, u]
    if has_sink:
        in_specs = [pl.BlockSpec(memory_space=pltpu.SMEM)] + in_specs
        args = [sink] + args
    return pl.pallas_call(
        kern,
        grid=(nb,),
        in_specs=in_specs,
        out_specs=pl.BlockSpec((seq, wq), lambda b: (b, 0)),
        out_shape=jax.ShapeDtypeStruct((nb * seq, wq), BF16),
        compiler_params=_cparams(1),
        name="ctx_attn_sink" if has_sink else "ctx_attn",
    )(*args)


NA_QROWS = 4
NA_WIN_ROWS = 12


def _na_kernel(q_ref, k_ref, v_ref, kc_ref, vc_ref, bias_ref, o_ref, *, heads, rows):
    qi = pl.program_id(1)
    r0 = qi * NA_QROWS
    w0 = jnp.clip(r0 - NA_ROWS // 2, 0, rows - NA_WIN_ROWS)
    k0 = pl.multiple_of(w0 * GRID_W, NA_QROWS * GRID_W)
    tq = NA_QROWS * GRID_W
    tk = NA_WIN_ROWS * GRID_W
    q = q_ref[...] * jnp.asarray(HEAD_DIM ** -0.5, BF16)
    k = k_ref[pl.ds(k0, tk), :]
    v = v_ref[pl.ds(k0, tk), :]
    kc = kc_ref[0, 0].astype(BF16)
    vc = vc_ref[0, 0].astype(BF16)
    qrow = r0 + lax.broadcasted_iota(jnp.int32, (tq, tk), 0) // GRID_W
    krow = w0 + lax.broadcasted_iota(jnp.int32, (tq, tk), 1) // GRID_W
    band0 = jnp.clip(qrow - NA_ROWS // 2, 0, rows - NA_ROWS)
    in_band = (krow >= band0) & (krow < band0 + NA_ROWS)
    low_q = lax.broadcasted_iota(jnp.int32, (tq, LANES), 1) < HEAD_DIM
    low_k = lax.broadcasted_iota(jnp.int32, (tk, LANES), 1) < HEAD_DIM
    low_c = lax.broadcasted_iota(jnp.int32, (kc.shape[0], LANES), 1) < HEAD_DIM

    def window_scores(h):
        def fn(s):
            bias = jnp.concatenate(
                [jnp.concatenate([bias_ref[h, w0 + 2 * p - (r0 + rq) + 2 * NA_ROWS - 1]
                                  for p in range(NA_WIN_ROWS // 2)], axis=1) for rq in range(NA_QROWS)], axis=0)
            return jnp.where(in_band, s + bias, NEG_INF)
        return fn

    head_specs = []
    for h in range(heads):
        sl = slice((h // 2) * LANES, (h // 2 + 1) * LANES)
        first = h % 2 == 0
        own = lambda low: low if first else jnp.logical_not(low)
        head_specs.append((jnp.where(own(low_q), q[:, sl], 0), [k[:, sl], kc[:, sl]],
                           [jnp.where(own(low_k), v[:, sl], 1), jnp.where(own(low_c), vc[:, sl], 1)],
                           [window_scores(h), None], None))
    res = _half_head_attention(head_specs, NA_HEAD_BATCH)
    o_ref[...] = jnp.concatenate([jnp.where(low_q, res[2 * p], res[2 * p + 1]) for p in range(heads // 2)],
                                 axis=-1).astype(BF16)


def _na_attn(u, kc, vc, bias, l, *, n_ctx, db, t_lat, w_a, heads):
    tq = NA_QROWS * GRID_W
    rows = t_lat // GRID_W
    nq = t_lat // tq
    p = kc.shape[2]
    assert rows >= NA_WIN_ROWS and NA_WIN_ROWS >= NA_QROWS + NA_ROWS - 1 and (rows - NA_WIN_ROWS) % NA_QROWS == 0
    kern = functools.partial(_na_kernel, heads=heads, rows=rows)
    return pl.pallas_call(
        kern,
        grid=(db, nq),
        in_specs=[
            pl.BlockSpec((tq, w_a), lambda b, i: (n_ctx // tq + b * nq + i, 0)),
            pl.BlockSpec((t_lat, w_a), lambda b, i: (n_ctx // t_lat + b, 1)),
            pl.BlockSpec((t_lat, w_a), lambda b, i: (n_ctx // t_lat + b, 2)),
            pl.BlockSpec((1, 1, p, w_a), lambda b, i: (b, l, 0, 0)),
            pl.BlockSpec((1, 1, p, w_a), lambda b, i: (b, l, 0, 0)),
            pl.BlockSpec((None,) + bias.shape[1:], lambda b, i: (l, 0, 0, 0, 0)),
        ],
        out_specs=pl.BlockSpec((tq, w_a), lambda b, i: (b * nq + i, 0)),
        out_shape=jax.ShapeDtypeStruct((db * t_lat, w_a), BF16),
        compiler_params=_cparams(2),
        name="na_attn",
    )(u, u, u, kc, vc, bias)


def _na_bias_tables(rpb):
    col = np.arange(GRID_W)
    col_start = np.clip(col - NA_COLS // 2, 0, GRID_W - NA_COLS)
    col_ok = (col[None, :] >= col_start[:, None]) & (col[None, :] < col_start[:, None] + NA_COLS)
    dc = np.clip(col[None, :] - col[:, None] + NA_COLS - 1, 0, 2 * NA_COLS - 2)
    pick = (np.arange(2 * NA_COLS - 1)[:, None, None] == dc[None]).astype(np.float32)
    t1 = jnp.einsum("lhdj,jck->lhdck", rpb.astype(F32), jnp.asarray(pick), precision=HIGHEST)
    t1 = jnp.where(jnp.asarray(col_ok), t1, NEG_INF)
    t1 = jnp.pad(t1, ((0, 0), (0, 0), (NA_ROWS, NA_ROWS), (0, 0), (0, 0)), constant_values=NEG_INF)
    return jnp.concatenate([t1[:, :, :-1], t1[:, :, 1:]], axis=-1)


def _rope(x, cos, sin_signed):
    w = x.shape[-1]
    q4 = HEAD_DIM // 4
    lane = lax.broadcasted_iota(jnp.int32, x.shape, 1)
    first = (lane % (2 * q4)) < q4
    swapped = jnp.where(first, pltpu.roll(x, w - q4, 1), pltpu.roll(x, q4, 1))
    return x * cos + swapped * sin_signed


def _swa_kernel(sink_ref, q_ref, k_ref, v_ref, kc_ref, vc_ref, cos_ref, sin_ref, o_ref,
                ks_scr, vs_scr, kcs_scr, vcs_scr, *, hkv, g, t_lat):
    n = pl.program_id(1)
    qb = SWA_QBLOCK
    wk = hkv * HEAD_DIM
    span = 3 * qb

    @pl.when(n == 0)
    def _():
        k_r = _rope(k_ref[...].astype(F32), cos_ref[...], sin_ref[...]).astype(BF16)
        sources = ((k_r, ks_scr, False), (v_ref[...], vs_scr, True),
                   (kc_ref[0, 0].astype(BF16), kcs_scr, False), (vc_ref[0, 0].astype(BF16), vcs_scr, True))
        for src, dst, is_value in sources:
            low = lax.broadcasted_iota(jnp.int32, (src.shape[0], LANES), 1) < HEAD_DIM
            for kh in range(hkv):
                slab = src[:, (kh // 2) * LANES:(kh // 2 + 1) * LANES]
                for half in range(2):
                    x = slab if kh % 2 == half else _swap_halves(slab)
                    if is_value:
                        x = jnp.where(low if half == 0 else jnp.logical_not(low), x, 1)
                    dst[kh * 2 + half] = x

    q0 = pl.multiple_of(n * qb, qb)
    start = pl.multiple_of(jnp.clip(n * qb - qb, 0, t_lat - span), qb)
    cos_q = cos_ref[pl.ds(q0, qb), :]
    sin_q = sin_ref[pl.ds(q0, qb), :]
    qpos = q0 + lax.broadcasted_iota(jnp.int32, (qb, span), 0)
    kpos = start + lax.broadcasted_iota(jnp.int32, (qb, span), 1)
    valid = jnp.abs(qpos - kpos) <= SWA_WIN
    in_window = lambda s: jnp.where(valid, s, NEG_INF)
    low = lax.broadcasted_iota(jnp.int32, (qb, LANES), 1) < HEAD_DIM
    scale = jnp.asarray(HEAD_DIM ** -0.5, BF16)
    heads = []
    for kh in range(hkv):
        q_r = _rope(q_ref[:, kh * wk:(kh + 1) * wk].astype(F32), cos_q, sin_q).astype(BF16) * scale
        for gi in range(g):
            half = gi % 2
            idx = kh * 2 + half
            qp = q_r[:, (gi // 2) * LANES:(gi // 2 + 1) * LANES]
            heads.append((jnp.where(low if half == 0 else jnp.logical_not(low), qp, 0),
                          [ks_scr[idx, pl.ds(start, span), :], kcs_scr[idx]],
                          [vs_scr[idx, pl.ds(start, span), :], vcs_scr[idx]],
                          [in_window, None], sink_ref[kh, gi]))
    res = _half_head_attention(heads, SWA_HEAD_BATCH)
    o_ref[...] = jnp.concatenate([jnp.where(low, res[2 * p], res[2 * p + 1]) for p in range(len(res) // 2)],
                                 axis=-1).astype(BF16)


def _swa_attn(u, sink, kc, vc, cos, sin, l, *, n_ctx, db, t_lat, qcol, kcol, vcol, hkv, g):
    qb = SWA_QBLOCK
    nq = t_lat // qb
    wq = hkv * g * HEAD_DIM
    wk = hkv * HEAD_DIM
    assert g * HEAD_DIM == wk, "rotary tables are shared between the q groups and k"
    p = kc.shape[2]
    kern = functools.partial(_swa_kernel, hkv=hkv, g=g, t_lat=t_lat)
    return pl.pallas_call(
        kern,
        grid=(db, nq),
        in_specs=[
            pl.BlockSpec(memory_space=pltpu.SMEM),
            pl.BlockSpec((qb, wq), lambda b, i: (n_ctx // qb + b * nq + i, qcol)),
            pl.BlockSpec((t_lat, wk), lambda b, i: (n_ctx // t_lat + b, kcol)),
            pl.BlockSpec((t_lat, wk), lambda b, i: (n_ctx // t_lat + b, vcol)),
            pl.BlockSpec((1, 1, p, wk), lambda b, i: (b, l, 0, 0)),
            pl.BlockSpec((1, 1, p, wk), lambda b, i: (b, l, 0, 0)),
            pl.BlockSpec((t_lat, wk), lambda b, i: (0, 0)),
            pl.BlockSpec((t_lat, wk), lambda b, i: (0, 0)),
        ],
        out_specs=pl.BlockSpec((qb, wq), lambda b, i: (b * nq + i, 0)),
        out_shape=jax.ShapeDtypeStruct((db * t_lat, wq), BF16),
        scratch_shapes=[pltpu.VMEM((2 * hkv, t_lat, LANES), BF16), pltpu.VMEM((2 * hkv, t_lat, LANES), BF16),
                        pltpu.VMEM((2 * hkv, p, LANES), BF16), pltpu.VMEM((2 * hkv, p, LANES), BF16)],
        compiler_params=_cparams(2),
        name="swa_attn",
    )(sink, u, u, u, kc, vc, cos, sin)


def _rope_tables(t_lat, width):
    quarter = HEAD_DIM // 4
    pos = np.arange(t_lat)
    freqs = ROPE_BASE ** (-np.arange(quarter, dtype=np.float32) / quarter)
    ang_r = (pos // GRID_W).astype(np.float32)[:, None] * freqs[None, :]
    ang_c = (pos % GRID_W).astype(np.float32)[:, None] * freqs[None, :]
    ang_r, ang_c = jnp.asarray(ang_r, F32), jnp.asarray(ang_c, F32)
    cos = jnp.concatenate([jnp.cos(ang_r), jnp.cos(ang_r), jnp.cos(ang_c), jnp.cos(ang_c)], axis=-1)
    sin = jnp.concatenate([-jnp.sin(ang_r), jnp.sin(ang_r), -jnp.sin(ang_c), jnp.sin(ang_c)], axis=-1)
    reps = width // HEAD_DIM
    return jnp.tile(cos, (1, reps)), jnp.tile(sin, (1, reps))


def _log_sigmoid(x):
    return jnp.minimum(x, 0.0) - jnp.log(1.0 + jnp.exp(-jnp.abs(x)))


def _gla_kernel(*refs, heads, dk, dv, seq_len, has_s0, out_state):
    refs = list(refs)
    ub_ref, z_ref, gw_ref, gb_ref, gn_ref = refs[:5]
    refs = refs[5:]
    s0_ref = refs.pop(0) if has_s0 else None
    if out_state:
        refs.pop(0)
    o_ref = refs.pop(0)
    st_ref = refs.pop(0) if out_state else None
    la_scr, o_scr, st_scr = refs
    c_len = GLA_CHUNK
    nc = seq_len // c_len
    wk = heads * dk
    wv = heads * dv

    z_hi, z_lo = _split_bf16(z_ref[...])
    for d in range(2):
        g_hi, g_lo = _split_bf16(gw_ref[d])
        pre = _dot(z_hi, g_hi) + (_dot(z_hi, g_lo) + _dot(z_lo, g_hi)) + gb_ref[d]
        la_scr[d] = _log_sigmoid(pre) / GLA_TAU

    eye_v = (lax.broadcasted_iota(jnp.int32, (dv, dv), 0) == lax.broadcasted_iota(jnp.int32, (dv, dv), 1)).astype(F32)
    for d in range(2):
        for h in range(heads):
            if has_s0:
                s0_rows = jnp.concatenate(
                    [s0_ref[0, 0, d, h] if hh == h else jnp.zeros((dk, dv), F32) for hh in range(heads)], axis=0)
                st_scr[d, h] = _dot_nt(eye_v, s0_rows, precision=HIGHEST)
            else:
                st_scr[d, h] = jnp.zeros((dv, wk), F32)

    ng = GLA_GROUP
    gr = ng * c_len
    ti = lax.broadcasted_iota(jnp.int32, (gr, gr), 0)
    si = lax.broadcasted_iota(jnp.int32, (gr, gr), 1)
    same_chunk = (ti // c_len) == (si // c_len)
    lane_head = lax.broadcasted_iota(jnp.int32, (gr, wk), 1) // dk
    row_chunk = lax.broadcasted_iota(jnp.int32, (gr, dv), 0) // c_len

    def group(gi, d):
        reverse = d == 1
        keep = same_chunk & ((si >= ti) if reverse else (si <= ti))
        sums = jnp.concatenate([keep.astype(BF16), same_chunk.astype(BF16)], axis=0)
        rows = pl.ds(pl.multiple_of(gi * gr, gr), gr)
        la_hi, la_lo = _split_bf16(la_scr[d, rows, :])
        res = _dot(sums, la_hi) + _dot(sums, la_lo)
        b, b_end = res[:gr], res[gr:]
        q = ub_ref[rows, 0:wk].astype(F32) * (dk ** -0.5)
        k = ub_ref[rows, wk:2 * wk].astype(F32)
        qt = q * jnp.exp(b)
        kt = (k * jnp.exp(-b)).astype(BF16)
        ke = (k * jnp.exp(b_end - b)).astype(BF16)
        dec = jnp.exp(b_end)
        order = range(ng - 1, -1, -1) if reverse else range(ng)
        for h in range(heads):
            vs = slice(h * dv, (h + 1) * dv)
            qm = jnp.where(lane_head == h, qt, 0.0).astype(BF16)
            a = jnp.where(keep, _dot_nt(qm, kt), 0.0).astype(BF16)
            v_h = ub_ref[rows, 2 * wk + h * dv:2 * wk + (h + 1) * dv]
            o = _dot(a, v_h)
            v_f = v_h.astype(F32)
            v_blk = jnp.concatenate([jnp.where(row_chunk == c, v_f, 0.0).astype(BF16) for c in range(ng)], axis=1)
            ut = _dot_tn(v_blk, ke)
            st = st_scr[d, h]
            inter = [None] * ng
            for c in order:
                inter[c] = _dot_nt(qm[c * c_len:(c + 1) * c_len], st.astype(BF16))
                st = st * dec[c * c_len:c * c_len + 1, :] + ut[c * dv:(c + 1) * dv]
            st_scr[d, h] = st
            o = o + jnp.concatenate(inter, axis=0)
            if reverse:
                o_scr[rows, vs] += o
            else:
                o_scr[rows, vs] = o

    n_groups = seq_len // gr
    lax.fori_loop(0, n_groups, lambda g, carry: (group(g, 0), carry)[1], 0)
    lax.fori_loop(0, n_groups, lambda g, carry: (group(n_groups - 1 - g, 1), carry)[1], 0)

    r = ub_ref[:, 2 * wk + wv:2 * wk + 2 * wv].astype(F32)
    gn = gn_ref[...]
    for h in range(heads):
        vs = slice(h * dv, (h + 1) * dv)
        o_ref[:, vs] = (_rmsnorm(o_scr[:, vs], gn[:, vs]) * _silu(r[:, vs])).astype(BF16)
    if out_state:
        for d in range(2):
            for h in range(heads):
                st_ref[0, 0, d, h] = st_scr[d, h].T[h * dk:(h + 1) * dk, :]


def _gla(u, z, gw, gb, gn, s0, st_buf, l, *, row0, nb, seq_len, bcol, heads, dk, dv):
    wk, wv = heads * dk, heads * dv
    wb = 2 * wk + 2 * wv
    has_s0 = s0 is not None
    out_state = st_buf is not None
    rb = row0 // seq_len
    kern = functools.partial(_gla_kernel, heads=heads, dk=dk, dv=dv, seq_len=seq_len, has_s0=has_s0,
                             out_state=out_state)
    in_specs = [
        pl.BlockSpec((seq_len, wb), lambda b: (rb + b, bcol)),
        pl.BlockSpec((seq_len, LANES), lambda b: (rb + b, 0)),
        pl.BlockSpec((2, LANES, wk), lambda b: (0, 0, 0)),
        pl.BlockSpec((2, 1, wk), lambda b: (0, 0, 0)),
        pl.BlockSpec((1, wv), lambda b: (0, 0)),
    ]
    args = [u, z, gw, gb, gn]
    if has_s0:
        in_specs.append(pl.BlockSpec((1, 1, 2, heads, dk, dv), lambda b: (b, l, 0, 0, 0, 0)))
        args.append(s0)
    out_specs = [pl.BlockSpec((seq_len, wv), lambda b: (b, 0))]
    out_shape = [jax.ShapeDtypeStruct((nb * seq_len, wv), BF16)]
    aliases = {}
    if out_state:
        aliases = {len(args): 1}
        in_specs.append(pl.BlockSpec(memory_space=pl.ANY))
        args.append(st_buf)
        out_specs.append(pl.BlockSpec((1, 1, 2, heads, dk, dv), lambda b: (b, l, 0, 0, 0, 0)))
        out_shape.append(jax.ShapeDtypeStruct(st_buf.shape, st_buf.dtype))
    res = pl.pallas_call(
        kern,
        grid=(nb,),
        in_specs=in_specs,
        out_specs=out_specs,
        out_shape=out_shape,
        scratch_shapes=[
            pltpu.VMEM((2, seq_len, wk), F32),
            pltpu.VMEM((seq_len, wv), F32),
            pltpu.VMEM((2, heads, dv, wk), F32),
        ],
        input_output_aliases=aliases,
        compiler_params=_cparams(1),
        name="gla_lat" if has_s0 else "gla_ctx",
    )(*args)
    return res


ROUTE_I1, ROUTE_I2, ROUTE_W1, ROUTE_W2, ROUTE_R1, ROUTE_R2 = range(6)


def _pack_bf16_pairs(h):
    c = h.shape[1] // 2
    lo = lax.bitcast_convert_type(h[:, :c].astype(BF16).astype(F32), jnp.uint32)
    hi = lax.bitcast_convert_type(h[:, c:].astype(BF16).astype(F32), jnp.uint32)
    return hi | (lo >> 16)


def _unpack_bf16_pairs(w):
    lo = lax.bitcast_convert_type(w << 16, F32).astype(BF16)
    hi = lax.bitcast_convert_type(w & jnp.uint32(0xFFFF0000), F32).astype(BF16)
    return jnp.concatenate([lo, hi], axis=-1)


def _outproj_kernel(*refs, n_experts, top_k, nct, split_x):
    refs = list(refs)
    x_ref = refs.pop(0)
    xl_ref = refs.pop(0) if split_x else None
    ctx_refs, lat_refs, refs = refs[0:3], refs[3:6], refs[6:]
    if n_experts:
        w_ref, mod_ref, g_ref, wr_ref, br_ref, xn_ref, h_ref, route_ref, counts_ref, run_scr = refs
    else:
        w_ref, mod_ref, g_ref, xn_ref, h_ref = refs
    is_ctx = pl.program_id(0) < nct
    mixed = jnp.concatenate([jnp.where(is_ctx, c[...], t[...]) for c, t in zip(ctx_refs, lat_refs)], axis=-1)
    x = jnp.where(is_ctx, x_ref[...], xl_ref[...]) if split_x else x_ref[...]
    xn = x + mod_ref[0, 2:3, :] * _dot(mixed, w_ref[...])
    xn_ref[...] = xn
    h = _rmsnorm(xn, g_ref[...]) * (1.0 + mod_ref[0, 4:5, :]) + mod_ref[0, 3:4, :]
    if not n_experts:
        h_ref[...] = h.astype(BF16)
        return
    assert top_k == 2
    h_ref[...] = _pack_bf16_pairs(h)
    h_hi = h.astype(BF16)
    h_lo = (h - h_hi.astype(F32)).astype(BF16)
    wr = wr_ref[...]
    w_hi = wr.astype(BF16)
    w_lo = (wr - w_hi.astype(F32)).astype(BF16)
    logits = _dot(h_hi, w_hi) + (_dot(h_hi, w_lo) + _dot(h_lo, w_hi)) + br_ref[...]
    tm = logits.shape[0]
    lane = lax.broadcasted_iota(jnp.int32, logits.shape, 1)
    l1 = jnp.max(logits, axis=-1, keepdims=True)
    i1 = jnp.min(jnp.where(logits == l1, lane, LANES), axis=-1, keepdims=True)
    rest = jnp.where(lane == i1, -jnp.inf, logits)
    l2 = jnp.max(rest, axis=-1, keepdims=True)
    i2 = jnp.min(jnp.where(rest == l2, lane, LANES), axis=-1, keepdims=True)
    e2 = jnp.exp(l2 - l1)
    w1 = 1.0 / (1.0 + e2)
    w2 = e2 * w1
    @pl.when(pl.program_id(0) == 0)
    def _():
        run_scr[...] = jnp.zeros(run_scr.shape, F32)

    oh1 = lane == i1
    oh2 = lane == i2
    earlier = (lax.broadcasted_iota(jnp.int32, (tm, tm), 1) < lax.broadcasted_iota(jnp.int32, (tm, tm), 0)).astype(BF16)
    c1 = _dot(earlier, oh1.astype(BF16))
    c2 = _dot(earlier, oh2.astype(BF16))
    n1 = jnp.sum(oh1.astype(F32), axis=0, keepdims=True)
    n2 = jnp.sum(oh2.astype(F32), axis=0, keepdims=True)
    run = run_scr[0:1, :]
    r1 = jnp.sum(jnp.where(oh1, run + c1, 0.0), axis=-1, keepdims=True)
    r2 = jnp.sum(jnp.where(oh2, run + n1 + c2, 0.0), axis=-1, keepdims=True)
    total = run + n1 + n2
    run_scr[...] = jnp.broadcast_to(total, run_scr.shape)
    counts_ref[...] = jnp.broadcast_to(total, counts_ref.shape)
    rec = jnp.zeros(logits.shape, F32)
    for slot, val in ((ROUTE_I1, i1.astype(F32)), (ROUTE_I2, i2.astype(F32)), (ROUTE_W1, w1), (ROUTE_W2, w2),
                      (ROUTE_R1, r1), (ROUTE_R2, r2)):
        rec = jnp.where(lane == slot, val, rec)
    route_ref[...] = rec


def _outproj(x, x_lat, mixed_ctx, mixed_lat, w_out, l, mod_l, g, router, *, n_ctx, t_lat, tm=512):
    split_x = x_lat is not None
    d = x.shape[1]
    n = x.shape[0] + (x_lat.shape[0] if split_x else 0)
    nct = n_ctx // tm
    nlt = n // tm - nct

    def cond_idx(i):
        return jnp.where(i < nct, 0, 1 + ((i - nct) * tm) // t_lat)

    row = lambda i: (i, 0)
    const = lambda i: (0, 0)
    ctx_row = lambda i: (jnp.minimum(i, nct - 1), 0)
    lat_row = lambda i: (jnp.clip(i - nct, 0, nlt - 1), 0)
    x_specs = [pl.BlockSpec((tm, d), ctx_row), pl.BlockSpec((tm, d), lat_row)] if split_x else [pl.BlockSpec((tm, d), row)]
    x_args = [x, x_lat] if split_x else [x]
    in_specs = (
        x_specs
        + [pl.BlockSpec((tm, o.shape[1]), ctx_row) for o in mixed_ctx]
        + [pl.BlockSpec((tm, o.shape[1]), lat_row) for o in mixed_lat]
        + [pl.BlockSpec((None, d, d), lambda i: (l, 0, 0)), pl.BlockSpec((1, 6, d), lambda i: (cond_idx(i), 0, 0)),
           pl.BlockSpec((1, d), const)]
    )
    args = [*x_args, *mixed_ctx, *mixed_lat, w_out, mod_l, g]
    n_experts = 0
    scratch = []
    if router is None:
        out_specs = [pl.BlockSpec((tm, d), row), pl.BlockSpec((tm, d), row)]
        out_shape = [jax.ShapeDtypeStruct((n, d), F32), jax.ShapeDtypeStruct((n, d), BF16)]
    else:
        wr, br, n_experts = router
        in_specs += [pl.BlockSpec((d, LANES), const), pl.BlockSpec((1, LANES), const)]
        args += [wr, br]
        out_specs = [pl.BlockSpec((tm, d), row), pl.BlockSpec((tm, d // 2), row), pl.BlockSpec((tm, LANES), row),
                     pl.BlockSpec((8, LANES), const)]
        out_shape = [jax.ShapeDtypeStruct((n, d), F32), jax.ShapeDtypeStruct((n, d // 2), jnp.uint32),
                     jax.ShapeDtypeStruct((n, LANES), F32), jax.ShapeDtypeStruct((8, LANES), F32)]
        scratch = [pltpu.VMEM((8, LANES), F32)]
    kern = functools.partial(_outproj_kernel, n_experts=n_experts, top_k=2, nct=nct, split_x=split_x)
    return pl.pallas_call(
        kern,
        grid=(n // tm,),
        in_specs=in_specs,
        out_specs=out_specs,
        out_shape=out_shape,
        scratch_shapes=scratch,
        compiler_params=_cparams(1),
        name="outproj_router" if n_experts else "outproj",
    )(*args)


def _row_copy(src_ref, src_row, dst_ref, dst_row, sem):
    return pltpu.make_async_copy(src_ref.at[pl.ds(src_row, 1), :], dst_ref.at[pl.ds(dst_row, 1), :], sem)


def _dispatch_kernel(pos_ref, ztile_ref, zon_ref, hp_ref, hs_ref, zbuf, sem, zsem, *, n, tm):
    base = pl.program_id(0) * tm
    tz = zbuf.shape[0]

    @pl.when(pl.program_id(0) == 0)
    def _():
        zbuf[...] = jnp.zeros(zbuf.shape, zbuf.dtype)
        clears = [pltpu.make_async_copy(zbuf, hs_ref.at[pl.ds(pl.multiple_of(ztile_ref[t] * tz, tz), tz), :], zsem)
                  for t in range(ztile_ref.shape[0])]
        for t, cp in enumerate(clears):
            pl.when(zon_ref[t] == 1)(cp.start)
        for t, cp in enumerate(clears):
            pl.when(zon_ref[t] == 1)(cp.wait)

    def copies(k):
        return (_row_copy(hp_ref, k, hs_ref, pos_ref[base + k], sem),
                _row_copy(hp_ref, k, hs_ref, pos_ref[n + base + k], sem))

    def start(k, carry):
        for queue, cp in enumerate(copies(k)):
            cp.start(priority=queue)
        return carry

    def wait(k, carry):
        for cp in copies(k):
            cp.wait()
        return carry

    lax.fori_loop(0, tm, start, 0, unroll=8)
    lax.fori_loop(0, tm, wait, 0, unroll=8)


def _dispatch(pos, clear_tiles, clear_on, hp, n_tiles, tile_rows, tm=512):
    n, c = hp.shape
    n_rows_sorted = n_tiles * tile_rows
    return pl.pallas_call(
        functools.partial(_dispatch_kernel, n=n, tm=tm),
        grid_spec=pltpu.PrefetchScalarGridSpec(
            num_scalar_prefetch=3,
            grid=(n // tm,),
            in_specs=[pl.BlockSpec((tm, c), lambda i, pos, zt, zo: (i, 0))],
            out_specs=pl.BlockSpec(memory_space=pl.ANY),
            scratch_shapes=[pltpu.VMEM((tile_rows, c), hp.dtype), pltpu.SemaphoreType.DMA(()),
                            pltpu.SemaphoreType.DMA(())],
        ),
        out_shape=jax.ShapeDtypeStruct((n_rows_sorted, c), hp.dtype),
        compiler_params=_cparams(1),
        name="moe_dispatch",
    )(pos, clear_tiles, clear_on, hp)


def _moe_ffn_kernel(texp_ref, trow_ref, nact_ref, hs_ref, wg_ref, wu_ref, wd_ref, ys_ref, h_scr, acc_scr):
    del texp_ref, trow_ref
    i = pl.program_id(0)
    f = pl.program_id(1)
    last = pl.num_programs(1) - 1
    active = i < nact_ref[0]

    @pl.when(active & (f == 0))
    def _():
        h_scr[...] = _unpack_bf16_pairs(hs_ref[...])

    def contribution():
        h = h_scr[...]
        t = _silu(_dot(h, wg_ref[...])) * _dot(h, wu_ref[...])
        return _dot(t.astype(BF16), wd_ref[...])

    @pl.when(active & (f == 0))
    def _():
        acc_scr[...] = contribution()

    @pl.when(active & (f > 0) & (f < last))
    def _():
        acc_scr[...] += contribution()

    @pl.when(active & (f == last))
    def _():
        ys_ref[...] = _pack_bf16_pairs(acc_scr[...] + contribution())

    @pl.when(jnp.logical_not(active) & (f == 0))
    def _():
        ys_ref[...] = jnp.zeros(ys_ref.shape, ys_ref.dtype)


def _moe_ffn(tile_exp, tile_row, n_active, hs, wg, wu, wd, li, *, tm, nf=2):
    r, c = hs.shape
    d, ff = wg.shape[2:]
    tf = ff // nf
    assert nf >= 2, "the first and the last hidden tile are handled by different branches"

    def f_eff(i, f, nact):
        return jnp.where(i < nact[0], f, nf - 1)

    return pl.pallas_call(
        _moe_ffn_kernel,
        grid_spec=pltpu.PrefetchScalarGridSpec(
            num_scalar_prefetch=3,
            grid=(r // tm, nf),
            in_specs=[
                pl.BlockSpec((tm, c), lambda i, f, te, tr, na: (tr[i], 0)),
                pl.BlockSpec((None, None, d, tf), lambda i, f, te, tr, na: (li, te[i], 0, f_eff(i, f, na))),
                pl.BlockSpec((None, None, d, tf), lambda i, f, te, tr, na: (li, te[i], 0, f_eff(i, f, na))),
                pl.BlockSpec((None, None, tf, d), lambda i, f, te, tr, na: (li, te[i], f_eff(i, f, na), 0)),
            ],
            out_specs=pl.BlockSpec((tm, d // 2), lambda i, f, te, tr, na: (i, 0)),
            scratch_shapes=[pltpu.VMEM((tm, d), BF16), pltpu.VMEM((tm, d), F32)],
        ),
        out_shape=jax.ShapeDtypeStruct((r, d // 2), jnp.uint32),
        compiler_params=_cparams(2),
        name="moe_ffn",
    )(tile_exp, tile_row, n_active, hs, wg, wu, wd)


def _combine_kernel(*refs, n, tm, final_norm, nct):
    refs = list(refs)
    pos_ref, ys_ref, x_ref, route_ref, mod_ref = refs[:5]
    refs = refs[5:]
    fg_ref = refs.pop(0) if final_norm else None
    o_ref = refs.pop(0)
    ol_ref = refs.pop(0) if final_norm else None
    buf, sem = refs
    base = pl.program_id(0) * tm

    def copies(k):
        return (_row_copy(ys_ref, pos_ref[base + k], buf.at[0], k, sem),
                _row_copy(ys_ref, pos_ref[n + base + k], buf.at[1], k, sem))

    def start(k, carry):
        for queue, cp in enumerate(copies(k)):
            cp.start(priority=queue)
        return carry

    def wait(k, carry):
        for cp in copies(k):
            cp.wait()
        return carry

    lax.fori_loop(0, tm, start, 0, unroll=8)
    lax.fori_loop(0, tm, wait, 0, unroll=8)
    route = route_ref[...]
    lane = lax.broadcasted_iota(jnp.int32, route.shape, 1)
    w1 = jnp.sum(jnp.where(lane == ROUTE_W1, route, 0.0), axis=-1, keepdims=True)
    w2 = jnp.sum(jnp.where(lane == ROUTE_W2, route, 0.0), axis=-1, keepdims=True)
    y1 = _unpack_bf16_pairs(buf[0]).astype(F32)
    y2 = _unpack_bf16_pairs(buf[1]).astype(F32)
    y = x_ref[...] + mod_ref[0, 5:6, :] * (w1 * y1 + w2 * y2)
    if not final_norm:
        o_ref[...] = y
        return
    y = _rmsnorm(y, fg_ref[...])
    is_ctx = pl.program_id(0) < nct

    @pl.when(is_ctx)
    def _():
        o_ref[...] = y

    @pl.when(jnp.logical_not(is_ctx))
    def _():
        ol_ref[...] = y


def _combine(pos, ys, x, route, mod_l, final_g, *, n_ctx, t_lat, tm=512):
    n, d = x.shape
    nct = n_ctx // tm
    nlt = n // tm - nct

    def cond_idx(i):
        return jnp.where(i < nct, 0, 1 + ((i - nct) * tm) // t_lat)

    in_specs = [
        pl.BlockSpec(memory_space=pl.ANY),
        pl.BlockSpec((tm, d), lambda i, pos: (i, 0)),
        pl.BlockSpec((tm, LANES), lambda i, pos: (i, 0)),
        pl.BlockSpec((1, 6, d), lambda i, pos: (cond_idx(i), 0, 0)),
    ]
    args = [ys, x, route, mod_l]
    if final_g is None:
        out_specs = pl.BlockSpec((tm, d), lambda i, pos: (i, 0))
        out_shape = jax.ShapeDtypeStruct((n, d), F32)
    else:
        in_specs.append(pl.BlockSpec((1, d), lambda i, pos: (0, 0)))
        args.append(final_g)
        out_specs = [pl.BlockSpec((tm, d), lambda i, pos: (jnp.minimum(i, nct - 1), 0)),
                     pl.BlockSpec((tm, d), lambda i, pos: (jnp.clip(i - nct, 0, nlt - 1), 0))]
        out_shape = [jax.ShapeDtypeStruct((n_ctx, d), F32), jax.ShapeDtypeStruct((n - n_ctx, d), F32)]
    return pl.pallas_call(
        functools.partial(_combine_kernel, n=n, tm=tm, final_norm=final_g is not None, nct=nct),
        grid_spec=pltpu.PrefetchScalarGridSpec(
            num_scalar_prefetch=1,
            grid=(n // tm,),
            in_specs=in_specs,
            out_specs=out_specs,
            scratch_shapes=[pltpu.VMEM((2, tm, d // 2), jnp.uint32), pltpu.SemaphoreType.DMA(())],
        ),
        out_shape=out_shape,
        compiler_params=_cparams(1),
        name="moe_combine",
    )(pos, *args)


def _routing_tables(route, counts, n_experts, tm):
    n = route.shape[0]
    n_tiles = -(-2 * n // tm) + n_experts
    cnt = counts[0, :n_experts].astype(jnp.int32)
    tiles_e = (cnt + tm - 1) // tm
    tile_end = jnp.cumsum(tiles_e)
    offs = (tile_end - tiles_e) * tm
    i1 = route[:, ROUTE_I1].astype(jnp.int32)
    i2 = route[:, ROUTE_I2].astype(jnp.int32)
    eidx = jnp.arange(n_experts, dtype=jnp.int32)[None, :]
    off1 = jnp.sum(jnp.where(i1[:, None] == eidx, offs[None, :], 0), axis=1)
    off2 = jnp.sum(jnp.where(i2[:, None] == eidx, offs[None, :], 0), axis=1)
    pos = jnp.concatenate([off1 + route[:, ROUTE_R1].astype(jnp.int32), off2 + route[:, ROUTE_R2].astype(jnp.int32)])
    n_active = tile_end[-1]
    t = jnp.arange(n_tiles, dtype=jnp.int32)
    t_eff = jnp.minimum(t, n_active - 1)
    tile_exp = jnp.sum((t_eff[:, None] >= tile_end[None, :]).astype(jnp.int32), axis=1)
    tails = jnp.maximum(tile_end - 1, 0)
    trailing = n_tiles - 1 - jnp.arange(n_experts, dtype=jnp.int32)
    clear_tiles = jnp.concatenate([tails, trailing]).astype(jnp.int32)
    clear_on = jnp.concatenate([tiles_e > 0, trailing >= n_active]).astype(jnp.int32)
    return (pos, clear_tiles, clear_on, tile_exp.astype(jnp.int32), t_eff.astype(jnp.int32),
            n_active.reshape(1).astype(jnp.int32), n_tiles)


def _ffn_kernel(*refs, final_norm):
    refs = list(refs)
    h_ref, x_ref, wg_ref, wu_ref, wd_ref, mod_ref = refs[:6]
    refs = refs[6:]
    fg_ref = refs.pop(0) if final_norm else None
    (o_ref,) = refs
    f = pl.program_id(1)

    @pl.when(f == 0)
    def _():
        o_ref[...] = jnp.zeros(o_ref.shape, F32)

    h = h_ref[...]
    t = _silu(_dot(h, wg_ref[...])) * _dot(h, wu_ref[...])
    o_ref[...] += _dot(t.astype(BF16), wd_ref[...])

    @pl.when(f == pl.num_programs(1) - 1)
    def _():
        y = x_ref[...] + mod_ref[0, 5:6, :] * o_ref[...]
        if final_norm:
            y = _rmsnorm(y, fg_ref[...])
        o_ref[...] = y


def _ffn(h, x, wg, wu, wd, li, mod_l, final_g, *, n_ctx, t_lat, tm=512):
    n, d = x.shape
    tf = FFN_TF
    nf = wg.shape[2] // tf
    nct = n_ctx // tm

    def cond_idx(i):
        return jnp.where(i < nct, 0, 1 + ((i - nct) * tm) // t_lat)

    row = lambda i, f: (i, 0)
    in_specs = [
        pl.BlockSpec((tm, d), row),
        pl.BlockSpec((tm, d), row),
        pl.BlockSpec((None, d, tf), lambda i, f: (li, 0, f)),
        pl.BlockSpec((None, d, tf), lambda i, f: (li, 0, f)),
        pl.BlockSpec((None, tf, d), lambda i, f: (li, f, 0)),
        pl.BlockSpec((1, 6, d), lambda i, f: (cond_idx(i), 0, 0)),
    ]
    args = [h, x, wg, wu, wd, mod_l]
    if final_g is not None:
        in_specs.append(pl.BlockSpec((1, d), lambda i, f: (0, 0)))
        args.append(final_g)
    return pl.pallas_call(
        functools.partial(_ffn_kernel, final_norm=final_g is not None),
        grid=(n // tm, nf),
        in_specs=in_specs,
        out_specs=pl.BlockSpec((tm, d), row),
        out_shape=jax.ShapeDtypeStruct((n, d), F32),
        compiler_params=_cparams(2),
        name="ffn_dense",
    )(*args)


def kernel(x_prompt, x_sample, cache_nat_k, cache_nat_v, state_gla, cache_swa_k, cache_swa_v, c, c_ctx, w_ada, b_ada,
           norm_mix_g, norm_ffn_g, w_in, na_rpb, gla_w_gate, gla_b_gate, gla_norm_g, swa_sink, w_out, ffn_w_gate,
           ffn_w_up, ffn_w_down, moe_w_router, moe_b_router, moe_w_gate, moe_w_up, moe_w_down, final_norm_g):
    nb, seq, d = x_prompt.shape
    db, t_lat, _ = x_sample.shape
    depth = w_in.shape[0]
    p_len, h_a = cache_nat_k.shape[2], cache_nat_k.shape[3]
    h_b, dk_b, dv_b = state_gla.shape[3:6]
    hkv_c, g_c = swa_sink.shape[1], swa_sink.shape[2]
    rank = gla_w_gate.shape[2]
    n_exp = moe_w_router.shape[-1]
    w_a = h_a * HEAD_DIM
    w_bk = h_b * dk_b
    w_bv = h_b * dv_b
    w_cq = hkv_c * g_c * HEAD_DIM
    w_ck = hkv_c * HEAD_DIM
    n_ctx = nb * seq
    n_lat = db * t_lat
    n = n_ctx + n_lat
    z0 = 3 * w_a + 2 * w_bk + 2 * w_bv
    z1 = z0 + 2 * rank
    a_w = 3 * w_a
    b_w = 2 * w_bk + 2 * w_bv
    assert a_w == b_w and (a_w + b_w) % w_cq == 0 and (a_w + b_w + w_cq) % w_ck == 0 and 2 * rank <= LANES
    assert db + 1 <= 8

    x, x_lat = x_prompt.reshape(n_ctx, d), x_sample.reshape(n_lat, d)
    cond8 = jnp.concatenate([c_ctx[None, :], c, jnp.zeros((7 - db, d), F32)], axis=0)
    mod = _adaln(cond8, w_ada, b_ada).reshape(depth, 8, 6, d)

    cos, sin = _rope_tables(t_lat, w_ck)
    na_tab = _na_bias_tables(na_rpb)
    kc_a = cache_nat_k.reshape(db, depth, p_len, w_a)
    vc_a = cache_nat_v.reshape(db, depth, p_len, w_a)
    kc_c = cache_swa_k.reshape(db, depth, p_len, w_ck)
    vc_c = cache_swa_v.reshape(db, depth, p_len, w_ck)
    sizes = dict(n_ctx=n_ctx, t_lat=t_lat)

    kv_bufs = [jnp.zeros((nb, depth, seq, w), F32) for w in (w_a, w_a, w_ck, w_ck)]
    gla_st = jnp.zeros((nb, depth, 2, h_b, dk_b, dv_b), F32)
    n_main = w_in.shape[2] - (z1 - z0)
    before_z = lax.broadcasted_iota(jnp.int32, (1, 1, n_main), 2) < z0
    w_main = jnp.where(before_z, w_in[:, :, :n_main], w_in[:, :, z1 - z0:]).astype(BF16)
    wz = jnp.pad(w_in[:, :, z0:z1], ((0, 0), (0, 0), (0, LANES - 2 * rank))).astype(BF16)
    w_out_b = w_out.astype(BF16)
    ffn_wg, ffn_wu, ffn_wd = ffn_w_gate.astype(BF16), ffn_w_up.astype(BF16), ffn_w_down.astype(BF16)
    moe_wg, moe_wu, moe_wd = moe_w_gate.astype(BF16), moe_w_up.astype(BF16), moe_w_down.astype(BF16)
    for l in range(depth):
        u, z, *kv_bufs = _inproj(x, x_lat, mod[l], norm_mix_g[l][None, :], w_main, wz, kv_bufs, l, seq=seq, w_a=w_a,
                                 w_cq=w_cq, w_kvc=w_ck, tm=INPROJ_TM_SPLIT if x_lat is not None else INPROJ_TM,
                                 **sizes)

        oa_c = _ctx_attn(u, None, nb=nb, seq=seq, qcol=0, kcol=1, vcol=2, wq=w_a, wk=w_a, hkv=h_a, g=1)
        oa_l = _na_attn(u, kc_a, vc_a, na_tab, l, n_ctx=n_ctx, db=db, t_lat=t_lat, w_a=w_a, heads=h_a)
        gw = jnp.zeros((2, LANES, w_bk), F32)
        gw = gw.at[0, 0:rank].set(gla_w_gate[l, 0]).at[1, rank:2 * rank].set(gla_w_gate[l, 1])
        gb = gla_b_gate[l][:, None, :]
        gn = gla_norm_g[l].reshape(1, w_bv)
        gla_args = dict(bcol=a_w // b_w, heads=h_b, dk=dk_b, dv=dv_b)
        ob_c, gla_st = _gla(u, z, gw, gb, gn, None, gla_st, l, row0=0, nb=nb, seq_len=seq, **gla_args)
        (ob_l,) = _gla(u, z, gw, gb, gn, state_gla, None, l, row0=n_ctx, nb=db, seq_len=t_lat, **gla_args)
        c0 = a_w + b_w
        cols = dict(qcol=c0 // w_cq, kcol=(c0 + w_cq) // w_ck, vcol=(c0 + w_cq) // w_ck + 1, hkv=hkv_c, g=g_c)
        oc_c = _ctx_attn(u, swa_sink[l], nb=nb, seq=seq, wq=w_cq, wk=w_ck, **cols)
        oc_l = _swa_attn(u, swa_sink[l], kc_c, vc_c, cos, sin, l, n_ctx=n_ctx, db=db, t_lat=t_lat, **cols)

        moe = l % 2 == 1
        i = l // 2
        router = None
        if moe:
            wr = jnp.pad(moe_w_router[i], ((0, 0), (0, LANES - n_exp)))
            br = jnp.concatenate([moe_b_router[i], jnp.full((LANES - n_exp,), NEG_INF, F32)])[None, :]
            router = (wr, br, n_exp)
        res = _outproj(x, x_lat, (oa_c, ob_c, oc_c), (oa_l, ob_l, oc_l), w_out_b, l, mod[l],
                       norm_ffn_g[l][None, :], router, **sizes)
        x_lat = None
        fg = final_norm_g[None, :] if l == depth - 1 else None
        if moe:
            xn, hp, route, counts = res
            tm_e = MOE_TILE_ROWS
            pos, clear_tiles, clear_on, tile_exp, tile_row, n_active, n_tiles = _routing_tables(route, counts, n_exp,
                                                                                                tm_e)
            hs = _dispatch(pos, clear_tiles, clear_on, hp, n_tiles, tm_e)
            ys = _moe_ffn(tile_exp, tile_row, n_active, hs, moe_wg, moe_wu, moe_wd, i, tm=tm_e)
            x = _combine(pos, ys, xn, route, mod[l], fg, **sizes)
        else:
            xn, h2 = res
            x = _ffn(h2, xn, ffn_wg, ffn_wu, ffn_wd, i, mod[l], fg, tm=FFN_TM, **sizes)

    y_ctx, y_lat = x if isinstance(x, (list, tuple)) else (x[:n_ctx], x[n_ctx:])
    nat_k, nat_v, swa_k, swa_v = kv_bufs
    return (y_ctx.reshape(nb, seq, d), y_lat.reshape(db, t_lat, d),
            nat_k.reshape(nb, depth, seq, h_a, HEAD_DIM), nat_v.reshape(nb, depth, seq, h_a, HEAD_DIM), gla_st,
            swa_k.reshape(nb, depth, seq, hkv_c, HEAD_DIM), swa_v.reshape(nb, depth, seq, hkv_c, HEAD_DIM))
```
